```python
import jax
import jax.numpy as jnp
from jax import lax
import numpy as np


D_MODEL = 1024
BATCH = 8
SEQ = 2048
DEPTH = 2

GRID_W = 64
CTX_LEN = 256
N_BRANCH = 4
BRANCH_W = 512

LRU_W = 512
LRU_BLOCKS = 8
LRU_BW = LRU_W // LRU_BLOCKS
LRU_CONV = 4
LRU_C = 8.0

GQA_HQ = 8
GQA_HKV = 2
GQA_DH = 64
WINDOW = 128
ATT_BLOCK = 128

MLA_H = 8
MLA_RQ = 384
MLA_RKV = 256
MLA_DN = 64
MLA_DR = 32
MLA_DV = 64

RET_H = 4
RET_DK = 128
RET_DV = 128
RET_CHUNK = 128

FF = 2816

ROPE_BASE = 10000.0
LN_EPS = 1e-5
MASK_VALUE = -1e30
DN_ALPHA = (2 * DEPTH) ** 0.25
DN_BETA = (8 * DEPTH) ** -0.25

IN_SIZES = (LRU_W, LRU_W, GQA_HQ * GQA_DH, GQA_HKV * GQA_DH, GQA_HKV * GQA_DH, MLA_RQ, MLA_RKV, MLA_DR, RET_H * RET_DK, RET_H * RET_DK, RET_H * RET_DV, RET_H * RET_DV, N_BRANCH * D_MODEL)
IN_COLS = sum(IN_SIZES)

kernel_name = 'hybrid_gated_parallel_diffusion_block'


def _split_cols(p):
    cuts = np.cumsum(np.array(IN_SIZES))[:-1].tolist()
    return jnp.split(p, cuts, axis=-1)


def _layer_norm(x, g, b):
    xf = x.astype(jnp.float32)
    mu = jnp.mean(xf, -1, keepdims=True)
    var = jnp.mean(jnp.square(xf - mu), -1, keepdims=True)
    y = (xf - mu) * lax.rsqrt(var + LN_EPS)
    return (y * g.astype(jnp.float32) + b.astype(jnp.float32)).astype(x.dtype)


def _rms_norm(x, g):
    xf = x.astype(jnp.float32)
    y = xf * lax.rsqrt(jnp.mean(jnp.square(xf), -1, keepdims=True) + LN_EPS)
    return (y * g.astype(jnp.float32)).astype(x.dtype)


def _axial_rope(x, rows, cols):
    dim = x.shape[-1]
    da = dim // 2
    nf = da // 2
    inv = ROPE_BASE ** (-jnp.arange(nf, dtype=jnp.float32) / nf)
    parts = []
    for pos, seg in ((rows, x[..., :da]), (cols, x[..., da:])):
        ang = pos.astype(jnp.float32)[:, None] * inv[None, :]
        cos = jnp.cos(ang)[:, None, :]
        sin = jnp.sin(ang)[:, None, :]
        sf = seg.astype(jnp.float32)
        s1, s2 = sf[..., :nf], sf[..., nf:]
        parts += [s1 * cos - s2 * sin, s2 * cos + s1 * sin]
    return jnp.concatenate(parts, -1).astype(x.dtype)


def _swiglu(h, w_in, w_out):
    a, b = jnp.split(h @ w_in, 2, axis=-1)
    return (jax.nn.silu(a) * b) @ w_out


def _half_ffn(h, shift, scale, gate, w1, w2, g, b):
    y = _swiglu(h * (1 + scale) + shift, w1, w2)
    return _layer_norm(DN_ALPHA * h + 0.5 * gate * y, g, b)


def _centred_dwconv(x, w, b):
    T = x.shape[1]
    left = LRU_CONV // 2
    xp = jnp.pad(x, ((0, 0), (left, LRU_CONV - 1 - left), (0, 0)))
    y = b
    for j in range(LRU_CONV):
        y = y + xp[:, j:j + T] * w[j]
    return y


def _rglru_coeffs(x, w_a, b_a, w_x, b_x, lam):
    B, T, _ = x.shape
    xb = x.reshape(B, T, LRU_BLOCKS, LRU_BW)
    r = jax.nn.sigmoid(jnp.einsum('btnd,nde->btne', xb, w_a).reshape(B, T, LRU_W) + b_a).astype(jnp.float32)
    i = jax.nn.sigmoid(jnp.einsum('btnd,nde->btne', xb, w_x).reshape(B, T, LRU_W) + b_x).astype(jnp.float32)
    log_a = -LRU_C * r * jax.nn.softplus(-lam.astype(jnp.float32))
    a = jnp.exp(log_a)
    bt = jnp.sqrt(1.0 - jnp.exp(2.0 * log_a)) * (i * x.astype(jnp.float32))
    return a, bt


def _linear_scan(a, b, h0):
    def comb(left, right):
        return (left[0] * right[0], right[0] * left[1] + right[1])
    acum, h = lax.associative_scan(comb, (a, b), axis=1)
    return h + acum * h0[:, None, :]


def _gqa_window(q, k, v, kc, vc, sink):
    B, S = q.shape[:2]
    G = GQA_HQ // GQA_HKV
    blk = ATT_BLOCK
    nb = S // blk
    Lc = kc.shape[1]
    scale = GQA_DH ** -0.5
    qb = q.reshape(B, nb, blk, GQA_HKV, G, GQA_DH)

    def windows(t):
        tp = jnp.pad(t, ((0, 0), (blk, blk), (0, 0), (0, 0))).reshape(B, nb + 2, blk, GQA_HKV, GQA_DH)
        return jnp.concatenate([tp[:, :nb], tp[:, 1:nb + 1], tp[:, 2:]], axis=2)

    kw, vw = windows(k), windows(v)
    s_win = jnp.einsum('bnqhgd,bnkhd->bnhgqk', qb, kw).astype(jnp.float32) * scale
    blocks = jnp.arange(nb)
    qpos = blocks[:, None] * blk + jnp.arange(blk)[None, :]
    kpos = (blocks[:, None] - 1) * blk + jnp.arange(3 * blk)[None, :]
    dist = qpos[:, :, None] - kpos[:, None, :]
    valid = (jnp.abs(dist) <= WINDOW) & (kpos[:, None, :] >= 0) & (kpos[:, None, :] < S)
    s_win = jnp.where(valid[None, :, None, None], s_win, MASK_VALUE)
    s_ctx = jnp.einsum('bnqhgd,bchd->bnhgqc', qb, kc).astype(jnp.float32) * scale
    s_sink = jnp.broadcast_to(sink.astype(jnp.float32).reshape(1, 1, GQA_HKV, G, 1, 1), s_ctx.shape[:-1] + (1,))
    p = jax.nn.softmax(jnp.concatenate([s_win, s_ctx, s_sink], -1), axis=-1).astype(v.dtype)
    nw = 3 * blk
    o = jnp.einsum('bnhgqk,bnkhd->bnqhgd', p[..., :nw], vw) + jnp.einsum('bnhgqc,bchd->bnqhgd', p[..., nw:nw + Lc], vc)
    return o.reshape(B, S, GQA_HQ * GQA_DH)


def _gqa_ctx(q, k, v, sink):
    B, L = q.shape[:2]
    G = GQA_HQ // GQA_HKV
    qg = q.reshape(B, L, GQA_HKV, G, GQA_DH)
    s = jnp.einsum('bqhgd,bkhd->bhgqk', qg, k).astype(jnp.float32) * (GQA_DH ** -0.5)
    s_sink = jnp.broadcast_to(sink.astype(jnp.float32).reshape(1, GQA_HKV, G, 1, 1), s.shape[:-1] + (1,))
    p = jax.nn.softmax(jnp.concatenate([s, s_sink], -1), axis=-1).astype(v.dtype)
    o = jnp.einsum('bhgqk,bkhd->bqhgd', p[..., :L], v)
    return o.reshape(B, L, GQA_HQ * GQA_DH)


def _mla_q(cq, g, w_uq, rows, cols):
    B, T, _ = cq.shape
    q = (_rms_norm(cq, g) @ w_uq).reshape(B, T, MLA_H, MLA_DN + MLA_DR)
    qn, qr = q[..., :MLA_DN], q[..., MLA_DN:]
    if rows is not None:
        qr = _axial_rope(qr, rows, cols)
    return jnp.concatenate([qn, qr], -1)


def _mla_kv(ckv, kr, g, w_ukv, rows, cols):
    B, T, _ = ckv.shape
    kv = (_rms_norm(ckv, g) @ w_ukv).reshape(B, T, MLA_H, MLA_DN + MLA_DV)
    kn, v = kv[..., :MLA_DN], kv[..., MLA_DN:]
    kr = kr[:, :, None, :]
    if rows is not None:
        kr = _axial_rope(kr, rows, cols)
    k = jnp.concatenate([kn, jnp.broadcast_to(kr, (B, T, MLA_H, MLA_DR))], -1)
    return k, v


def _dense_attend(q, k, v):
    s = jnp.einsum('bqhd,bkhd->bhqk', q, k).astype(jnp.float32) * ((MLA_DN + MLA_DR) ** -0.5)
    p = jax.nn.softmax(s, axis=-1).astype(v.dtype)
    return jnp.einsum('bhqk,bkhd->bqhd', p, v)


def _ret_heads(t, n_dim, rows, cols):
    B, T, _ = t.shape
    t = t.reshape(B, T, RET_H, n_dim)
    if rows is not None:
        t = _axial_rope(t, rows, cols)
    return t.astype(jnp.float32)


def _retention_chunkwise(q, k, v, gamma, state0, strict):
    B, T, H, DK = q.shape
    DV = v.shape[-1]
    C = RET_CHUNK
    nc = T // C
    qc = q.reshape(B, nc, C, H, DK)
    kc = k.reshape(B, nc, C, H, DK)
    vc = v.reshape(B, nc, C, H, DV)
    lg = jnp.log(gamma)
    pos = jnp.arange(C, dtype=jnp.float32)
    diff = pos[:, None] - pos[None, :]
    keep = (diff > 0) if strict else (diff >= 0)
    dmask = jnp.where(keep[None], jnp.exp(jnp.where(keep, diff, 0.0)[None] * lg[:, None, None]), 0.0)
    s = jnp.einsum('bnihd,bnjhd->bnhij', qc, kc) * dmask
    intra = jnp.einsum('bnhij,bnjhe->bnihe', s, vc)
    zeta = jnp.exp((C - 1 - pos)[:, None] * lg[None, :])
    kv = jnp.einsum('bnjhd,bnjhe->nbhde', kc * zeta[:, :, None], vc)
    decay = jnp.exp(C * lg)[None, :, None, None]

    def step(R, kv_n):
        return decay * R + kv_n, R

    final, before = lax.scan(step, state0, kv)
    xi = jnp.exp((pos + 1)[:, None] * lg[None, :])
    cross = jnp.einsum('bnihd,nbhde->bnihe', qc * xi[:, :, None], before)
    return (intra + cross).reshape(B, T, H, DV), final


def _retention_state(k, v, gamma):
    T = k.shape[1]
    w = jnp.exp((T - 1 - jnp.arange(T, dtype=jnp.float32))[:, None] * jnp.log(gamma)[None, :])
    return jnp.einsum('bthd,th,bthe->bhde', k, w, v)


def _head_group_norm(y, g):
    B, T, H, DV = y.shape
    mu = jnp.mean(y, -1, keepdims=True)
    var = jnp.mean(jnp.square(y - mu), -1, keepdims=True)
    yn = (y - mu) * lax.rsqrt(var + LN_EPS)
    return yn.reshape(B, T, H * DV) * g.astype(jnp.float32)


def _gated_merge(branches, gates, w_branch, w_out):
    B, T = gates.shape[:2]
    br = jnp.stack([b.astype(gates.dtype) for b in branches], axis=2)
    g = jax.nn.sigmoid(gates.reshape(B, T, N_BRANCH, D_MODEL))
    proj = jnp.einsum('btkw,kwd->btkd', br, w_branch)
    return jnp.einsum('btkd,btkd->btd', g, proj) @ w_out


def _mixer(hl, hc, rows, cols, need_ctx, w_in, conv_w, conv_b, lru_w_a, lru_b_a, lru_w_x, lru_b_x, lru_lambda, sink, q_norm, kv_norm, w_uq, w_ukv, ret_decay, gn_g, w_branch, w_out):
    B, S, _ = hl.shape
    Lc = hc.shape[1]
    dt = hl.dtype
    f32 = jnp.float32
    (ax_l, ay_l, bq_l, bk_l, bv_l, cq_l, ckv_l, ckr_l, dq_l, dk_l, dv_l, dg_l, gt_l) = _split_cols(hl @ w_in)
    (ax_c, ay_c, bq_c, bk_c, bv_c, cq_c, ckv_c, ckr_c, dq_c, dk_c, dv_c, dg_c, gt_c) = _split_cols(hc @ w_in)

    def flip(t):
        return t[:, ::-1]

    xa_l = _centred_dwconv(ax_l, conv_w, conv_b)
    xa_c = _centred_dwconv(ax_c, conv_w, conv_b)
    hs_l, hs_c = [], []
    for d in range(2):
        prm = (lru_w_a[d], lru_b_a[d], lru_w_x[d], lru_b_x[d], lru_lambda[d])
        src_c = flip(xa_c) if d == 1 else xa_c
        src_l = flip(xa_l) if d == 1 else xa_l
        st_c = _linear_scan(*_rglru_coeffs(src_c, *prm), jnp.zeros((B, LRU_W), f32))
        st_l = _linear_scan(*_rglru_coeffs(src_l, *prm), st_c[:, -1])
        hs_l.append(flip(st_l) if d == 1 else st_l)
        if need_ctx:
            hs_c.append(flip(st_c) if d == 1 else st_c)
    out_a_l = (hs_l[0] + hs_l[1]).astype(dt) * jax.nn.gelu(ay_l)

    qb_l = _axial_rope(bq_l.reshape(B, S, GQA_HQ, GQA_DH), rows, cols)
    kb_l = _axial_rope(bk_l.reshape(B, S, GQA_HKV, GQA_DH), rows, cols)
    vb_l = bv_l.reshape(B, S, GQA_HKV, GQA_DH)
    kb_c = bk_c.reshape(B, Lc, GQA_HKV, GQA_DH)
    vb_c = bv_c.reshape(B, Lc, GQA_HKV, GQA_DH)
    out_b_l = _gqa_window(qb_l, kb_l, vb_l, kb_c, vb_c, sink)

    k_c, v_c = _mla_kv(ckv_c, ckr_c, kv_norm, w_ukv, None, None)
    k_l, v_l = _mla_kv(ckv_l, ckr_l, kv_norm, w_ukv, rows, cols)
    q_l = _mla_q(cq_l, q_norm, w_uq, rows, cols)
    k_all = jnp.concatenate([k_c, k_l], axis=1)
    v_all = jnp.concatenate([v_c, v_l], axis=1)
    nb = S // ATT_BLOCK
    qblk = q_l.reshape(B, nb, ATT_BLOCK, MLA_H, MLA_DN + MLA_DR).transpose(1, 0, 2, 3, 4)
    o = lax.map(lambda qi: _dense_attend(qi, k_all, v_all), qblk)
    out_c_l = o.transpose(1, 0, 2, 3, 4).reshape(B, S, MLA_H * MLA_DV)

    gam = jax.nn.sigmoid(ret_decay.astype(f32))
    kd_c = _ret_heads(dk_c, RET_DK, None, None) * (RET_DK ** -0.5)
    vd_c = _ret_heads(dv_c, RET_DV, None, None)
    zero = jnp.zeros((B, RET_H, RET_DK, RET_DV), f32)
    if need_ctx:
        qd_c = _ret_heads(dq_c, RET_DK, None, None)
        oc_f, st_f = _retention_chunkwise(qd_c, kd_c, vd_c, gam[0], zero, False)
        oc_b, st_b = _retention_chunkwise(flip(qd_c), flip(kd_c), flip(vd_c), gam[1], zero, True)
    else:
        st_f = _retention_state(kd_c, vd_c, gam[0])
        st_b = _retention_state(flip(kd_c), flip(vd_c), gam[1])
    qd_l = _ret_heads(dq_l, RET_DK, rows, cols)
    kd_l = _ret_heads(dk_l, RET_DK, rows, cols) * (RET_DK ** -0.5)
    vd_l = _ret_heads(dv_l, RET_DV, None, None)
    ol_f, _ = _retention_chunkwise(qd_l, kd_l, vd_l, gam[0], st_f, False)
    ol_b, _ = _retention_chunkwise(flip(qd_l), flip(kd_l), flip(vd_l), gam[1], st_b, True)
    out_d_l = jax.nn.silu(dg_l) * _head_group_norm(ol_f + flip(ol_b), gn_g).astype(dt)

    y_l = _gated_merge([out_a_l, out_b_l, out_c_l, out_d_l], gt_l, w_branch, w_out)
    if not need_ctx:
        return y_l, None

    out_a_c = (hs_c[0] + hs_c[1]).astype(dt) * jax.nn.gelu(ay_c)
    out_b_c = _gqa_ctx(bq_c.reshape(B, Lc, GQA_HQ, GQA_DH), kb_c, vb_c, sink)
    q_c = _mla_q(cq_c, q_norm, w_uq, None, None)
    out_c_c = _dense_attend(q_c, k_c, v_c).reshape(B, Lc, MLA_H * MLA_DV)
    out_d_c = jax.nn.silu(dg_c) * _head_group_norm(oc_f + flip(oc_b), gn_g).astype(dt)
    y_c = _gated_merge([out_a_c, out_b_c, out_c_c, out_d_c], gt_c, w_branch, w_out)
    return y_l, y_c


def setup_inputs(seed: int = 0) -> dict:
    key = jax.random.key(seed)
    ks = jax.random.split(key, 32)
    f32 = jnp.float32

    def nrm(k, shape, scale):
        return jax.random.normal(k, shape, f32) * scale

    x = nrm(ks[0], (BATCH, SEQ, D_MODEL), 1.0)
    c = nrm(ks[1], (BATCH, D_MODEL), 1.0)
    ctx = nrm(ks[2], (BATCH, CTX_LEN, D_MODEL), 1.0)
    c_ctx = nrm(ks[3], (D_MODEL,), 1.0)
    w_mod = nrm(ks[4], (DEPTH, D_MODEL, 9 * D_MODEL), 0.5 * D_MODEL ** -0.5)
    b_mod = nrm(ks[5], (DEPTH, 9 * D_MODEL), 0.02)
    ln_g = 1.0 + nrm(ks[6], (DEPTH, 3, D_MODEL), 0.02)
    ln_b = nrm(ks[7], (DEPTH, 3, D_MODEL), 0.02)
    ffn_w_in = nrm(ks[8], (DEPTH, 2, D_MODEL, 2 * FF), D_MODEL ** -0.5)
    ffn_w_out = nrm(ks[9], (DEPTH, 2, FF, D_MODEL), DN_BETA * FF ** -0.5)
    w_in = nrm(ks[10], (DEPTH, D_MODEL, IN_COLS), D_MODEL ** -0.5)
    lru_conv_w = nrm(ks[11], (DEPTH, LRU_CONV, LRU_W), LRU_CONV ** -0.5)
    lru_conv_b = nrm(ks[12], (DEPTH, LRU_W), 0.02)
    lru_w_a = nrm(ks[13], (DEPTH, 2, LRU_BLOCKS, LRU_BW, LRU_BW), LRU_BW ** -0.5)
    lru_b_a = nrm(ks[14], (DEPTH, 2, LRU_W), 0.02)
    lru_w_x = nrm(ks[15], (DEPTH, 2, LRU_BLOCKS, LRU_BW, LRU_BW), LRU_BW ** -0.5)
    lru_b_x = nrm(ks[16], (DEPTH, 2, LRU_W), 0.02)
    u = jax.random.uniform(ks[17], (DEPTH, 2, LRU_W), f32, 0.9, 0.999)
    s = u ** (1.0 / LRU_C)
    lru_lambda = jnp.log(s) - jnp.log1p(-s)
    gqa_sink = nrm(ks[18], (DEPTH, GQA_HQ), 0.5)
    mla_q_norm = 1.0 + nrm(ks[19], (DEPTH, MLA_RQ), 0.02)
    mla_kv_norm = 1.0 + nrm(ks[20], (DEPTH, MLA_RKV), 0.02)
    mla_w_uq = nrm(ks[21], (DEPTH, MLA_RQ, MLA_H * (MLA_DN + MLA_DR)), MLA_RQ ** -0.5)
    mla_w_ukv = nrm(ks[22], (DEPTH, MLA_RKV, MLA_H * (MLA_DN + MLA_DV)), MLA_RKV ** -0.5)
    g0 = 1.0 - 2.0 ** (-5.0 - jnp.arange(RET_H, dtype=f32))
    ret_decay = (jnp.log(g0) - jnp.log1p(-g0))[None, None, :] + nrm(ks[23], (DEPTH, 2, RET_H), 0.1)
    ret_gn_g = 1.0 + nrm(ks[24], (DEPTH, RET_H * RET_DV), 0.02)
    w_branch = nrm(ks[25], (DEPTH, N_BRANCH, BRANCH_W, D_MODEL), BRANCH_W ** -0.5)
    w_out = nrm(ks[26], (DEPTH, D_MODEL, D_MODEL), DN_BETA * D_MODEL ** -0.5)
    return {'x': x, 'c': c, 'ctx': ctx, 'c_ctx': c_ctx, 'w_mod': w_mod, 'b_mod': b_mod, 'ln_g': ln_g, 'ln_b': ln_b, 'ffn_w_in': ffn_w_in, 'ffn_w_out': ffn_w_out, 'w_in': w_in, 'lru_conv_w': lru_conv_w, 'lru_conv_b': lru_conv_b, 'lru_w_a': lru_w_a, 'lru_b_a': lru_b_a, 'lru_w_x': lru_w_x, 'lru_b_x': lru_b_x, 'lru_lambda': lru_lambda, 'gqa_sink': gqa_sink, 'mla_q_norm': mla_q_norm, 'mla_kv_norm': mla_kv_norm, 'mla_w_uq': mla_w_uq, 'mla_w_ukv': mla_w_ukv, 'ret_decay': ret_decay, 'ret_gn_g': ret_gn_g, 'w_branch': w_branch, 'w_out': w_out}


def reference(x, c, ctx, c_ctx, w_mod, b_mod, ln_g, ln_b, ffn_w_in, ffn_w_out, w_in, lru_conv_w, lru_conv_b, lru_w_a, lru_b_a, lru_w_x, lru_b_x, lru_lambda, gqa_sink, mla_q_norm, mla_kv_norm, mla_w_uq, mla_w_ukv, ret_decay, ret_gn_g, w_branch, w_out):
    B, S, D = x.shape
    n_rows = S // GRID_W
    rows = jnp.repeat(jnp.arange(n_rows), GRID_W)
    cols = jnp.tile(jnp.arange(GRID_W), n_rows)
    s_lat = jax.nn.silu(c)
    s_ctx = jax.nn.silu(c_ctx)
    xl, xc = x, ctx
    for l in range(DEPTH):
        last = l == DEPTH - 1
        mod_l = (s_lat @ w_mod[l] + b_mod[l]).reshape(B, 9, 1, D)
        mod_c = (s_ctx @ w_mod[l] + b_mod[l]).reshape(9, 1, 1, D)
        ml = [mod_l[:, k] for k in range(9)]
        mc = [mod_c[k] for k in range(9)]
        xl = _half_ffn(xl, ml[0], ml[1], ml[2], ffn_w_in[l, 0], ffn_w_out[l, 0], ln_g[l, 0], ln_b[l, 0])
        xc = _half_ffn(xc, mc[0], mc[1], mc[2], ffn_w_in[l, 0], ffn_w_out[l, 0], ln_g[l, 0], ln_b[l, 0])
        hl = xl * (1 + ml[4]) + ml[3]
        hc = xc * (1 + mc[4]) + mc[3]
        yl, yc = _mixer(hl, hc, rows, cols, not last, w_in[l], lru_conv_w[l], lru_conv_b[l], lru_w_a[l], lru_b_a[l], lru_w_x[l], lru_b_x[l], lru_lambda[l], gqa_sink[l], mla_q_norm[l], mla_kv_norm[l], mla_w_uq[l], mla_w_ukv[l], ret_decay[l], ret_gn_g[l], w_branch[l], w_out[l])
        xl = _layer_norm(DN_ALPHA * xl + ml[5] * yl, ln_g[l, 1], ln_b[l, 1])
        xl = _half_ffn(xl, ml[6], ml[7], ml[8], ffn_w_in[l, 1], ffn_w_out[l, 1], ln_g[l, 2], ln_b[l, 2])
        if not last:
            xc = _layer_norm(DN_ALPHA * xc + mc[5] * yc, ln_g[l, 1], ln_b[l, 1])
            xc = _half_ffn(xc, mc[6], mc[7], mc[8], ffn_w_in[l, 1], ffn_w_out[l, 1], ln_g[l, 2], ln_b[l, 2])
    return xl
```

```python
import functools

import jax
import jax.numpy as jnp
from jax import lax
from jax.experimental import pallas as pl
from jax.experimental.pallas import tpu as pltpu

D_MODEL = 1024
DEPTH = 2
GRID_W = 64
N_BRANCH = 4
BRANCH_W = 512
LRU_W = 512
LRU_BLOCKS = 8
LRU_BW = LRU_W // LRU_BLOCKS
LRU_C = 8.0
GQA_HQ = 8
GQA_HKV = 2
GQA_DH = 64
WINDOW = 128
ATT_BLOCK = 128
MLA_H = 8
MLA_RQ = 384
MLA_RKV = 256
MLA_DN = 64
MLA_DR = 32
MLA_DV = 64
RET_H = 4
RET_DK = 128
RET_DV = 128
RET_CHUNK = 128
FF = 2816
ROPE_BASE = 10000.0
LN_EPS = 1e-5
MASK_VALUE = -1e30
DN_ALPHA = (2 * DEPTH) ** 0.25
IN_SIZES = (LRU_W, LRU_W, GQA_HQ * GQA_DH, GQA_HKV * GQA_DH, GQA_HKV * GQA_DH, MLA_RQ, MLA_RKV, MLA_DR,
            RET_H * RET_DK, RET_H * RET_DK, RET_H * RET_DV, RET_H * RET_DV, N_BRANCH * D_MODEL)

LANES = 128
SUBLANES = 8
VMEM_LIMIT = 56 * 1024 * 1024
FFN_CHUNK = 1408
BF = jnp.bfloat16
F32 = jnp.float32


def _params(n_axes):
    return pltpu.CompilerParams(dimension_semantics=("arbitrary",) * n_axes, vmem_limit_bytes=VMEM_LIMIT)


def _resident(shape):
    nd = len(shape)
    return pl.BlockSpec(shape, lambda *_: (0,) * nd, pipeline_mode=pl.Buffered(1))


def _const(shape):
    nd = len(shape)
    return pl.BlockSpec(shape, lambda *_: (0,) * nd)


def _layer_norm(z, g, b):
    mu = jnp.mean(z, axis=-1, keepdims=True)
    zc = z - mu
    var = jnp.mean(zc * zc, axis=-1, keepdims=True)
    return zc * lax.rsqrt(var + LN_EPS) * g + b


def _sigmoid(x):
    return 1.0 / (1.0 + jnp.exp(-x))


def _dot(a, b):
    return jnp.dot(a, b, preferred_element_type=F32)


def _dot_nt(a, b):
    return lax.dot_general(a, b, (((1,), (1,)), ((), ())), preferred_element_type=F32)


def _dot_tn(a, b):
    return lax.dot_general(a, b, (((0,), (0,)), ((), ())), preferred_element_type=F32)


def _rope(y, c, s, nf):
    w = y.shape[-1]
    lane = lax.broadcasted_iota(jnp.int32, y.shape, 1)
    partner = jnp.where((lane % (2 * nf)) < nf, pltpu.roll(y, w - nf, axis=1), pltpu.roll(y, nf, axis=1))
    return y * c + partner * s


class _Layout:
    def __init__(self, B, S, Lc):
        self.B, self.S, self.Lc = B, S, Lc
        self.n_ctx, self.n_lat = B * Lc, B * S
        self.n_tok = self.n_ctx + self.n_lat
        tm = 512
        while S % tm or self.n_ctx % tm:
            tm //= 2
        self.tm = tm
        self.ctx_tiles = self.n_ctx // tm
        self.tiles_per_seq = S // tm
        self.tiles = self.ctx_tiles + B * self.tiles_per_seq
        assert self.n_ctx % S == 0, "latent rows must start on a whole-sequence block"
        self.lat_seq_block = self.n_ctx // S
        self.lat_att_block = self.n_ctx // ATT_BLOCK

    def mod_index(self, g):
        return jnp.where(g < self.ctx_tiles, self.B, (g - self.ctx_tiles) // self.tiles_per_seq)

    def rope_index(self, g):
        return jnp.where(g < self.ctx_tiles, self.tiles_per_seq, (g - self.ctx_tiles) % self.tiles_per_seq)


def _mod_kernel(s_ref, w_ref, b_ref, o_ref):
    s = s_ref[...]
    s = s * _sigmoid(s)
    o_ref[0] = _dot(s.astype(BF), w_ref[0].astype(BF)) + b_ref[0]


def _modulation(cond, w_mod, b_mod):
    R = cond.shape[0]
    tn = 1024
    n9 = 9 * D_MODEL
    return pl.pallas_call(
        _mod_kernel,
        grid=(DEPTH, n9 // tn),
        in_specs=[pl.BlockSpec((R, D_MODEL), lambda l, j: (0, 0)),
                  pl.BlockSpec((1, D_MODEL, tn), lambda l, j: (l, 0, j)),
                  pl.BlockSpec((1, 1, tn), lambda l, j: (l, 0, j))],
        out_specs=pl.BlockSpec((1, R, tn), lambda l, j: (l, 0, j)),
        out_shape=jax.ShapeDtypeStruct((DEPTH, R, n9), F32),
        compiler_params=_params(2),
        name="modulation",
    )(cond, w_mod, b_mod.reshape(DEPTH, 1, n9))


def _ffn_kernel(x_ref, mod_ref, w1_ref, w2_ref, g_ref, b_ref, o_ref, *, k0):
    x = x_ref[...]
    shift = mod_ref[0, k0:k0 + 1, :]
    scale = mod_ref[0, k0 + 1:k0 + 2, :]
    gate = mod_ref[0, k0 + 2:k0 + 3, :]
    xm = (x * (1.0 + scale) + shift).astype(BF)
    acc = jnp.zeros(x.shape, F32)
    for c in range(FF // FFN_CHUNK):
        lo = c * FFN_CHUNK
        a = _dot(xm, w1_ref[:, lo:lo + FFN_CHUNK])
        b = _dot(xm, w1_ref[:, FF + lo:FF + lo + FFN_CHUNK])
        h = (a * _sigmoid(a) * b).astype(BF)
        acc = acc + _dot(h, w2_ref[lo:lo + FFN_CHUNK, :])
    z = DN_ALPHA * x + (0.5 * gate) * acc
    o_ref[...] = _layer_norm(z, g_ref[...], b_ref[...])


def _half_ffn(x, mod, w1, w2, g, b, lay, k0, x_off, g_off):
    tm = lay.tm
    n = lay.tiles - g_off
    return pl.pallas_call(
        functools.partial(_ffn_kernel, k0=k0),
        grid=(n,),
        in_specs=[pl.BlockSpec((tm, D_MODEL), lambda i: (i + x_off, 0)),
                  pl.BlockSpec((1, 9, D_MODEL), lambda i: (lay.mod_index(i + g_off), 0, 0)),
                  _resident(w1.shape), _resident(w2.shape),
                  _const((1, D_MODEL)), _const((1, D_MODEL))],
        out_specs=pl.BlockSpec((tm, D_MODEL), lambda i: (i, 0)),
        out_shape=jax.ShapeDtypeStruct((n * tm, D_MODEL), F32),
        compiler_params=_params(1),
        name="half_ffn",
    )(x, mod, w1, w2, g, b)


_PROJ_GROUPS = (
    ("ax", 512, 0, None, 1.0), ("ay", 512, 0, None, 1.0),
    ("bq", 512, 16, 0, GQA_DH ** -0.5), ("bk", 128, 16, 1, 1.0), ("bv", 128, 0, None, 1.0),
    ("cq", 384, 0, None, 1.0), ("ckv", 256, 0, None, 1.0), ("ckr", 128, 8, 2, 1.0),
    ("dq", 512, 32, 3, 1.0), ("dk", 512, 32, 3, RET_DK ** -0.5), ("dv", 512, 0, None, 1.0), ("dg", 512, 0, None, 1.0),
)
_N_ROPE_TABLES = 4


def _proj_kernel(x_ref, mod_ref, w_ref, *refs):
    tabs = refs[:2 * _N_ROPE_TABLES]
    outs = refs[2 * _N_ROPE_TABLES:]
    x = x_ref[...]
    h = (x * (1.0 + mod_ref[0, 4:5, :]) + mod_ref[0, 3:4, :]).astype(BF)
    off = 0
    for (_, width, nf, tab, post), o_ref in zip(_PROJ_GROUPS, outs):
        y = _dot(h, w_ref[:, off:off + width])
        if nf:
            y = _rope(y, tabs[2 * tab][...], tabs[2 * tab + 1][...], nf)
        if post != 1.0:
            y = y * post
        o_ref[...] = y.astype(BF)
        off += width


def _mixer_proj(x, mod, w, tabs, lay):
    tm = lay.tm
    row = lambda i: (i, 0)
    tab_specs = [pl.BlockSpec((tm, t.shape[1]), lambda i: (lay.rope_index(i), 0)) for t in tabs]
    return pl.pallas_call(
        _proj_kernel,
        grid=(lay.tiles,),
        in_specs=[pl.BlockSpec((tm, D_MODEL), row),
                  pl.BlockSpec((1, 9, D_MODEL), lambda i: (lay.mod_index(i), 0, 0)),
                  _resident(w.shape)] + tab_specs,
        out_specs=[pl.BlockSpec((tm, g[1]), row) for g in _PROJ_GROUPS],
        out_shape=[jax.ShapeDtypeStruct((lay.n_tok, g[1]), BF) for g in _PROJ_GROUPS],
        compiler_params=_params(1),
        name="mixer_proj",
    )(x, mod, w, *tabs)


def _gate_kernel(x_ref, mod_ref, w_ref, o_ref):
    x = x_ref[...]
    h = (x * (1.0 + mod_ref[0, 4:5, :]) + mod_ref[0, 3:4, :]).astype(BF)
    for k in range(N_BRANCH):
        sl = slice(k * D_MODEL, (k + 1) * D_MODEL)
        o_ref[:, sl] = _dot(h, w_ref[:, sl]).astype(BF)


def _gate_proj(x, mod, w, lay, g_off):
    tm = lay.tm
    n = lay.tiles - g_off
    return pl.pallas_call(
        _gate_kernel,
        grid=(n,),
        in_specs=[pl.BlockSpec((tm, D_MODEL), lambda i: (i + g_off, 0)),
                  pl.BlockSpec((1, 9, D_MODEL), lambda i: (lay.mod_index(i + g_off), 0, 0)),
                  _resident(w.shape)],
        out_specs=pl.BlockSpec((tm, N_BRANCH * D_MODEL), lambda i: (i, 0)),
        out_shape=jax.ShapeDtypeStruct((n * tm, N_BRANCH * D_MODEL), BF),
        compiler_params=_params(1),
        name="gate_proj",
    )(x, mod, w)


def _conv4(x, w_ref, b_ref):
    T = x.shape[0]
    row = lax.broadcasted_iota(jnp.int32, x.shape, 0)
    y = b_ref[...] + w_ref[2:3, :] * x
    y = y + w_ref[0:1, :] * jnp.where(row >= 2, pltpu.roll(x, 2, axis=0), 0.0)
    y = y + w_ref[1:2, :] * jnp.where(row >= 1, pltpu.roll(x, 1, axis=0), 0.0)
    y = y + w_ref[3:4, :] * jnp.where(row < T - 1, pltpu.roll(x, T - 1, axis=0), 0.0)
    return y


def _scan8(a, b, reverse):
    row = lax.broadcasted_iota(jnp.int32, a.shape, 0)
    for s in (1, 2, 4):
        if reverse:
            keep = row < SUBLANES - s
            a_sh = pltpu.roll(a, SUBLANES - s, axis=0)
            b_sh = pltpu.roll(b, SUBLANES - s, axis=0)
        else:
            keep = row >= s
            a_sh = pltpu.roll(a, s, axis=0)
            b_sh = pltpu.roll(b, s, axis=0)
        b = jnp.where(keep, b + a * b_sh, b)
        a = jnp.where(keep, a * a_sh, a)
    return a, b


def _lru_kernel(axc_ref, axl_ref, ayc_ref, ayl_ref, cw_ref, cb_ref, wg_ref, bg_ref, lam_ref,
                oc_ref, ol_ref, xa_s, a_s, b_s, h_s, *, Lc, S, rows):
    W = LRU_W
    xa_s[0:Lc, :] = _conv4(axc_ref[...].astype(F32), cw_ref, cb_ref)
    xa_s[Lc:Lc + S, :] = _conv4(axl_ref[...].astype(F32), cw_ref, cb_ref)
    n8 = (Lc + S) // SUBLANES
    c8 = Lc // SUBLANES

    for d in range(2):
        lam = lam_ref[d:d + 1, :]
        neg = -lam
        softplus = jnp.maximum(neg, 0.0) + jnp.log(1.0 + jnp.exp(-jnp.abs(neg)))
        decay = -LRU_C * softplus

        def gates(i, carry):
            r0 = pl.multiple_of(i * rows, rows)
            xa = xa_s[pl.ds(r0, rows), :]
            xb = xa.astype(BF)
            lo = 2 * d * W
            r = _sigmoid(_dot(xb, wg_ref[:, lo:lo + W]) + bg_ref[:, lo:lo + W])
            g = _sigmoid(_dot(xb, wg_ref[:, lo + W:lo + 2 * W]) + bg_ref[:, lo + W:lo + 2 * W])
            a = jnp.exp(decay * r)
            a_s[pl.ds(r0, rows), :] = a
            b_s[pl.ds(r0, rows), :] = jnp.sqrt(1.0 - a * a) * (g * xa)
            return carry

        lax.fori_loop(0, (Lc + S) // rows, gates, 0)

        def step(i, h, reverse=bool(d)):
            r0 = pl.multiple_of(i * SUBLANES, SUBLANES)
            a8, b8 = _scan8(a_s[pl.ds(r0, SUBLANES), :], b_s[pl.ds(r0, SUBLANES), :], reverse)
            h8 = b8 + a8 * h
            if reverse:
                h_s[pl.ds(r0, SUBLANES), :] = h_s[pl.ds(r0, SUBLANES), :] + h8
                return jnp.broadcast_to(h8[0:1, :], h8.shape)
            h_s[pl.ds(r0, SUBLANES), :] = h8
            return jnp.broadcast_to(h8[SUBLANES - 1:SUBLANES, :], h8.shape)

        h0 = jnp.zeros((SUBLANES, W), F32)
        if d == 0:
            lax.fori_loop(0, n8, step, h0)
        else:
            h1 = lax.fori_loop(0, c8, lambda i, h: step(c8 - 1 - i, h), h0)
            lax.fori_loop(0, n8 - c8, lambda i, h: step(n8 - 1 - i, h), h1)

    def gelu(v):
        return 0.5 * v * (1.0 + jnp.tanh(0.7978845608028654 * (v + 0.044715 * v * v * v)))

    oc_ref[...] = (h_s[0:Lc, :] * gelu(ayc_ref[...].astype(F32))).astype(BF)
    ol_ref[...] = (h_s[Lc:Lc + S, :] * gelu(ayl_ref[...].astype(F32))).astype(BF)


def _lru(ax, ay, conv_w, conv_b, wg, bg, lam, lay):
    B, S, Lc = lay.B, lay.S, lay.Lc
    W = LRU_W
    rows = 256
    while Lc % rows or S % rows:
        rows //= 2
    ctx = pl.BlockSpec((Lc, W), lambda b: (b, 0))
    lat = pl.BlockSpec((S, W), lambda b: (b + lay.lat_seq_block, 0))
    return pl.pallas_call(
        functools.partial(_lru_kernel, Lc=Lc, S=S, rows=rows),
        grid=(B,),
        in_specs=[ctx, lat, ctx, lat, _const((4, W)), _const((1, W)), _const(wg.shape), _const((1, 4 * W)), _const((2, W))],
        out_specs=[pl.BlockSpec((Lc, W), lambda b: (b, 0)), pl.BlockSpec((S, W), lambda b: (b, 0))],
        out_shape=[jax.ShapeDtypeStruct((lay.n_ctx, W), BF), jax.ShapeDtypeStruct((lay.n_lat, W), BF)],
        scratch_shapes=[pltpu.VMEM((Lc + S, W), F32)] * 4,
        compiler_params=_params(1),
        name="rglru",
    )(ax, ax, ay, ay, conv_w, conv_b, wg, bg, lam)


def _split_heads_pair(x):
    xf = x.astype(F32)
    sw = pltpu.roll(xf, GQA_DH, axis=1)
    lane = lax.broadcasted_iota(jnp.int32, xf.shape, 1)
    low = lane < GQA_DH
    z = jnp.zeros_like(xf)
    head0 = (jnp.where(low, xf, z).astype(BF), jnp.where(low, z, sw).astype(BF))
    head1 = (jnp.where(low, sw, z).astype(BF), jnp.where(low, z, xf).astype(BF))
    return head0, head1


def _gqa_kernel(sink_ref, q_ref, kp_ref, kc_ref, kn_ref, vp_ref, vc_ref, vn_ref, kx_ref, vx_ref, o_ref,
                *, ctx_blocks, S, Lc):
    blk = ATT_BLOCK
    n = pl.program_id(1) - ctx_blocks
    k_all = jnp.concatenate([kp_ref[...], kc_ref[...], kn_ref[...], kx_ref[...]], axis=0)
    v_all = jnp.concatenate([vp_ref[...], vc_ref[...], vn_ref[...], vx_ref[...]], axis=0)
    k_heads = _split_heads_pair(k_all)
    v_heads = _split_heads_pair(v_all)
    nk = 3 * blk + Lc
    row = lax.broadcasted_iota(jnp.int32, (2 * blk, nk), 0)
    col = lax.broadcasted_iota(jnp.int32, (2 * blk, nk), 1)
    qpos = n * blk + (row % blk)
    kpos = jnp.where(n >= 0, (n - 1) * blk, -(1 << 20)) + col
    dist = qpos - kpos
    in_win = (col < 3 * blk) & (dist <= WINDOW) & (dist >= -WINDOW) & (kpos >= 0) & (kpos < S)
    valid = in_win | (col >= 3 * blk)
    top = row[:, 0:1] < blk
    q = q_ref[...]
    for j in range(GQA_HKV):
        q2 = jnp.concatenate([q[:, 2 * j * LANES:(2 * j + 1) * LANES], q[:, (2 * j + 1) * LANES:(2 * j + 2) * LANES]], axis=0)
        acc = jnp.zeros((2 * blk, LANES), F32)
        for half in range(2):
            s = _dot_nt(q2, k_heads[j][half])
            s = jnp.where(valid, s, MASK_VALUE)
            sink = jnp.where(top, sink_ref[4 * j + half], sink_ref[4 * j + 2 + half])
            m = jnp.maximum(jnp.max(s, axis=-1, keepdims=True), sink)
            p = jnp.exp(s - m)
            den = jnp.sum(p, axis=-1, keepdims=True) + jnp.exp(sink - m)
            acc = acc + _dot(p.astype(BF), v_heads[j][half]) * (1.0 / den)
        o_ref[:, 2 * j * LANES:(2 * j + 1) * LANES] = acc[0:blk].astype(BF)
        o_ref[:, (2 * j + 1) * LANES:(2 * j + 2) * LANES] = acc[blk:2 * blk].astype(BF)


def _gqa(q, k, v, sink, lay, ctx_queries):
    B, S, Lc = lay.B, lay.S, lay.Lc
    blk = ATT_BLOCK
    nb = S // blk
    cb = Lc // blk if ctx_queries else 0
    lat0 = lay.lat_att_block

    def q_idx(b, j):
        return (jnp.where(j < cb, b * cb + j, lat0 + b * nb + (j - cb)), 0)

    def win_idx(delta):
        def idx(b, j):
            n = jnp.clip(j - cb + delta, 0, nb - 1)
            return (lat0 + b * nb + n, 0)
        return idx

    def o_idx(b, j):
        if ctx_queries:
            return q_idx(b, j)
        return (b * nb + j, 0)

    kv = lambda d: pl.BlockSpec((blk, LANES), win_idx(d))
    ctx_kv = pl.BlockSpec((Lc, LANES), lambda b, j: (b, 0))
    rows = lay.n_tok if ctx_queries else lay.n_lat
    return pl.pallas_call(
        functools.partial(_gqa_kernel, ctx_blocks=cb, S=S, Lc=Lc),
        grid=(B, cb + nb),
        in_specs=[pl.BlockSpec(memory_space=pltpu.SMEM), pl.BlockSpec((blk, 4 * LANES), q_idx),
                  kv(-1), kv(0), kv(1), kv(-1), kv(0), kv(1), ctx_kv, ctx_kv],
        out_specs=pl.BlockSpec((blk, 4 * LANES), o_idx),
        out_shape=jax.ShapeDtypeStruct((rows, 4 * LANES), BF),
        compiler_params=_params(2),
        name="gqa_window",
    )(sink, q, k, k, k, v, v, v, k, v)


def _rms(x, g):
    return x * lax.rsqrt(jnp.mean(x * x, axis=-1, keepdims=True) + LN_EPS) * g


def _mla_prep_kernel(cq_ref, ckv_ref, ckr_ref, gq_ref, gkv_ref, wq_ref, wk_ref, wv_ref, c_ref, s_ref,
                     q_ref, k_ref, v_ref):
    yq = _rms(cq_ref[...].astype(F32), gq_ref[...]).astype(BF)
    q = _rope(_dot(yq, wq_ref[...]), c_ref[...], s_ref[...], MLA_DR // 4)
    q_ref[...] = (q * ((MLA_DN + MLA_DR) ** -0.5)).astype(BF)
    ykv = _rms(ckv_ref[...].astype(F32), gkv_ref[...]).astype(BF)
    kr = ckr_ref[...].astype(F32)
    kr_all = pltpu.roll(jnp.concatenate([kr] * MLA_H, axis=1), MLA_DN, axis=1)
    k_ref[...] = (_dot(ykv, wk_ref[...]) + kr_all).astype(BF)
    v_ref[...] = _dot(ykv, wv_ref[...]).astype(BF)


def _mla_prep(cq, ckv, ckr, gq, gkv, wq, wk, wv, ctab, stab, lay):
    tm = lay.tm
    row = lambda i: (i, 0)
    tab = pl.BlockSpec((tm, MLA_H * LANES), lambda i: (lay.rope_index(i), 0))
    return pl.pallas_call(
        _mla_prep_kernel,
        grid=(lay.tiles,),
        in_specs=[pl.BlockSpec((tm, MLA_RQ), row), pl.BlockSpec((tm, MLA_RKV), row), pl.BlockSpec((tm, LANES), row),
                  _const((1, MLA_RQ)), _const((1, MLA_RKV)), _const(wq.shape), _const(wk.shape), _const(wv.shape), tab, tab],
        out_specs=[pl.BlockSpec((tm, MLA_H * LANES), row), pl.BlockSpec((tm, MLA_H * LANES), row),
                   pl.BlockSpec((tm, MLA_H * MLA_DV), row)],
        out_shape=[jax.ShapeDtypeStruct((lay.n_tok, MLA_H * LANES), BF), jax.ShapeDtypeStruct((lay.n_tok, MLA_H * LANES), BF),
                   jax.ShapeDtypeStruct((lay.n_tok, MLA_H * MLA_DV), BF)],
        compiler_params=_params(1),
        name="mla_prep",
    )(cq, ckv, ckr, gq, gkv, wq, wk, wv, ctab, stab)


def _mla_heads(q_ref, kv_refs, o_ref):
    lane = lax.broadcasted_iota(jnp.int32, (1, LANES), 1)
    low = lane < MLA_DV
    for pair in range(MLA_H // 2):
        vsl = slice(pair * LANES, (pair + 1) * LANES)
        acc = None
        for half in range(2):
            h = 2 * pair + half
            hsl = slice(h * LANES, (h + 1) * LANES)
            qh = q_ref[:, hsl]
            scores = [_dot_nt(qh, k_ref[:, hsl]) for k_ref, _ in kv_refs]
            m = functools.reduce(jnp.maximum, [jnp.max(s, axis=-1, keepdims=True) for s in scores])
            probs = [jnp.exp(s - m) for s in scores]
            den = functools.reduce(jnp.add, [jnp.sum(p, axis=-1, keepdims=True) for p in probs])
            keep = low if half == 0 else jnp.logical_not(low)
            o = None
            for p, (_, v_ref) in zip(probs, kv_refs):
                vh = jnp.where(keep, v_ref[:, vsl], jnp.zeros((), BF))
                t = _dot(p.astype(BF), vh)
                o = t if o is None else o + t
            o = o * (1.0 / den)
            acc = o if acc is None else acc + o
        o_ref[:, vsl] = acc.astype(BF)


def _mla_lat_kernel(q_ref, kc_ref, kl_ref, vc_ref, vl_ref, o_ref):
    _mla_heads(q_ref, ((kc_ref, vc_ref), (kl_ref, vl_ref)), o_ref)


def _mla_ctx_kernel(q_ref, kc_ref, vc_ref, o_ref):
    _mla_heads(q_ref, ((kc_ref, vc_ref),), o_ref)


def _mla_latent(q, k, v, lay):
    B, S, Lc = lay.B, lay.S, lay.Lc
    tq = min(512, S)
    nq = S // tq
    lat_q0 = lay.n_ctx // tq
    kw, vw = MLA_H * LANES, MLA_H * MLA_DV
    return pl.pallas_call(
        _mla_lat_kernel,
        grid=(B, nq),
        in_specs=[pl.BlockSpec((tq, kw), lambda b, i: (lat_q0 + b * nq + i, 0)),
                  pl.BlockSpec((Lc, kw), lambda b, i: (b, 0)),
                  pl.BlockSpec((S, kw), lambda b, i: (lay.lat_seq_block + b, 0)),
                  pl.BlockSpec((Lc, vw), lambda b, i: (b, 0)),
                  pl.BlockSpec((S, vw), lambda b, i: (lay.lat_seq_block + b, 0))],
        out_specs=pl.BlockSpec((tq, vw), lambda b, i: (b * nq + i, 0)),
        out_shape=jax.ShapeDtypeStruct((lay.n_lat, vw), BF),
        compiler_params=_params(2),
        name="mla_latent",
    )(q, k, k, v, v)


def _mla_context(q, k, v, lay):
    B, Lc = lay.B, lay.Lc
    kw, vw = MLA_H * LANES, MLA_H * MLA_DV
    blk = lambda w: pl.BlockSpec((Lc, w), lambda b: (b, 0))
    return pl.pallas_call(
        _mla_ctx_kernel,
        grid=(B,),
        in_specs=[blk(kw), blk(kw), blk(vw)],
        out_specs=blk(vw),
        out_shape=jax.ShapeDtypeStruct((lay.n_ctx, vw), BF),
        compiler_params=_params(1),
        name="mla_context",
    )(q, k, v)


def _ret_kernel(dec_ref, qc_ref, ql_ref, kc_ref, kl_ref, vc_ref, vl_ref, gc_ref, gl_ref, gn_ref,
                oc_ref, ol_ref, acc_c, acc_l, *, Lc, S):
    C = RET_CHUNK
    pos_i = lax.broadcasted_iota(jnp.int32, (C, C), 0).astype(F32)
    pos_j = lax.broadcasted_iota(jnp.int32, (C, C), 1).astype(F32)
    pos = pos_i[:, 0:1]

    def run(direction):
        gam = _sigmoid(dec_ref[0, direction:direction + 1, :])
        lg = jnp.log(gam)
        if direction == 0:
            diff = pos_i - pos_j
            keep = diff >= 0.0
            zeta = jnp.exp((C - 1.0 - pos) * lg)
            xi = jnp.exp((pos + 1.0) * lg)
        else:
            diff = pos_j - pos_i
            keep = diff > 0.0
            zeta = jnp.exp(pos * lg)
            xi = jnp.exp((C - pos) * lg)
        dmask = jnp.where(keep, jnp.exp(jnp.where(keep, diff, 0.0) * lg), 0.0)
        decay = jnp.exp(C * lg)

        def chunk(q_ref, k_ref, v_ref, acc_ref, c, R):
            r0 = pl.multiple_of(c * C, C)
            q = q_ref[pl.ds(r0, C), :]
            k = k_ref[pl.ds(r0, C), :]
            v = v_ref[pl.ds(r0, C), :]
            s = _dot_nt(q, k) * dmask
            o = _dot(s.astype(BF), v) + _dot((q.astype(F32) * xi).astype(BF), R.astype(BF))
            if direction == 0:
                acc_ref[pl.ds(r0, C), :] = o
            else:
                acc_ref[pl.ds(r0, C), :] = acc_ref[pl.ds(r0, C), :] + o
            kz = (k.astype(F32) * zeta).astype(BF)
            return decay * R + _dot_tn(kz, v)

        nc, nl = Lc // C, S // C
        R = jnp.zeros((RET_DK, RET_DV), F32)
        if direction == 0:
            R = lax.fori_loop(0, nc, lambda c, R: chunk(qc_ref, kc_ref, vc_ref, acc_c, c, R), R)
            lax.fori_loop(0, nl, lambda c, R: chunk(ql_ref, kl_ref, vl_ref, acc_l, c, R), R)
        else:
            R = lax.fori_loop(0, nc, lambda c, R: chunk(qc_ref, kc_ref, vc_ref, acc_c, nc - 1 - c, R), R)
            lax.fori_loop(0, nl, lambda c, R: chunk(ql_ref, kl_ref, vl_ref, acc_l, nl - 1 - c, R), R)

    run(0)
    run(1)

    def finish(acc_ref, g_ref, o_ref):
        y = acc_ref[...]
        mu = jnp.mean(y, axis=-1, keepdims=True)
        yc = y - mu
        var = jnp.mean(yc * yc, axis=-1, keepdims=True)
        yn = yc * lax.rsqrt(var + LN_EPS) * gn_ref[...]
        g = g_ref[...].astype(F32)
        o_ref[...] = (g * _sigmoid(g) * yn).astype(BF)

    finish(acc_c, gc_ref, oc_ref)
    finish(acc_l, gl_ref, ol_ref)


def _retention(dq, dk, dv, dg, decay_lanes, gn, lay):
    B, S, Lc = lay.B, lay.S, lay.Lc
    ctx = pl.BlockSpec((Lc, LANES), lambda b, h: (b, h))
    lat = pl.BlockSpec((S, LANES), lambda b, h: (lay.lat_seq_block + b, h))
    return pl.pallas_call(
        functools.partial(_ret_kernel, Lc=Lc, S=S),
        grid=(B, RET_H),
        in_specs=[pl.BlockSpec((1, 2, LANES), lambda b, h: (h, 0, 0)),
                  ctx, lat, ctx, lat, ctx, lat, ctx, lat, pl.BlockSpec((1, LANES), lambda b, h: (0, h))],
        out_specs=[pl.BlockSpec((Lc, LANES), lambda b, h: (b, h)), pl.BlockSpec((S, LANES), lambda b, h: (b, h))],
        out_shape=[jax.ShapeDtypeStruct((lay.n_ctx, RET_H * RET_DV), BF), jax.ShapeDtypeStruct((lay.n_lat, RET_H * RET_DV), BF)],
        scratch_shapes=[pltpu.VMEM((Lc, RET_DV), F32), pltpu.VMEM((S, RET_DV), F32)],
        compiler_params=_params(2),
        name="retention",
    )(decay_lanes, dq, dq, dk, dk, dv, dv, dg, dg, gn)


def _merge_kernel(x_ref, mod_ref, a_ref, b_ref, c_ref, d_ref, gt_ref, wb_ref, wo_ref, g_ref, beta_ref, o_ref):
    acc = None
    for k, br in enumerate((a_ref, b_ref, c_ref, d_ref)):
        gate = _sigmoid(gt_ref[:, k * D_MODEL:(k + 1) * D_MODEL].astype(F32))
        t = gate * _dot(br[...], wb_ref[k])
        acc = t if acc is None else acc + t
    y = _dot(acc.astype(BF), wo_ref[...])
    z = DN_ALPHA * x_ref[...] + mod_ref[0, 5:6, :] * y
    o_ref[...] = _layer_norm(z, g_ref[...], beta_ref[...])


def _merge(x, mod, branches, gt, wb, wo, g, beta, lay, g_off, br_off):
    tm = lay.tm
    n = lay.tiles - g_off
    br = pl.BlockSpec((tm, BRANCH_W), lambda i: (i + br_off, 0))
    return pl.pallas_call(
        _merge_kernel,
        grid=(n,),
        in_specs=[pl.BlockSpec((tm, D_MODEL), lambda i: (i + g_off, 0)),
                  pl.BlockSpec((1, 9, D_MODEL), lambda i: (lay.mod_index(i + g_off), 0, 0)),
                  br, br, br, br,
                  pl.BlockSpec((tm, N_BRANCH * D_MODEL), lambda i: (i, 0)),
                  _resident(wb.shape), _resident(wo.shape), _const((1, D_MODEL)), _const((1, D_MODEL))],
        out_specs=pl.BlockSpec((tm, D_MODEL), lambda i: (i, 0)),
        out_shape=jax.ShapeDtypeStruct((n * tm, D_MODEL), F32),
        compiler_params=_params(1),
        name="gated_merge",
    )(x, mod, *branches, gt, wb, wo, g, beta)


def _rope_tables(S, head_dim, lead, reps, tail, tm):
    nf = head_dim // 4
    inv = ROPE_BASE ** (-jnp.arange(nf, dtype=F32) / nf)
    t = jnp.arange(S)
    ar = (t // GRID_W).astype(F32)[:, None] * inv[None, :]
    ac = (t % GRID_W).astype(F32)[:, None] * inv[None, :]
    cos = jnp.concatenate([jnp.cos(ar), jnp.cos(ar), jnp.cos(ac), jnp.cos(ac)], -1)
    sin = jnp.concatenate([-jnp.sin(ar), jnp.sin(ar), -jnp.sin(ac), jnp.sin(ac)], -1)
    one = lambda w: jnp.ones((S, w), F32)
    zero = lambda w: jnp.zeros((S, w), F32)
    cos = jnp.tile(jnp.concatenate([one(lead), cos, one(tail)], -1), (1, reps))
    sin = jnp.tile(jnp.concatenate([zero(lead), sin, zero(tail)], -1), (1, reps))
    W = cos.shape[1]
    cos = jnp.concatenate([cos, jnp.ones((tm, W), F32)], 0)
    sin = jnp.concatenate([sin, jnp.zeros((tm, W), F32)], 0)
    return cos, sin


def _block_diag(w):
    n, bw, _ = w.shape
    eye = jnp.eye(n, dtype=w.dtype)
    return (eye[:, None, :, None] * w[:, :, None, :]).reshape(n * bw, n * bw)


def _layer_weights(l, w_in, lru_w_a, lru_b_a, lru_w_x, lru_b_x, mla_w_uq, mla_w_ukv, ret_decay):
    cuts = [0]
    for s in IN_SIZES:
        cuts.append(cuts[-1] + s)
    cols = [w_in[l][:, cuts[i]:cuts[i + 1]] for i in range(len(IN_SIZES))]
    cols[7] = jnp.pad(cols[7], ((0, 0), (0, LANES - MLA_DR)))
    w_proj = jnp.concatenate(cols[:12], axis=1).astype(BF)
    w_gate = cols[12].astype(BF)
    wg = jnp.concatenate([_block_diag(lru_w_a[l, 0]), _block_diag(lru_w_x[l, 0]),
                          _block_diag(lru_w_a[l, 1]), _block_diag(lru_w_x[l, 1])], axis=1).astype(BF)
    bg = jnp.concatenate([lru_b_a[l, 0], lru_b_x[l, 0], lru_b_a[l, 1], lru_b_x[l, 1]])[None, :]
    uq = mla_w_uq[l].reshape(MLA_RQ, MLA_H, MLA_DN + MLA_DR)
    wq = jnp.pad(uq, ((0, 0), (0, 0), (0, LANES - MLA_DN - MLA_DR))).reshape(MLA_RQ, MLA_H * LANES).astype(BF)
    ukv = mla_w_ukv[l].reshape(MLA_RKV, MLA_H, MLA_DN + MLA_DV)
    wk = jnp.pad(ukv[:, :, :MLA_DN], ((0, 0), (0, 0), (0, LANES - MLA_DN))).reshape(MLA_RKV, MLA_H * LANES).astype(BF)
    wv = ukv[:, :, MLA_DN:].reshape(MLA_RKV, MLA_H * MLA_DV).astype(BF)
    dec = jnp.broadcast_to(ret_decay[l].T[:, :, None], (RET_H, 2, LANES))
    return w_proj, w_gate, wg, bg, wq, wk, wv, dec


def kernel(x, c, ctx, c_ctx, w_mod, b_mod, ln_g, ln_b, ffn_w_in, ffn_w_out, w_in, lru_conv_w, lru_conv_b, lru_w_a, lru_b_a, lru_w_x, lru_b_x, lru_lambda, gqa_sink, mla_q_norm, mla_kv_norm, mla_w_uq, mla_w_ukv, ret_decay, ret_gn_g, w_branch, w_out):
    B, S, D = x.shape
    Lc = ctx.shape[1]
    lay = _Layout(B, S, Lc)
    tm = lay.tm

    n_cond = -(-(B + 1) // SUBLANES) * SUBLANES
    cond = jnp.concatenate([c, c_ctx[None, :], jnp.zeros((n_cond - B - 1, D), F32)], axis=0)
    mod_all = _modulation(cond, w_mod, b_mod)[:, :B + 1].reshape(DEPTH, B + 1, 9, D)

    tabs = []
    for head_dim, reps, tail in ((GQA_DH, GQA_HQ, 0), (GQA_DH, GQA_HKV, 0), (MLA_DR, 1, LANES - MLA_DR), (RET_DK, RET_H, 0)):
        tabs.extend(_rope_tables(S, head_dim, 0, reps, tail, tm))
    mla_c, mla_s = _rope_tables(S, MLA_DR, MLA_DN, MLA_H, LANES - MLA_DN - MLA_DR, tm)

    tok = jnp.concatenate([ctx.reshape(B * Lc, D), x.reshape(B * S, D)], axis=0)
    row = lambda v: v[None, :]
    ct = lay.ctx_tiles

    for l in range(DEPTH):
        last = l == DEPTH - 1
        mod = mod_all[l]
        w_proj, w_gate, wg, bg, wq, wk, wv, dec = _layer_weights(
            l, w_in, lru_w_a, lru_b_a, lru_w_x, lru_b_x, mla_w_uq, mla_w_ukv, ret_decay)
        ffn1 = (ffn_w_in[l, 0].astype(BF), ffn_w_out[l, 0].astype(BF))
        ffn2 = (ffn_w_in[l, 1].astype(BF), ffn_w_out[l, 1].astype(BF))

        tok = _half_ffn(tok, mod, *ffn1, row(ln_g[l, 0]), row(ln_b[l, 0]), lay, 0, 0, 0)

        ax, ay, bq, bk, bv, cq, ckv, ckr, dq, dk, dv, dg = _mixer_proj(tok, mod, w_proj, tabs, lay)
        gt = _gate_proj(tok, mod, w_gate, lay, ct if last else 0)

        a_c, a_l = _lru(ax, ay, lru_conv_w[l], row(lru_conv_b[l]), wg, bg, lru_lambda[l], lay)
        b_all = _gqa(bq, bk, bv, gqa_sink[l], lay, ctx_queries=not last)
        mq, mk, mv = _mla_prep(cq, ckv, ckr, row(mla_q_norm[l]), row(mla_kv_norm[l]), wq, wk, wv, mla_c, mla_s, lay)
        c_l = _mla_latent(mq, mk, mv, lay)
        d_c, d_l = _retention(dq, dk, dv, dg, dec, row(ret_gn_g[l]), lay)

        wb = w_branch[l].astype(BF)
        wo = w_out[l].astype(BF)
        if last:
            branches = (a_l, b_all, c_l, d_l)
            tok = _merge(tok, mod, branches, gt, wb, wo, row(ln_g[l, 1]), row(ln_b[l, 1]), lay, ct, 0)
            tok = _half_ffn(tok, mod, *ffn2, row(ln_g[l, 2]), row(ln_b[l, 2]), lay, 6, 0, ct)
        else:
            c_c = _mla_context(mq, mk, mv, lay)
            branches = (jnp.concatenate([a_c, a_l], 0), b_all, jnp.concatenate([c_c, c_l], 0), jnp.concatenate([d_c, d_l], 0))
            tok = _merge(tok, mod, branches, gt, wb, wo, row(ln_g[l, 1]), row(ln_b[l, 1]), lay, 0, 0)
            tok = _half_ffn(tok, mod, *ffn2, row(ln_g[l, 2]), row(ln_b[l, 2]), lay, 6, 0, 0)

    return tok.reshape(B, S, D)
```

```python
import functools

import jax
import jax.numpy as jnp
from jax import lax
from jax.experimental import pallas as pl
from jax.experimental.pallas import tpu as pltpu

D_MODEL = 1024
DEPTH = 2
GRID_W = 64
N_BRANCH = 4
BRANCH_W = 512
LRU_W = 512
LRU_BLOCKS = 8
LRU_BW = LRU_W // LRU_BLOCKS
LRU_C = 8.0
GQA_HQ = 8
GQA_HKV = 2
GQA_DH = 64
WINDOW = 128
ATT_BLOCK = 128
MLA_H = 8
MLA_RQ = 384
MLA_RKV = 256
MLA_DN = 64
MLA_DR = 32
MLA_DV = 64
RET_H = 4
RET_DK = 128
RET_DV = 128
RET_CHUNK = 128
FF = 2816
ROPE_BASE = 10000.0
LN_EPS = 1e-5
MASK_VALUE = -1e30
DN_ALPHA = (2 * DEPTH) ** 0.25
IN_SIZES = (LRU_W, LRU_W, GQA_HQ * GQA_DH, GQA_HKV * GQA_DH, GQA_HKV * GQA_DH, MLA_RQ, MLA_RKV, MLA_DR,
            RET_H * RET_DK, RET_H * RET_DK, RET_H * RET_DV, RET_H * RET_DV, N_BRANCH * D_MODEL)

LANES = 128
SUBLANES = 8
VMEM_LIMIT = 56 * 1024 * 1024
FFN_CHUNK = 1408
BF = jnp.bfloat16
F32 = jnp.float32


def _params(n_axes):
    return pltpu.CompilerParams(dimension_semantics=("arbitrary",) * n_axes, vmem_limit_bytes=VMEM_LIMIT)


def _resident(shape):
    nd = len(shape)
    return pl.BlockSpec(shape, lambda *_: (0,) * nd, pipeline_mode=pl.Buffered(1))


def _const(shape):
    nd = len(shape)
    return pl.BlockSpec(shape, lambda *_: (0,) * nd)


def _layer_norm(z, g, b):
    mu = jnp.mean(z, axis=-1, keepdims=True)
    zc = z - mu
    var = jnp.mean(zc * zc, axis=-1, keepdims=True)
    return zc * lax.rsqrt(var + LN_EPS) * g + b


def _sigmoid(x):
    return 1.0 / (1.0 + jnp.exp(-x))


def _dot(a, b):
    return jnp.dot(a, b, preferred_element_type=F32)


def _dot_nt(a, b):
    return lax.dot_general(a, b, (((1,), (1,)), ((), ())), preferred_element_type=F32)


def _dot_tn(a, b):
    return lax.dot_general(a, b, (((0,), (0,)), ((), ())), preferred_element_type=F32)


def _rope(y, c, s, nf):
    w = y.shape[-1]
    lane = lax.broadcasted_iota(jnp.int32, y.shape, 1)
    partner = jnp.where((lane % (2 * nf)) < nf, pltpu.roll(y, w - nf, axis=1), pltpu.roll(y, nf, axis=1))
    return y * c + partner * s


class _Layout:
    def __init__(self, B, S, Lc):
        self.B, self.S, self.Lc = B, S, Lc
        self.n_ctx, self.n_lat = B * Lc, B * S
        self.n_tok = self.n_ctx + self.n_lat
        tm = 512
        while S % tm or self.n_ctx % tm:
            tm //= 2
        self.tm = tm
        self.ctx_tiles = self.n_ctx // tm
        self.tiles_per_seq = S // tm
        self.tiles = self.ctx_tiles + B * self.tiles_per_seq
        assert self.n_ctx % S == 0, "latent rows must start on a whole-sequence block"
        self.lat_seq_block = self.n_ctx // S
        self.lat_att_block = self.n_ctx // ATT_BLOCK

    def mod_index(self, g):
        return jnp.where(g < self.ctx_tiles, self.B, (g - self.ctx_tiles) // self.tiles_per_seq)

    def rope_index(self, g):
        return jnp.where(g < self.ctx_tiles, self.tiles_per_seq, (g - self.ctx_tiles) % self.tiles_per_seq)


def _mod_kernel(s_ref, w_ref, b_ref, o_ref):
    s = s_ref[...]
    s = s * _sigmoid(s)
    o_ref[0] = _dot(s.astype(BF), w_ref[0].astype(BF)) + b_ref[0]


def _modulation(cond, w_mod, b_mod):
    R = cond.shape[0]
    tn = 1024
    n9 = 9 * D_MODEL
    return pl.pallas_call(
        _mod_kernel,
        grid=(DEPTH, n9 // tn),
        in_specs=[pl.BlockSpec((R, D_MODEL), lambda l, j: (0, 0)),
                  pl.BlockSpec((1, D_MODEL, tn), lambda l, j: (l, 0, j)),
                  pl.BlockSpec((1, 1, tn), lambda l, j: (l, 0, j))],
        out_specs=pl.BlockSpec((1, R, tn), lambda l, j: (l, 0, j)),
        out_shape=jax.ShapeDtypeStruct((DEPTH, R, n9), F32),
        compiler_params=_params(2),
        name="modulation",
    )(cond, w_mod, b_mod.reshape(DEPTH, 1, n9))


def _ffn_kernel(x_ref, mod_ref, w1_ref, w2_ref, g_ref, b_ref, o_ref, *, k0):
    x = x_ref[...]
    shift = mod_ref[0, k0:k0 + 1, :]
    scale = mod_ref[0, k0 + 1:k0 + 2, :]
    gate = mod_ref[0, k0 + 2:k0 + 3, :]
    xm = (x * (1.0 + scale) + shift).astype(BF)
    acc = jnp.zeros(x.shape, F32)
    for c in range(FF // FFN_CHUNK):
        lo = c * FFN_CHUNK
        a = _dot(xm, w1_ref[:, lo:lo + FFN_CHUNK])
        b = _dot(xm, w1_ref[:, FF + lo:FF + lo + FFN_CHUNK])
        h = (a * _sigmoid(a) * b).astype(BF)
        acc = acc + _dot(h, w2_ref[lo:lo + FFN_CHUNK, :])
    z = DN_ALPHA * x + (0.5 * gate) * acc
    o_ref[...] = _layer_norm(z, g_ref[...], b_ref[...])


def _half_ffn(x, mod, w1, w2, g, b, lay, k0, x_off, g_off):
    tm = lay.tm
    n = lay.tiles - g_off
    return pl.pallas_call(
        functools.partial(_ffn_kernel, k0=k0),
        grid=(n,),
        in_specs=[pl.BlockSpec((tm, D_MODEL), lambda i: (i + x_off, 0)),
                  pl.BlockSpec((1, 9, D_MODEL), lambda i: (lay.mod_index(i + g_off), 0, 0)),
                  _resident(w1.shape), _resident(w2.shape),
                  _const((1, D_MODEL)), _const((1, D_MODEL))],
        out_specs=pl.BlockSpec((tm, D_MODEL), lambda i: (i, 0)),
        out_shape=jax.ShapeDtypeStruct((n * tm, D_MODEL), F32),
        compiler_params=_params(1),
        name="half_ffn",
    )(x, mod, w1, w2, g, b)


_PROJ_GROUPS = (
    ("ax", 512, 0, None, 1.0), ("ay", 512, 0, None, 1.0),
    ("bq", 512, 16, 0, GQA_DH ** -0.5), ("bk", 128, 16, 1, 1.0), ("bv", 128, 0, None, 1.0),
    ("cq", 384, 0, None, 1.0), ("ckv", 256, 0, None, 1.0), ("ckr", 128, 8, 2, 1.0),
    ("dq", 512, 32, 3, 1.0), ("dk", 512, 32, 3, RET_DK ** -0.5), ("dv", 512, 0, None, 1.0), ("dg", 512, 0, None, 1.0),
)
_N_ROPE_TABLES = 4


def _proj_kernel(x_ref, mod_ref, w_ref, *refs):
    tabs = refs[:2 * _N_ROPE_TABLES]
    outs = refs[2 * _N_ROPE_TABLES:]
    x = x_ref[...]
    h = (x * (1.0 + mod_ref[0, 4:5, :]) + mod_ref[0, 3:4, :]).astype(BF)
    off = 0
    for (_, width, nf, tab, post), o_ref in zip(_PROJ_GROUPS, outs):
        y = _dot(h, w_ref[:, off:off + width])
        if nf:
            y = _rope(y, tabs[2 * tab][...], tabs[2 * tab + 1][...], nf)
        if post != 1.0:
            y = y * post
        o_ref[...] = y.astype(BF)
        off += width


def _mixer_proj(x, mod, w, tabs, lay):
    tm = lay.tm
    row = lambda i: (i, 0)
    tab_specs = [pl.BlockSpec((tm, t.shape[1]), lambda i: (lay.rope_index(i), 0)) for t in tabs]
    return pl.pallas_call(
        _proj_kernel,
        grid=(lay.tiles,),
        in_specs=[pl.BlockSpec((tm, D_MODEL), row),
                  pl.BlockSpec((1, 9, D_MODEL), lambda i: (lay.mod_index(i), 0, 0)),
                  _resident(w.shape)] + tab_specs,
        out_specs=[pl.BlockSpec((tm, g[1]), row) for g in _PROJ_GROUPS],
        out_shape=[jax.ShapeDtypeStruct((lay.n_tok, g[1]), BF) for g in _PROJ_GROUPS],
        compiler_params=_params(1),
        name="mixer_proj",
    )(x, mod, w, *tabs)


def _gate_kernel(x_ref, mod_ref, w_ref, o_ref):
    x = x_ref[...]
    h = (x * (1.0 + mod_ref[0, 4:5, :]) + mod_ref[0, 3:4, :]).astype(BF)
    for k in range(N_BRANCH):
        sl = slice(k * D_MODEL, (k + 1) * D_MODEL)
        o_ref[:, sl] = _dot(h, w_ref[:, sl]).astype(BF)


def _gate_proj(x, mod, w, lay, g_off):
    tm = lay.tm
    n = lay.tiles - g_off
    return pl.pallas_call(
        _gate_kernel,
        grid=(n,),
        in_specs=[pl.BlockSpec((tm, D_MODEL), lambda i: (i + g_off, 0)),
                  pl.BlockSpec((1, 9, D_MODEL), lambda i: (lay.mod_index(i + g_off), 0, 0)),
                  _resident(w.shape)],
        out_specs=pl.BlockSpec((tm, N_BRANCH * D_MODEL), lambda i: (i, 0)),
        out_shape=jax.ShapeDtypeStruct((n * tm, N_BRANCH * D_MODEL), BF),
        compiler_params=_params(1),
        name="gate_proj",
    )(x, mod, w)


def _conv4(x, w_ref, b_ref):
    T = x.shape[0]
    row = lax.broadcasted_iota(jnp.int32, x.shape, 0)
    y = b_ref[...] + w_ref[2:3, :] * x
    y = y + w_ref[0:1, :] * jnp.where(row >= 2, pltpu.roll(x, 2, axis=0), 0.0)
    y = y + w_ref[1:2, :] * jnp.where(row >= 1, pltpu.roll(x, 1, axis=0), 0.0)
    y = y + w_ref[3:4, :] * jnp.where(row < T - 1, pltpu.roll(x, T - 1, axis=0), 0.0)
    return y


def _scan8(a, b, reverse):
    row = lax.broadcasted_iota(jnp.int32, a.shape, 0)
    for s in (1, 2, 4):
        if reverse:
            keep = row < SUBLANES - s
            a_sh = pltpu.roll(a, SUBLANES - s, axis=0)
            b_sh = pltpu.roll(b, SUBLANES - s, axis=0)
        else:
            keep = row >= s
            a_sh = pltpu.roll(a, s, axis=0)
            b_sh = pltpu.roll(b, s, axis=0)
        b = jnp.where(keep, b + a * b_sh, b)
        a = jnp.where(keep, a * a_sh, a)
    return a, b


def _lru_segment(T):
    for seg in (36, 44, 28, 20, 12, 52, 60, 4, 8, 16, 32):
        if T % (SUBLANES * seg) == 0:
            return seg
    raise ValueError(f"no scan segment length for {T} rows")


def _lru_kernel(axc_ref, axl_ref, ayc_ref, ayl_ref, cw_ref, cb_ref, wg_ref, bg_ref, lam_ref,
                oc_ref, ol_ref, xa_s, a_s, b_s, h0_s, h1_s, ac_s, bc_s, *, Lc, S, rows, seg):
    W = LRU_W
    T = Lc + S
    NS = W // LANES
    G = SUBLANES * seg
    xa_s[0:Lc, :] = _conv4(axc_ref[...].astype(F32), cw_ref, cb_ref)
    xa_s[Lc:T, :] = _conv4(axl_ref[...].astype(F32), cw_ref, cb_ref)
    row8 = lax.broadcasted_iota(jnp.int32, (SUBLANES, LANES), 0)
    zero8 = jnp.zeros((SUBLANES, LANES), F32)

    for d in range(2):
        reverse = d == 1
        h_s = h1_s if reverse else h0_s
        neg = -lam_ref[d:d + 1, :]
        softplus = jnp.maximum(neg, 0.0) + jnp.log(1.0 + jnp.exp(-jnp.abs(neg)))
        c = (-0.5 * LRU_C * 1.4426950408889634) * softplus

        def gates(i, carry):
            r0 = pl.multiple_of(i * rows, rows)
            dst = pl.multiple_of(jnp.where(r0 < Lc, r0 + S, r0 - Lc), rows) if reverse else r0
            xa = xa_s[pl.ds(r0, rows), :]
            xb = xa.astype(BF)
            lo = 2 * d * W
            t_r = jnp.tanh(_dot(xb, wg_ref[:, lo:lo + W]) + bg_ref[:, lo:lo + W])
            t_i = jnp.tanh(_dot(xb, wg_ref[:, lo + W:lo + 2 * W]) + bg_ref[:, lo + W:lo + 2 * W])
            a = jnp.exp2(c * t_r + c)
            y = 1.0 - a * a
            root = jnp.where(y > 0.0, y * lax.rsqrt(y), 0.0)
            b = (root * (0.5 * xa)) * (t_i + 1.0)
            for k in range(NS):
                a_s[k, pl.ds(dst, rows), :] = a[:, k * LANES:(k + 1) * LANES]
                b_s[k, pl.ds(dst, rows), :] = b[:, k * LANES:(k + 1) * LANES]
            return carry

        lax.fori_loop(0, T // rows, gates, 0)

        def group(j, h_in):
            base = ((T // G - 1 - j) if reverse else j) * G
            acc_a, acc_b = [None] * NS, [None] * NS
            for i in (range(seg - 1, -1, -1) if reverse else range(seg)):
                for k in range(NS):
                    a = a_s[k, pl.ds(base + i, SUBLANES, stride=seg), :]
                    b = b_s[k, pl.ds(base + i, SUBLANES, stride=seg), :]
                    if acc_a[k] is None:
                        acc_a[k], acc_b[k] = a, b
                    else:
                        acc_b[k] = a * acc_b[k] + b
                        acc_a[k] = a * acc_a[k]
                    ac_s[k, i] = acc_a[k]
                    bc_s[k, i] = acc_b[k]
            h_out, enter = [], []
            for k in range(NS):
                tot_a, tot_b = _scan8(acc_a[k], acc_b[k], reverse)
                after = tot_a * h_in[k] + tot_b
                if reverse:
                    enter.append(jnp.where(row8 == SUBLANES - 1, h_in[k], pltpu.roll(after, SUBLANES - 1, axis=0)))
                    h_out.append(jnp.broadcast_to(after[0:1, :], after.shape))
                else:
                    enter.append(jnp.where(row8 == 0, h_in[k], pltpu.roll(after, 1, axis=0)))
                    h_out.append(jnp.broadcast_to(after[SUBLANES - 1:SUBLANES, :], after.shape))
            for i in range(seg):
                for k in range(NS):
                    h_s[k, pl.ds(base + i, SUBLANES, stride=seg), :] = ac_s[k, i] * enter[k] + bc_s[k, i]
            return tuple(h_out)

        lax.fori_loop(0, T // G, group, (zero8,) * NS)

    def gelu(v):
        return 0.5 * v * (1.0 + jnp.tanh(0.7978845608028654 * (v + 0.044715 * v * v * v)))

    for k in range(NS):
        sl = slice(k * LANES, (k + 1) * LANES)
        hc = h0_s[k, 0:Lc, :] + h1_s[k, S:T, :]
        hl = h0_s[k, Lc:T, :] + h1_s[k, 0:S, :]
        oc_ref[:, sl] = (hc * gelu(ayc_ref[:, sl].astype(F32))).astype(BF)
        ol_ref[:, sl] = (hl * gelu(ayl_ref[:, sl].astype(F32))).astype(BF)


def _lru(ax, ay, conv_w, conv_b, wg, bg, lam, lay):
    B, S, Lc = lay.B, lay.S, lay.Lc
    W = LRU_W
    T = Lc + S
    rows = 256
    while Lc % rows or S % rows:
        rows //= 2
    seg = _lru_segment(T)
    ctx = pl.BlockSpec((Lc, W), lambda b: (b, 0))
    lat = pl.BlockSpec((S, W), lambda b: (b + lay.lat_seq_block, 0))
    slabs = pltpu.VMEM((W // LANES, T, LANES), F32)
    part = pltpu.VMEM((W // LANES, seg, SUBLANES, LANES), F32)
    return pl.pallas_call(
        functools.partial(_lru_kernel, Lc=Lc, S=S, rows=rows, seg=seg),
        grid=(B,),
        in_specs=[ctx, lat, ctx, lat, _const((4, W)), _const((1, W)), _const(wg.shape), _const((1, 4 * W)), _const((2, W))],
        out_specs=[pl.BlockSpec((Lc, W), lambda b: (b, 0)), pl.BlockSpec((S, W), lambda b: (b, 0))],
        out_shape=[jax.ShapeDtypeStruct((lay.n_ctx, W), BF), jax.ShapeDtypeStruct((lay.n_lat, W), BF)],
        scratch_shapes=[pltpu.VMEM((T, W), F32), slabs, slabs, slabs, slabs, part, part],
        compiler_params=_params(1),
        name="rglru",
    )(ax, ax, ay, ay, conv_w, conv_b, wg, bg, lam)


def _split_heads_pair(x):
    xf = x.astype(F32)
    sw = pltpu.roll(xf, GQA_DH, axis=1)
    lane = lax.broadcasted_iota(jnp.int32, xf.shape, 1)
    low = lane < GQA_DH
    z = jnp.zeros_like(xf)
    head0 = (jnp.where(low, xf, z).astype(BF), jnp.where(low, z, sw).astype(BF))
    head1 = (jnp.where(low, sw, z).astype(BF), jnp.where(low, z, xf).astype(BF))
    return head0, head1


def _gqa_kernel(sink_ref, *refs, S, Lc, ctx_queries):
    if ctx_queries:
        q_ref, k_ref, v_ref, kx_ref, vx_ref, qx_ref, o_ref, ox_ref, ks_s, vs_s, kxs_s, vxs_s = refs
    else:
        q_ref, k_ref, v_ref, kx_ref, vx_ref, o_ref, ks_s, vs_s, kxs_s, vxs_s = refs
    blk = ATT_BLOCK
    nb = S // blk
    win = min(3 * blk, S)
    for src, dst in ((k_ref, ks_s), (v_ref, vs_s), (kx_ref, kxs_s), (vx_ref, vxs_s)):
        heads = _split_heads_pair(src[...])
        for j in range(GQA_HKV):
            for half in range(2):
                dst[2 * j + half] = heads[j][half]

    row = lax.broadcasted_iota(jnp.int32, (2 * blk, win), 0)
    col = lax.broadcasted_iota(jnp.int32, (2 * blk, win), 1)
    rel = (row % blk) - col
    top = row[:, 0:1] < blk

    def attend(q, kstart, valid):
        outs = []
        for j in range(GQA_HKV):
            q2 = jnp.concatenate([q[:, 2 * j * LANES:(2 * j + 1) * LANES], q[:, (2 * j + 1) * LANES:(2 * j + 2) * LANES]], axis=0)
            acc = jnp.zeros((2 * blk, LANES), F32)
            for half in range(2):
                i = 2 * j + half
                sink = jnp.where(top, sink_ref[4 * j + half], sink_ref[4 * j + 2 + half])
                s_c = _dot_nt(q2, kxs_s[i])
                m = jnp.maximum(jnp.max(s_c, axis=-1, keepdims=True), sink)
                if kstart is not None:
                    s_w = jnp.where(valid, _dot_nt(q2, ks_s[i, pl.ds(kstart, win), :]), MASK_VALUE)
                    m = jnp.maximum(m, jnp.max(s_w, axis=-1, keepdims=True))
                p_c = jnp.exp(s_c - m)
                den = jnp.sum(p_c, axis=-1, keepdims=True) + jnp.exp(sink - m)
                o = _dot(p_c.astype(BF), vxs_s[i])
                if kstart is not None:
                    p_w = jnp.exp(s_w - m)
                    den = den + jnp.sum(p_w, axis=-1, keepdims=True)
                    o = o + _dot(p_w.astype(BF), vs_s[i, pl.ds(kstart, win), :])
                acc = acc + o * (1.0 / den)
            outs += [acc[0:blk].astype(BF), acc[blk:2 * blk].astype(BF)]
        return jnp.concatenate(outs, axis=1)

    def block(n, carry):
        q0 = pl.multiple_of(n * blk, blk)
        kstart = pl.multiple_of(jnp.clip((n - 1) * blk, 0, S - win), blk)
        dist = rel + (q0 - kstart)
        valid = (dist <= WINDOW) & (dist >= -WINDOW)
        o_ref[pl.ds(q0, blk), :] = attend(q_ref[pl.ds(q0, blk), :], kstart, valid)
        return carry

    lax.fori_loop(0, nb, block, 0)
    if ctx_queries:
        for n in range(Lc // blk):
            ox_ref[n * blk:(n + 1) * blk, :] = attend(qx_ref[n * blk:(n + 1) * blk, :], None, None)


def _gqa(q, k, v, sink, lay, ctx_queries):
    B, S, Lc = lay.B, lay.S, lay.Lc
    W = GQA_HQ * GQA_DH
    lat = lambda w: pl.BlockSpec((S, w), lambda b: (lay.lat_seq_block + b, 0))
    ctx = lambda w: pl.BlockSpec((Lc, w), lambda b: (b, 0))
    in_specs = [pl.BlockSpec(memory_space=pltpu.SMEM), lat(W), lat(LANES), lat(LANES), ctx(LANES), ctx(LANES)]
    out_specs = [pl.BlockSpec((S, W), lambda b: (b, 0))]
    out_shape = [jax.ShapeDtypeStruct((lay.n_lat, W), BF)]
    args = [sink, q, k, v, k, v]
    if ctx_queries:
        in_specs.append(ctx(W))
        out_specs.append(ctx(W))
        out_shape.append(jax.ShapeDtypeStruct((lay.n_ctx, W), BF))
        args.append(q)
    return pl.pallas_call(
        functools.partial(_gqa_kernel, S=S, Lc=Lc, ctx_queries=ctx_queries),
        grid=(B,),
        in_specs=in_specs,
        out_specs=out_specs,
        out_shape=out_shape,
        scratch_shapes=[pltpu.VMEM((4, S, LANES), BF), pltpu.VMEM((4, S, LANES), BF),
                        pltpu.VMEM((4, Lc, LANES), BF), pltpu.VMEM((4, Lc, LANES), BF)],
        compiler_params=_params(1),
        name="gqa_window",
    )(*args)


def _rms(x, g):
    return x * lax.rsqrt(jnp.mean(x * x, axis=-1, keepdims=True) + LN_EPS) * g


def _mla_prep_kernel(cq_ref, ckv_ref, ckr_ref, gq_ref, gkv_ref, wq_ref, wk_ref, wv_ref, c_ref, s_ref,
                     q_ref, k_ref, v_ref):
    yq = _rms(cq_ref[...].astype(F32), gq_ref[...]).astype(BF)
    q = _rope(_dot(yq, wq_ref[...]), c_ref[...], s_ref[...], MLA_DR // 4)
    q_ref[...] = (q * ((MLA_DN + MLA_DR) ** -0.5)).astype(BF)
    ykv = _rms(ckv_ref[...].astype(F32), gkv_ref[...]).astype(BF)
    kr = ckr_ref[...].astype(F32)
    kr_all = pltpu.roll(jnp.concatenate([kr] * MLA_H, axis=1), MLA_DN, axis=1)
    k_ref[...] = (_dot(ykv, wk_ref[...]) + kr_all).astype(BF)
    v_ref[...] = _dot(ykv, wv_ref[...]).astype(BF)


def _mla_prep(cq, ckv, ckr, gq, gkv, wq, wk, wv, ctab, stab, lay):
    tm = lay.tm
    row = lambda i: (i, 0)
    tab = pl.BlockSpec((tm, MLA_H * LANES), lambda i: (lay.rope_index(i), 0))
    return pl.pallas_call(
        _mla_prep_kernel,
        grid=(lay.tiles,),
        in_specs=[pl.BlockSpec((tm, MLA_RQ), row), pl.BlockSpec((tm, MLA_RKV), row), pl.BlockSpec((tm, LANES), row),
                  _const((1, MLA_RQ)), _const((1, MLA_RKV)), _const(wq.shape), _const(wk.shape), _const(wv.shape), tab, tab],
        out_specs=[pl.BlockSpec((tm, MLA_H * LANES), row), pl.BlockSpec((tm, MLA_H * LANES), row),
                   pl.BlockSpec((tm, MLA_H * MLA_DV), row)],
        out_shape=[jax.ShapeDtypeStruct((lay.n_tok, MLA_H * LANES), BF), jax.ShapeDtypeStruct((lay.n_tok, MLA_H * LANES), BF),
                   jax.ShapeDtypeStruct((lay.n_tok, MLA_H * MLA_DV), BF)],
        compiler_params=_params(1),
        name="mla_prep",
    )(cq, ckv, ckr, gq, gkv, wq, wk, wv, ctab, stab)


def _mla_heads(q_ref, kv_refs, o_ref):
    lane = lax.broadcasted_iota(jnp.int32, (1, LANES), 1)
    low = lane < MLA_DV
    for pair in range(MLA_H // 2):
        vsl = slice(pair * LANES, (pair + 1) * LANES)
        acc = None
        for half in range(2):
            h = 2 * pair + half
            hsl = slice(h * LANES, (h + 1) * LANES)
            qh = q_ref[:, hsl]
            scores = [_dot_nt(qh, k_ref[:, hsl]) for k_ref, _ in kv_refs]
            m = functools.reduce(jnp.maximum, [jnp.max(s, axis=-1, keepdims=True) for s in scores])
            probs = [jnp.exp(s - m) for s in scores]
            den = functools.reduce(jnp.add, [jnp.sum(p, axis=-1, keepdims=True) for p in probs])
            keep = low if half == 0 else jnp.logical_not(low)
            o = None
            for p, (_, v_ref) in zip(probs, kv_refs):
                vh = jnp.where(keep, v_ref[:, vsl], jnp.zeros((), BF))
                t = _dot(p.astype(BF), vh)
                o = t if o is None else o + t
            o = o * (1.0 / den)
            acc = o if acc is None else acc + o
        o_ref[:, vsl] = acc.astype(BF)


def _mla_lat_kernel(q_ref, kc_ref, kl_ref, vc_ref, vl_ref, o_ref):
    _mla_heads(q_ref, ((kc_ref, vc_ref), (kl_ref, vl_ref)), o_ref)


def _mla_ctx_kernel(q_ref, kc_ref, vc_ref, o_ref):
    _mla_heads(q_ref, ((kc_ref, vc_ref),), o_ref)


def _mla_latent(q, k, v, lay):
    B, S, Lc = lay.B, lay.S, lay.Lc
    tq = min(512, S)
    nq = S // tq
    lat_q0 = lay.n_ctx // tq
    kw, vw = MLA_H * LANES, MLA_H * MLA_DV
    return pl.pallas_call(
        _mla_lat_kernel,
        grid=(B, nq),
        in_specs=[pl.BlockSpec((tq, kw), lambda b, i: (lat_q0 + b * nq + i, 0)),
                  pl.BlockSpec((Lc, kw), lambda b, i: (b, 0)),
                  pl.BlockSpec((S, kw), lambda b, i: (lay.lat_seq_block + b, 0)),
                  pl.BlockSpec((Lc, vw), lambda b, i: (b, 0)),
                  pl.BlockSpec((S, vw), lambda b, i: (lay.lat_seq_block + b, 0))],
        out_specs=pl.BlockSpec((tq, vw), lambda b, i: (b * nq + i, 0)),
        out_shape=jax.ShapeDtypeStruct((lay.n_lat, vw), BF),
        compiler_params=_params(2),
        name="mla_latent",
    )(q, k, k, v, v)


def _mla_context(q, k, v, lay):
    B, Lc = lay.B, lay.Lc
    kw, vw = MLA_H * LANES, MLA_H * MLA_DV
    blk = lambda w: pl.BlockSpec((Lc, w), lambda b: (b, 0))
    return pl.pallas_call(
        _mla_ctx_kernel,
        grid=(B,),
        in_specs=[blk(kw), blk(kw), blk(vw)],
        out_specs=blk(vw),
        out_shape=jax.ShapeDtypeStruct((lay.n_ctx, vw), BF),
        compiler_params=_params(1),
        name="mla_context",
    )(q, k, v)


def _ret_chunk(Lc, S):
    return 256 if Lc % 256 == 0 and S % 256 == 0 else RET_CHUNK


def _ret_kernel(dec_ref, qc_ref, ql_ref, kc_ref, kl_ref, vc_ref, vl_ref, gc_ref, gl_ref, gn_ref,
                oc_ref, ol_ref, kv_s, r_s, *, Lc, S, ch):
    nc, nl = Lc // ch, S // ch
    n = nc + nl
    lg_f = jnp.log(_sigmoid(dec_ref[0, 0:1, :]))
    lg_b = jnp.log(_sigmoid(dec_ref[0, 1:2, :]))
    pos_i = lax.broadcasted_iota(jnp.int32, (ch, ch), 0).astype(F32)
    pos_j = lax.broadcasted_iota(jnp.int32, (ch, ch), 1).astype(F32)
    diff = pos_i - pos_j
    fwd = diff >= 0.0
    mask = jnp.where(fwd, jnp.exp(jnp.where(fwd, diff, 0.0) * lg_f), jnp.exp(jnp.where(fwd, 0.0, -diff) * lg_b))
    pos = pos_i[:, 0:1]
    lf, lb = lg_f[:, :LANES], lg_b[:, :LANES]
    zeta = (jnp.exp((ch - 1.0 - pos) * lf), jnp.exp(pos * lb))
    xi = (jnp.exp((pos + 1.0) * lf), jnp.exp((ch - pos) * lb))
    decay = (jnp.exp(ch * lf), jnp.exp(ch * lb))

    def rows(c_ref, l_ref, c):
        if c < nc:
            return c_ref[c * ch:(c + 1) * ch, :]
        return l_ref[(c - nc) * ch:(c - nc + 1) * ch, :]

    for c in range(n):
        k = rows(kc_ref, kl_ref, c).astype(F32)
        kz = jnp.concatenate([(k * zeta[0]).astype(BF), (k * zeta[1]).astype(BF)], axis=1)
        kv_s[c] = _dot_tn(kz, rows(vc_ref, vl_ref, c))

    orders = (list(range(n)), list(range(nc - 1, -1, -1)) + list(range(n - 1, nc - 1, -1)))
    for d, order in enumerate(orders):
        R = jnp.zeros((RET_DK, RET_DV), F32)
        for c in order:
            r_s[c, d * RET_DK:(d + 1) * RET_DK, :] = R.astype(BF)
            R = decay[d] * R + kv_s[c, d * RET_DK:(d + 1) * RET_DK, :]

    for c in range(n):
        q = rows(qc_ref, ql_ref, c)
        s = _dot_nt(q, rows(kc_ref, kl_ref, c)) * mask
        qf = q.astype(F32)
        qx = jnp.concatenate([(qf * xi[0]).astype(BF), (qf * xi[1]).astype(BF)], axis=1)
        y = _dot(s.astype(BF), rows(vc_ref, vl_ref, c)) + _dot(qx, r_s[c])
        mu = jnp.mean(y, axis=-1, keepdims=True)
        yc = y - mu
        var = jnp.mean(yc * yc, axis=-1, keepdims=True)
        yn = yc * lax.rsqrt(var + LN_EPS) * gn_ref[...]
        g = rows(gc_ref, gl_ref, c).astype(F32)
        out = (g * _sigmoid(g) * yn).astype(BF)
        if c < nc:
            oc_ref[c * ch:(c + 1) * ch, :] = out
        else:
            ol_ref[(c - nc) * ch:(c - nc + 1) * ch, :] = out


def _retention(dq, dk, dv, dg, decay_lanes, gn, lay):
    B, S, Lc = lay.B, lay.S, lay.Lc
    ch = _ret_chunk(Lc, S)
    n = (Lc + S) // ch
    ctx = pl.BlockSpec((Lc, LANES), lambda b, h: (b, h))
    lat = pl.BlockSpec((S, LANES), lambda b, h: (lay.lat_seq_block + b, h))
    return pl.pallas_call(
        functools.partial(_ret_kernel, Lc=Lc, S=S, ch=ch),
        grid=(B, RET_H),
        in_specs=[pl.BlockSpec((1, 2, ch), lambda b, h: (h, 0, 0)),
                  ctx, lat, ctx, lat, ctx, lat, ctx, lat, pl.BlockSpec((1, LANES), lambda b, h: (0, h))],
        out_specs=[pl.BlockSpec((Lc, LANES), lambda b, h: (b, h)), pl.BlockSpec((S, LANES), lambda b, h: (b, h))],
        out_shape=[jax.ShapeDtypeStruct((lay.n_ctx, RET_H * RET_DV), BF), jax.ShapeDtypeStruct((lay.n_lat, RET_H * RET_DV), BF)],
        scratch_shapes=[pltpu.VMEM((n, 2 * RET_DK, RET_DV), F32), pltpu.VMEM((n, 2 * RET_DK, RET_DV), BF)],
        compiler_params=_params(2),
        name="retention",
    )(decay_lanes, dq, dq, dk, dk, dv, dv, dg, dg, gn)


def _merge_kernel(x_ref, mod_ref, a_ref, b_ref, c_ref, d_ref, gt_ref, wb_ref, wo_ref, g_ref, beta_ref, o_ref):
    acc = None
    for k, br in enumerate((a_ref, b_ref, c_ref, d_ref)):
        gate = _sigmoid(gt_ref[:, k * D_MODEL:(k + 1) * D_MODEL].astype(F32))
        t = gate * _dot(br[...], wb_ref[k])
        acc = t if acc is None else acc + t
    y = _dot(acc.astype(BF), wo_ref[...])
    z = DN_ALPHA * x_ref[...] + mod_ref[0, 5:6, :] * y
    o_ref[...] = _layer_norm(z, g_ref[...], beta_ref[...])


def _merge(x, mod, branches, gt, wb, wo, g, beta, lay, g_off, br_off):
    tm = lay.tm
    n = lay.tiles - g_off
    br = pl.BlockSpec((tm, BRANCH_W), lambda i: (i + br_off, 0))
    return pl.pallas_call(
        _merge_kernel,
        grid=(n,),
        in_specs=[pl.BlockSpec((tm, D_MODEL), lambda i: (i + g_off, 0)),
                  pl.BlockSpec((1, 9, D_MODEL), lambda i: (lay.mod_index(i + g_off), 0, 0)),
                  br, br, br, br,
                  pl.BlockSpec((tm, N_BRANCH * D_MODEL), lambda i: (i, 0)),
                  _resident(wb.shape), _resident(wo.shape), _const((1, D_MODEL)), _const((1, D_MODEL))],
        out_specs=pl.BlockSpec((tm, D_MODEL), lambda i: (i, 0)),
        out_shape=jax.ShapeDtypeStruct((n * tm, D_MODEL), F32),
        compiler_params=_params(1),
        name="gated_merge",
    )(x, mod, *branches, gt, wb, wo, g, beta)


def _rope_tables(S, head_dim, lead, reps, tail, tm):
    nf = head_dim // 4
    inv = ROPE_BASE ** (-jnp.arange(nf, dtype=F32) / nf)
    t = jnp.arange(S)
    ar = (t // GRID_W).astype(F32)[:, None] * inv[None, :]
    ac = (t % GRID_W).astype(F32)[:, None] * inv[None, :]
    cos = jnp.concatenate([jnp.cos(ar), jnp.cos(ar), jnp.cos(ac), jnp.cos(ac)], -1)
    sin = jnp.concatenate([-jnp.sin(ar), jnp.sin(ar), -jnp.sin(ac), jnp.sin(ac)], -1)
    one = lambda w: jnp.ones((S, w), F32)
    zero = lambda w: jnp.zeros((S, w), F32)
    cos = jnp.tile(jnp.concatenate([one(lead), cos, one(tail)], -1), (1, reps))
    sin = jnp.tile(jnp.concatenate([zero(lead), sin, zero(tail)], -1), (1, reps))
    W = cos.shape[1]
    cos = jnp.concatenate([cos, jnp.ones((tm, W), F32)], 0)
    sin = jnp.concatenate([sin, jnp.zeros((tm, W), F32)], 0)
    return cos, sin


def _block_diag(w):
    n, bw, _ = w.shape
    eye = jnp.eye(n, dtype=w.dtype)
    return (eye[:, None, :, None] * w[:, :, None, :]).reshape(n * bw, n * bw)


def _layer_weights(l, w_in, lru_w_a, lru_b_a, lru_w_x, lru_b_x, mla_w_uq, mla_w_ukv):
    cuts = [0]
    for s in IN_SIZES:
        cuts.append(cuts[-1] + s)
    cols = [w_in[l][:, cuts[i]:cuts[i + 1]] for i in range(len(IN_SIZES))]
    cols[7] = jnp.pad(cols[7], ((0, 0), (0, LANES - MLA_DR)))
    w_proj = jnp.concatenate(cols[:12], axis=1).astype(BF)
    w_gate = cols[12].astype(BF)
    wg = (0.5 * jnp.concatenate([_block_diag(lru_w_a[l, 0]), _block_diag(lru_w_x[l, 0]),
                                 _block_diag(lru_w_a[l, 1]), _block_diag(lru_w_x[l, 1])], axis=1)).astype(BF)
    bg = 0.5 * jnp.concatenate([lru_b_a[l, 0], lru_b_x[l, 0], lru_b_a[l, 1], lru_b_x[l, 1]])[None, :]
    uq = mla_w_uq[l].reshape(MLA_RQ, MLA_H, MLA_DN + MLA_DR)
    wq = jnp.pad(uq, ((0, 0), (0, 0), (0, LANES - MLA_DN - MLA_DR))).reshape(MLA_RQ, MLA_H * LANES).astype(BF)
    ukv = mla_w_ukv[l].reshape(MLA_RKV, MLA_H, MLA_DN + MLA_DV)
    wk = jnp.pad(ukv[:, :, :MLA_DN], ((0, 0), (0, 0), (0, LANES - MLA_DN))).reshape(MLA_RKV, MLA_H * LANES).astype(BF)
    wv = ukv[:, :, MLA_DN:].reshape(MLA_RKV, MLA_H * MLA_DV).astype(BF)
    return w_proj, w_gate, wg, bg, wq, wk, wv


def kernel(x, c, ctx, c_ctx, w_mod, b_mod, ln_g, ln_b, ffn_w_in, ffn_w_out, w_in, lru_conv_w, lru_conv_b, lru_w_a, lru_b_a, lru_w_x, lru_b_x, lru_lambda, gqa_sink, mla_q_norm, mla_kv_norm, mla_w_uq, mla_w_ukv, ret_decay, ret_gn_g, w_branch, w_out):
    B, S, D = x.shape
    Lc = ctx.shape[1]
    lay = _Layout(B, S, Lc)
    tm = lay.tm

    n_cond = -(-(B + 1) // SUBLANES) * SUBLANES
    cond = jnp.concatenate([c, c_ctx[None, :], jnp.zeros((n_cond - B - 1, D), F32)], axis=0)
    mod_all = _modulation(cond, w_mod, b_mod)[:, :B + 1].reshape(DEPTH, B + 1, 9, D)

    tabs = []
    for head_dim, reps, tail in ((GQA_DH, GQA_HQ, 0), (GQA_DH, GQA_HKV, 0), (MLA_DR, 1, LANES - MLA_DR), (RET_DK, RET_H, 0)):
        tabs.extend(_rope_tables(S, head_dim, 0, reps, tail, tm))
    mla_c, mla_s = _rope_tables(S, MLA_DR, MLA_DN, MLA_H, LANES - MLA_DN - MLA_DR, tm)

    tok = jnp.concatenate([ctx.reshape(B * Lc, D), x.reshape(B * S, D)], axis=0)
    row = lambda v: v[None, :]
    ct = lay.ctx_tiles

    for l in range(DEPTH):
        last = l == DEPTH - 1
        mod = mod_all[l]
        w_proj, w_gate, wg, bg, wq, wk, wv = _layer_weights(
            l, w_in, lru_w_a, lru_b_a, lru_w_x, lru_b_x, mla_w_uq, mla_w_ukv)
        dec = jnp.broadcast_to(ret_decay[l].T[:, :, None], (RET_H, 2, _ret_chunk(Lc, S)))
        ffn1 = (ffn_w_in[l, 0].astype(BF), ffn_w_out[l, 0].astype(BF))
        ffn2 = (ffn_w_in[l, 1].astype(BF), ffn_w_out[l, 1].astype(BF))

        tok = _half_ffn(tok, mod, *ffn1, row(ln_g[l, 0]), row(ln_b[l, 0]), lay, 0, 0, 0)

        ax, ay, bq, bk, bv, cq, ckv, ckr, dq, dk, dv, dg = _mixer_proj(tok, mod, w_proj, tabs, lay)
        gt = _gate_proj(tok, mod, w_gate, lay, ct if last else 0)

        a_c, a_l = _lru(ax, ay, lru_conv_w[l], row(lru_conv_b[l]), wg, bg, lru_lambda[l], lay)
        b_out = _gqa(bq, bk, bv, gqa_sink[l], lay, ctx_queries=not last)
        mq, mk, mv = _mla_prep(cq, ckv, ckr, row(mla_q_norm[l]), row(mla_kv_norm[l]), wq, wk, wv, mla_c, mla_s, lay)
        c_l = _mla_latent(mq, mk, mv, lay)
        d_c, d_l = _retention(dq, dk, dv, dg, dec, row(ret_gn_g[l]), lay)

        wb = w_branch[l].astype(BF)
        wo = w_out[l].astype(BF)
        if last:
            branches = (a_l, b_out[0], c_l, d_l)
            tok = _merge(tok, mod, branches, gt, wb, wo, row(ln_g[l, 1]), row(ln_b[l, 1]), lay, ct, 0)
            tok = _half_ffn(tok, mod, *ffn2, row(ln_g[l, 2]), row(ln_b[l, 2]), lay, 6, 0, ct)
        else:
            c_c = _mla_context(mq, mk, mv, lay)
            branches = (jnp.concatenate([a_c, a_l], 0), jnp.concatenate([b_out[1], b_out[0]], 0),
                        jnp.concatenate([c_c, c_l], 0), jnp.concatenate([d_c, d_l], 0))
            tok = _merge(tok, mod, branches, gt, wb, wo, row(ln_g[l, 1]), row(ln_b[l, 1]), lay, 0, 0)
            tok = _half_ffn(tok, mod, *ffn2, row(ln_g[l, 2]), row(ln_b[l, 2]), lay, 6, 0, 0)

    return tok.reshape(B, S, D)
```

```python
import functools

import jax
import jax.numpy as jnp
from jax import lax
from jax.experimental import pallas as pl
from jax.experimental.pallas import tpu as pltpu

D_MODEL = 1024
DEPTH = 2
GRID_W = 64
N_BRANCH = 4
BRANCH_W = 512
LRU_W = 512
LRU_BLOCKS = 8
LRU_BW = LRU_W // LRU_BLOCKS
LRU_C = 8.0
GQA_HQ = 8
GQA_HKV = 2
GQA_DH = 64
WINDOW = 128
ATT_BLOCK = 128
MLA_H = 8
MLA_RQ = 384
MLA_RKV = 256
MLA_DN = 64
MLA_DR = 32
MLA_DV = 64
RET_H = 4
RET_DK = 128
RET_DV = 128
RET_CHUNK = 128
FF = 2816
ROPE_BASE = 10000.0
LN_EPS = 1e-5
MASK_VALUE = -1e30
DN_ALPHA = (2 * DEPTH) ** 0.25
IN_SIZES = (LRU_W, LRU_W, GQA_HQ * GQA_DH, GQA_HKV * GQA_DH, GQA_HKV * GQA_DH, MLA_RQ, MLA_RKV, MLA_DR,
            RET_H * RET_DK, RET_H * RET_DK, RET_H * RET_DV, RET_H * RET_DV, N_BRANCH * D_MODEL)

LANES = 128
SUBLANES = 8
VMEM_LIMIT = 56 * 1024 * 1024
MXU_DIM = 256
FFN_CHUNKS = (6 * MXU_DIM, 5 * MXU_DIM)
LOG2E = 1.4426950408889634
BF = jnp.bfloat16
F32 = jnp.float32


def _params(n_axes):
    return pltpu.CompilerParams(dimension_semantics=("arbitrary",) * n_axes, vmem_limit_bytes=VMEM_LIMIT)


def _resident(shape):
    nd = len(shape)
    return pl.BlockSpec(shape, lambda *_: (0,) * nd, pipeline_mode=pl.Buffered(1))


def _const(shape):
    nd = len(shape)
    return pl.BlockSpec(shape, lambda *_: (0,) * nd)


def _layer_norm(z, g, b):
    mu = jnp.mean(z, axis=-1, keepdims=True)
    zc = z - mu
    var = jnp.mean(zc * zc, axis=-1, keepdims=True)
    return zc * lax.rsqrt(var + LN_EPS) * g + b


def _sigmoid(x):
    return 1.0 / (1.0 + jnp.exp(-x))


def _dot(a, b):
    return jnp.dot(a, b, preferred_element_type=F32)


def _dot_nt(a, b):
    return lax.dot_general(a, b, (((1,), (1,)), ((), ())), preferred_element_type=F32)


def _dot_tn(a, b):
    return lax.dot_general(a, b, (((0,), (0,)), ((), ())), preferred_element_type=F32)


def _rope(y, c, s, nf):
    w = y.shape[-1]
    lane = lax.broadcasted_iota(jnp.int32, y.shape, 1)
    partner = jnp.where((lane % (2 * nf)) < nf, pltpu.roll(y, w - nf, axis=1), pltpu.roll(y, nf, axis=1))
    return y * c + partner * s


class _Layout:
    def __init__(self, B, S, Lc):
        self.B, self.S, self.Lc = B, S, Lc
        self.n_ctx, self.n_lat = B * Lc, B * S
        self.n_tok = self.n_ctx + self.n_lat
        tm = 512
        while S % tm or self.n_ctx % tm:
            tm //= 2
        self.tm = tm
        self.ctx_tiles = self.n_ctx // tm
        self.tiles_per_seq = S // tm
        self.tiles = self.ctx_tiles + B * self.tiles_per_seq
        assert self.n_ctx % S == 0, "latent rows must start on a whole-sequence block"
        self.lat_seq_block = self.n_ctx // S
        self.lat_att_block = self.n_ctx // ATT_BLOCK

    def mod_index(self, g):
        return jnp.where(g < self.ctx_tiles, self.B, (g - self.ctx_tiles) // self.tiles_per_seq)

    def rope_index(self, g):
        return jnp.where(g < self.ctx_tiles, self.tiles_per_seq, (g - self.ctx_tiles) % self.tiles_per_seq)


def _mod_kernel(s_ref, w_ref, b_ref, o_ref):
    s = s_ref[...]
    s = s * _sigmoid(s)
    o_ref[0] = _dot(s.astype(BF), w_ref[0].astype(BF)) + b_ref[0]


def _modulation(cond, w_mod, b_mod):
    R = cond.shape[0]
    tn = 1024
    n9 = 9 * D_MODEL
    return pl.pallas_call(
        _mod_kernel,
        grid=(DEPTH, n9 // tn),
        in_specs=[pl.BlockSpec((R, D_MODEL), lambda l, j: (0, 0)),
                  pl.BlockSpec((1, D_MODEL, tn), lambda l, j: (l, 0, j)),
                  pl.BlockSpec((1, 1, tn), lambda l, j: (l, 0, j))],
        out_specs=pl.BlockSpec((1, R, tn), lambda l, j: (l, 0, j)),
        out_shape=jax.ShapeDtypeStruct((DEPTH, R, n9), F32),
        compiler_params=_params(2),
        name="modulation",
    )(cond, w_mod, b_mod.reshape(DEPTH, 1, n9))


def _row_specs(src, width, lay, g_off):
    tm, ct = lay.tm, lay.ctx_tiles
    if isinstance(src, tuple):
        assert g_off == 0
        return [pl.BlockSpec((tm, width), lambda i: (jnp.minimum(i, ct - 1), 0)),
                pl.BlockSpec((tm, width), lambda i: (jnp.maximum(i - ct, 0), 0))], list(src)
    assert src.shape[0] == lay.n_tok or (src.shape[0] == lay.n_lat and g_off == ct)
    off = g_off if src.shape[0] == lay.n_tok else 0
    return [pl.BlockSpec((tm, width), lambda i: (i + off, 0))], [src]


def _rows(refs, ct):
    if len(refs) == 2:
        return jnp.where(pl.program_id(0) < ct, refs[0][...], refs[1][...])
    return refs[0][...]


def _ffn_kernel(*refs, k0, n_x, ct):
    mod_ref, w1_ref, w2_ref, g_ref, b_ref, o_ref = refs[n_x:]
    x = _rows(refs[:n_x], ct)
    shift = mod_ref[0, k0:k0 + 1, :]
    scale = mod_ref[0, k0 + 1:k0 + 2, :]
    gate = mod_ref[0, k0 + 2:k0 + 3, :]
    xm = (x * (1.0 + scale) + shift).astype(BF)
    acc = jnp.zeros(x.shape, F32)
    assert sum(FFN_CHUNKS) == FF
    lo = 0
    for width in FFN_CHUNKS:
        a = _dot(xm, w1_ref[:, lo:lo + width])
        b = _dot(xm, w1_ref[:, FF + lo:FF + lo + width])
        h = (a * _sigmoid(a) * b).astype(BF)
        acc = acc + _dot(h, w2_ref[lo:lo + width, :])
        lo += width
    z = DN_ALPHA * x + (0.5 * gate) * acc
    o_ref[...] = _layer_norm(z, g_ref[...], b_ref[...])


def _half_ffn(x, mod, w1_all, w2_all, l, j, g, b, lay, k0, g_off):
    tm = lay.tm
    n = lay.tiles - g_off
    x_specs, x_args = _row_specs(x, D_MODEL, lay, g_off)
    pick = lambda *_: (l, j, 0, 0)
    return pl.pallas_call(
        functools.partial(_ffn_kernel, k0=k0, n_x=len(x_args), ct=lay.ctx_tiles),
        grid=(n,),
        in_specs=x_specs + [
            pl.BlockSpec((1, 9, D_MODEL), lambda i: (lay.mod_index(i + g_off), 0, 0)),
            pl.BlockSpec((None, None, D_MODEL, 2 * FF), pick, pipeline_mode=pl.Buffered(1)),
            pl.BlockSpec((None, None, FF, D_MODEL), pick, pipeline_mode=pl.Buffered(1)),
            _const((1, D_MODEL)), _const((1, D_MODEL))],
        out_specs=pl.BlockSpec((tm, D_MODEL), lambda i: (i, 0)),
        out_shape=jax.ShapeDtypeStruct((n * tm, D_MODEL), F32),
        compiler_params=_params(1),
        name="half_ffn",
    )(*x_args, mod, w1_all, w2_all, g, b)


_PROJ_GROUPS = (
    ("ax", 512, 0, None, 1.0), ("ay", 512, 0, None, 1.0),
    ("bq", 512, 16, 0, GQA_DH ** -0.5 * LOG2E), ("bk", 128, 16, 1, 1.0), ("bv", 128, 0, None, 1.0),
    ("cq", 384, 0, None, 1.0), ("ckv", 256, 0, None, 1.0), ("ckr", 128, 8, 2, 1.0),
    ("dq", 512, 32, 3, 1.0), ("dk", 512, 32, 3, RET_DK ** -0.5), ("dv", 512, 0, None, 1.0), ("dg", 512, 0, None, 1.0),
)
_N_ROPE_TABLES = 4


def _rope_store(o_ref, y, tab, nf, post):
    rc_ref, rs_ref, cc_ref, cs_ref = tab
    for g in range(y.shape[0] // GRID_W):
        c = rc_ref[g:g + 1, :] * cc_ref[...]
        s = rs_ref[g:g + 1, :] + cs_ref[...]
        sl = slice(g * GRID_W, (g + 1) * GRID_W)
        o_ref[sl, :] = (_rope(y[sl, :], c, s, nf) * post).astype(o_ref.dtype)


def _proj_kernel(x_ref, mod_ref, w_ref, *refs):
    tabs = refs[:4 * _N_ROPE_TABLES]
    outs = refs[4 * _N_ROPE_TABLES:]
    x = x_ref[...]
    h = (x * (1.0 + mod_ref[0, 4:5, :]) + mod_ref[0, 3:4, :]).astype(BF)
    off = 0
    pending = []
    for group, o_ref in zip(_PROJ_GROUPS, outs):
        pending.append((group, o_ref))
        total = sum(g[1] for g, _ in pending)
        if total % MXU_DIM:
            continue
        y_all = _dot(h, w_ref[:, off:off + total])
        lo = 0
        for (_, width, nf, tab, post), o in pending:
            y = y_all[:, lo:lo + width]
            if nf:
                _rope_store(o, y, tabs[4 * tab:4 * tab + 4], nf, post)
            else:
                o[...] = y.astype(BF)
            lo += width
        off += total
        pending = []
    assert not pending


def _rope_specs(tab, lay):
    w = tab[0].shape[1]
    row = pl.BlockSpec((SUBLANES, w), lambda i: (lay.rope_index(i), 0))
    col = pl.BlockSpec((None, GRID_W, w), lambda i: (jnp.where(i < lay.ctx_tiles, 1, 0), 0, 0))
    return [row, row, col, col]


def _mixer_proj(x, mod, w, tabs, lay):
    tm = lay.tm
    row = lambda i: (i, 0)
    tab_specs, tab_args = [], []
    for tab in tabs:
        tab_specs += _rope_specs(tab, lay)
        tab_args += list(tab)
    return pl.pallas_call(
        _proj_kernel,
        grid=(lay.tiles,),
        in_specs=[pl.BlockSpec((tm, D_MODEL), row),
                  pl.BlockSpec((1, 9, D_MODEL), lambda i: (lay.mod_index(i), 0, 0)),
                  _resident(w.shape)] + tab_specs,
        out_specs=[pl.BlockSpec((tm, g[1]), row) for g in _PROJ_GROUPS],
        out_shape=[jax.ShapeDtypeStruct((lay.n_tok, g[1]), BF) for g in _PROJ_GROUPS],
        compiler_params=_params(1),
        name="mixer_proj",
    )(x, mod, w, *tab_args)


def _conv4(x, w_ref, b_ref):
    T = x.shape[0]
    row = lax.broadcasted_iota(jnp.int32, x.shape, 0)
    y = b_ref[...] + w_ref[2:3, :] * x
    y = y + w_ref[0:1, :] * jnp.where(row >= 2, pltpu.roll(x, 2, axis=0), 0.0)
    y = y + w_ref[1:2, :] * jnp.where(row >= 1, pltpu.roll(x, 1, axis=0), 0.0)
    y = y + w_ref[3:4, :] * jnp.where(row < T - 1, pltpu.roll(x, T - 1, axis=0), 0.0)
    return y


def _scan8(a, b, reverse):
    row = lax.broadcasted_iota(jnp.int32, a.shape, 0)
    for s in (1, 2, 4):
        if reverse:
            keep = row < SUBLANES - s
            a_sh = pltpu.roll(a, SUBLANES - s, axis=0)
            b_sh = pltpu.roll(b, SUBLANES - s, axis=0)
        else:
            keep = row >= s
            a_sh = pltpu.roll(a, s, axis=0)
            b_sh = pltpu.roll(b, s, axis=0)
        b = jnp.where(keep, b + a * b_sh, b)
        a = jnp.where(keep, a * a_sh, a)
    return a, b


def _lru_segment(T):
    for seg in (36, 44, 28, 20, 12, 52, 60, 4, 8, 16, 32):
        if T % (SUBLANES * seg) == 0:
            return seg
    raise ValueError(f"no scan segment length for {T} rows")


def _lru_kernel(axc_ref, axl_ref, ayc_ref, ayl_ref, cw_ref, cb_ref, wg_ref, bg_ref, lam_ref,
                oc_ref, ol_ref, xa_s, a_s, b_s, h0_s, h1_s, ac_s, bc_s, *, Lc, S, rows, seg):
    W = LRU_W
    T = Lc + S
    NS = W // LANES
    G = SUBLANES * seg
    xa_s[0:Lc, :] = _conv4(axc_ref[...].astype(F32), cw_ref, cb_ref)
    xa_s[Lc:T, :] = _conv4(axl_ref[...].astype(F32), cw_ref, cb_ref)
    row8 = lax.broadcasted_iota(jnp.int32, (SUBLANES, LANES), 0)
    zero8 = jnp.zeros((SUBLANES, LANES), F32)

    for d in range(2):
        reverse = d == 1
        h_s = h1_s if reverse else h0_s
        neg = -lam_ref[d:d + 1, :]
        softplus = jnp.maximum(neg, 0.0) + jnp.log(1.0 + jnp.exp(-jnp.abs(neg)))
        c = (-0.5 * LRU_C * 1.4426950408889634) * softplus

        def gates(i, carry):
            r0 = pl.multiple_of(i * rows, rows)
            dst = pl.multiple_of(jnp.where(r0 < Lc, r0 + S, r0 - Lc), rows) if reverse else r0
            xa = xa_s[pl.ds(r0, rows), :]
            xb = xa.astype(BF)
            lo = 2 * d * W
            t_r = jnp.tanh(_dot(xb, wg_ref[:, lo:lo + W]) + bg_ref[:, lo:lo + W])
            t_i = jnp.tanh(_dot(xb, wg_ref[:, lo + W:lo + 2 * W]) + bg_ref[:, lo + W:lo + 2 * W])
            a = jnp.exp2(c * t_r + c)
            y = 1.0 - a * a
            root = jnp.where(y > 0.0, y * lax.rsqrt(y), 0.0)
            b = (root * (0.5 * xa)) * (t_i + 1.0)
            for k in range(NS):
                a_s[k, pl.ds(dst, rows), :] = a[:, k * LANES:(k + 1) * LANES]
                b_s[k, pl.ds(dst, rows), :] = b[:, k * LANES:(k + 1) * LANES]
            return carry

        lax.fori_loop(0, T // rows, gates, 0)

        def group(j, h_in):
            base = ((T // G - 1 - j) if reverse else j) * G
            acc_a, acc_b = [None] * NS, [None] * NS
            for i in (range(seg - 1, -1, -1) if reverse else range(seg)):
                for k in range(NS):
                    a = a_s[k, pl.ds(base + i, SUBLANES, stride=seg), :]
                    b = b_s[k, pl.ds(base + i, SUBLANES, stride=seg), :]
                    if acc_a[k] is None:
                        acc_a[k], acc_b[k] = a, b
                    else:
                        acc_b[k] = a * acc_b[k] + b
                        acc_a[k] = a * acc_a[k]
                    ac_s[k, i] = acc_a[k]
                    bc_s[k, i] = acc_b[k]
            h_out, enter = [], []
            for k in range(NS):
                tot_a, tot_b = _scan8(acc_a[k], acc_b[k], reverse)
                after = tot_a * h_in[k] + tot_b
                if reverse:
                    enter.append(jnp.where(row8 == SUBLANES - 1, h_in[k], pltpu.roll(after, SUBLANES - 1, axis=0)))
                    h_out.append(jnp.broadcast_to(after[0:1, :], after.shape))
                else:
                    enter.append(jnp.where(row8 == 0, h_in[k], pltpu.roll(after, 1, axis=0)))
                    h_out.append(jnp.broadcast_to(after[SUBLANES - 1:SUBLANES, :], after.shape))
            for i in range(seg):
                for k in range(NS):
                    h_s[k, pl.ds(base + i, SUBLANES, stride=seg), :] = ac_s[k, i] * enter[k] + bc_s[k, i]
            return tuple(h_out)

        lax.fori_loop(0, T // G, group, (zero8,) * NS)

    def gelu(v):
        return 0.5 * v * (1.0 + jnp.tanh(0.7978845608028654 * (v + 0.044715 * v * v * v)))

    for k in range(NS):
        sl = slice(k * LANES, (k + 1) * LANES)
        hc = h0_s[k, 0:Lc, :] + h1_s[k, S:T, :]
        hl = h0_s[k, Lc:T, :] + h1_s[k, 0:S, :]
        oc_ref[:, sl] = (hc * gelu(ayc_ref[:, sl].astype(F32))).astype(BF)
        ol_ref[:, sl] = (hl * gelu(ayl_ref[:, sl].astype(F32))).astype(BF)


def _lru(ax, ay, conv_w, conv_b, wg, bg, lam, lay):
    B, S, Lc = lay.B, lay.S, lay.Lc
    W = LRU_W
    T = Lc + S
    rows = 256
    while Lc % rows or S % rows:
        rows //= 2
    seg = _lru_segment(T)
    ctx = pl.BlockSpec((Lc, W), lambda b: (b, 0))
    lat = pl.BlockSpec((S, W), lambda b: (b + lay.lat_seq_block, 0))
    slabs = pltpu.VMEM((W // LANES, T, LANES), F32)
    part = pltpu.VMEM((W // LANES, seg, SUBLANES, LANES), F32)
    return pl.pallas_call(
        functools.partial(_lru_kernel, Lc=Lc, S=S, rows=rows, seg=seg),
        grid=(B,),
        in_specs=[ctx, lat, ctx, lat, _const((4, W)), _const((1, W)), _const(wg.shape), _const((1, 4 * W)), _const((2, W))],
        out_specs=[pl.BlockSpec((Lc, W), lambda b: (b, 0)), pl.BlockSpec((S, W), lambda b: (b, 0))],
        out_shape=[jax.ShapeDtypeStruct((lay.n_ctx, W), BF), jax.ShapeDtypeStruct((lay.n_lat, W), BF)],
        scratch_shapes=[pltpu.VMEM((T, W), F32), slabs, slabs, slabs, slabs, part, part],
        compiler_params=_params(1),
        name="rglru",
    )(ax, ax, ay, ay, conv_w, conv_b, wg, bg, lam)


def _split_heads_pair(x, ones_lane):
    xf = x.astype(F32)
    sw = pltpu.roll(xf, GQA_DH, axis=1)
    lane = lax.broadcasted_iota(jnp.int32, xf.shape, 1)
    low = lane < GQA_DH
    lo_fill = jnp.where(lane == LANES - 1, 1.0, 0.0) if ones_lane else jnp.zeros_like(xf)
    hi_fill = jnp.where(lane == 0, 1.0, 0.0) if ones_lane else jnp.zeros_like(xf)
    head0 = (jnp.where(low, xf, lo_fill).astype(BF), jnp.where(low, hi_fill, sw).astype(BF))
    head1 = (jnp.where(low, sw, lo_fill).astype(BF), jnp.where(low, hi_fill, xf).astype(BF))
    return head0, head1


def _gqa_kernel(sink_ref, *refs, S, Lc, ctx_queries):
    if ctx_queries:
        q_ref, k_ref, v_ref, kx_ref, vx_ref, qx_ref, o_ref, ox_ref, ks_s, vs_s, kxs_s, vxs_s = refs
    else:
        q_ref, k_ref, v_ref, kx_ref, vx_ref, o_ref, ks_s, vs_s, kxs_s, vxs_s = refs
    blk = ATT_BLOCK
    nb = S // blk
    win = min(3 * blk, S)
    for src, dst, is_value in ((k_ref, ks_s, False), (v_ref, vs_s, True), (kx_ref, kxs_s, False), (vx_ref, vxs_s, True)):
        heads = _split_heads_pair(src[...], is_value)
        for j in range(GQA_HKV):
            for half in range(2):
                dst[2 * j + half] = heads[j][half]

    row = lax.broadcasted_iota(jnp.int32, (2 * blk, win), 0)
    col = lax.broadcasted_iota(jnp.int32, (2 * blk, win), 1)
    rel = (row % blk) - col
    top = row[:, 0:1] < blk
    low = lax.broadcasted_iota(jnp.int32, (1, LANES), 1) < GQA_DH

    def attend(q, kstart, valid):
        outs = []
        for j in range(GQA_HKV):
            q2 = jnp.concatenate([q[:, 2 * j * LANES:(2 * j + 1) * LANES], q[:, (2 * j + 1) * LANES:(2 * j + 2) * LANES]], axis=0)
            acc = jnp.zeros((2 * blk, LANES), F32)
            for half in range(2):
                i = 2 * j + half
                sink = jnp.where(top, sink_ref[4 * j + half] * LOG2E, sink_ref[4 * j + 2 + half] * LOG2E)
                s_c = _dot_nt(q2, kxs_s[i])
                m = jnp.maximum(jnp.max(s_c, axis=-1, keepdims=True), sink)
                if kstart is not None:
                    s_w = jnp.where(valid, _dot_nt(q2, ks_s[i, pl.ds(kstart, win), :]), MASK_VALUE)
                    m = jnp.maximum(m, jnp.max(s_w, axis=-1, keepdims=True))
                t = _dot(jnp.exp2(s_c - m).astype(BF), vxs_s[i])
                if kstart is not None:
                    t = t + _dot(jnp.exp2(s_w - m).astype(BF), vs_s[i, pl.ds(kstart, win), :])
                ones_lane = LANES - 1 if half == 0 else 0
                den = t[:, ones_lane:ones_lane + 1] + jnp.exp2(sink - m)
                acc = acc + jnp.where(low if half == 0 else jnp.logical_not(low), t * (1.0 / den), 0.0)
            outs += [acc[0:blk].astype(BF), acc[blk:2 * blk].astype(BF)]
        return jnp.concatenate(outs, axis=1)

    def block(n, carry):
        q0 = pl.multiple_of(n * blk, blk)
        kstart = pl.multiple_of(jnp.clip((n - 1) * blk, 0, S - win), blk)
        dist = rel + (q0 - kstart)
        valid = (dist <= WINDOW) & (dist >= -WINDOW)
        o_ref[pl.ds(q0, blk), :] = attend(q_ref[pl.ds(q0, blk), :], kstart, valid)
        return carry

    lax.fori_loop(0, nb, block, 0)
    if ctx_queries:
        for n in range(Lc // blk):
            ox_ref[n * blk:(n + 1) * blk, :] = attend(qx_ref[n * blk:(n + 1) * blk, :], None, None)


def _gqa(q, k, v, sink, lay, ctx_queries):
    B, S, Lc = lay.B, lay.S, lay.Lc
    W = GQA_HQ * GQA_DH
    lat = lambda w: pl.BlockSpec((S, w), lambda b: (lay.lat_seq_block + b, 0))
    ctx = lambda w: pl.BlockSpec((Lc, w), lambda b: (b, 0))
    in_specs = [pl.BlockSpec(memory_space=pltpu.SMEM), lat(W), lat(LANES), lat(LANES), ctx(LANES), ctx(LANES)]
    out_specs = [pl.BlockSpec((S, W), lambda b: (b, 0))]
    out_shape = [jax.ShapeDtypeStruct((lay.n_lat, W), BF)]
    args = [sink, q, k, v, k, v]
    if ctx_queries:
        in_specs.append(ctx(W))
        out_specs.append(ctx(W))
        out_shape.append(jax.ShapeDtypeStruct((lay.n_ctx, W), BF))
        args.append(q)
    return pl.pallas_call(
        functools.partial(_gqa_kernel, S=S, Lc=Lc, ctx_queries=ctx_queries),
        grid=(B,),
        in_specs=in_specs,
        out_specs=out_specs,
        out_shape=out_shape,
        scratch_shapes=[pltpu.VMEM((4, S, LANES), BF), pltpu.VMEM((4, S, LANES), BF),
                        pltpu.VMEM((4, Lc, LANES), BF), pltpu.VMEM((4, Lc, LANES), BF)],
        compiler_params=_params(1),
        name="gqa_window",
    )(*args)


def _rms(x, g):
    return x * lax.rsqrt(jnp.mean(x * x, axis=-1, keepdims=True) + LN_EPS) * g


def _mla_prep_kernel(cq_ref, ckv_ref, ckr_ref, gq_ref, gkv_ref, wq_ref, wk_ref, wv_ref, rc_ref, rs_ref, cc_ref, cs_ref,
                     q_ref, k_ref, v_ref):
    yq = _rms(cq_ref[...].astype(F32), gq_ref[...]).astype(BF)
    _rope_store(q_ref, _dot(yq, wq_ref[...]), (rc_ref, rs_ref, cc_ref, cs_ref), MLA_DR // 4,
                (MLA_DN + MLA_DR) ** -0.5 * LOG2E)
    ykv = _rms(ckv_ref[...].astype(F32), gkv_ref[...]).astype(BF)
    kr = ckr_ref[...].astype(F32)
    kr_all = pltpu.roll(jnp.concatenate([kr] * MLA_H, axis=1), MLA_DN, axis=1)
    k_ref[...] = (_dot(ykv, wk_ref[...]) + kr_all).astype(BF)
    v_ref[...] = _dot(ykv, wv_ref[...]).astype(BF)


def _mla_prep(cq, ckv, ckr, gq, gkv, wq, wk, wv, tab, lay):
    tm = lay.tm
    row = lambda i: (i, 0)
    return pl.pallas_call(
        _mla_prep_kernel,
        grid=(lay.tiles,),
        in_specs=[pl.BlockSpec((tm, MLA_RQ), row), pl.BlockSpec((tm, MLA_RKV), row), pl.BlockSpec((tm, LANES), row),
                  _const((1, MLA_RQ)), _const((1, MLA_RKV)), _const(wq.shape), _const(wk.shape), _const(wv.shape)]
                 + _rope_specs(tab, lay),
        out_specs=[pl.BlockSpec((tm, MLA_H * LANES), row), pl.BlockSpec((tm, MLA_H * LANES), row),
                   pl.BlockSpec((tm, MLA_H * MLA_DV), row)],
        out_shape=[jax.ShapeDtypeStruct((lay.n_tok, MLA_H * LANES), BF), jax.ShapeDtypeStruct((lay.n_tok, MLA_H * LANES), BF),
                   jax.ShapeDtypeStruct((lay.n_tok, MLA_H * MLA_DV), BF)],
        compiler_params=_params(1),
        name="mla_prep",
    )(cq, ckv, ckr, gq, gkv, wq, wk, wv, *tab)


def _mla_heads(q_ref, kv_refs, o_ref):
    lane = lax.broadcasted_iota(jnp.int32, (1, LANES), 1)
    low = lane < MLA_DV
    for pair in range(MLA_H // 2):
        vsl = slice(pair * LANES, (pair + 1) * LANES)
        acc = None
        for half in range(2):
            h = 2 * pair + half
            hsl = slice(h * LANES, (h + 1) * LANES)
            qh = q_ref[:, hsl]
            scores = [_dot_nt(qh, k_ref[:, hsl]) for k_ref, _ in kv_refs]
            m = functools.reduce(jnp.maximum, [jnp.max(s, axis=-1, keepdims=True) for s in scores])
            keep = low if half == 0 else jnp.logical_not(low)
            ones_lane = LANES - 1 if half == 0 else 0
            t = None
            for s, (_, v_ref) in zip(scores, kv_refs):
                vh = jnp.where(keep, v_ref[:, vsl], jnp.zeros((), BF))
                vh = jnp.where(lane == ones_lane, jnp.ones((), BF), vh)
                u = _dot(jnp.exp2(s - m).astype(BF), vh)
                t = u if t is None else t + u
            o = jnp.where(keep, t * (1.0 / t[:, ones_lane:ones_lane + 1]), 0.0)
            acc = o if acc is None else acc + o
        o_ref[:, vsl] = acc.astype(BF)


def _mla_lat_kernel(q_ref, kc_ref, kl_ref, vc_ref, vl_ref, o_ref):
    _mla_heads(q_ref, ((kc_ref, vc_ref), (kl_ref, vl_ref)), o_ref)


def _mla_ctx_kernel(q_ref, kc_ref, vc_ref, o_ref):
    _mla_heads(q_ref, ((kc_ref, vc_ref),), o_ref)


def _mla_latent(q, k, v, lay):
    B, S, Lc = lay.B, lay.S, lay.Lc
    tq = min(512, S)
    nq = S // tq
    lat_q0 = lay.n_ctx // tq
    kw, vw = MLA_H * LANES, MLA_H * MLA_DV
    return pl.pallas_call(
        _mla_lat_kernel,
        grid=(B, nq),
        in_specs=[pl.BlockSpec((tq, kw), lambda b, i: (lat_q0 + b * nq + i, 0)),
                  pl.BlockSpec((Lc, kw), lambda b, i: (b, 0)),
                  pl.BlockSpec((S, kw), lambda b, i: (lay.lat_seq_block + b, 0)),
                  pl.BlockSpec((Lc, vw), lambda b, i: (b, 0)),
                  pl.BlockSpec((S, vw), lambda b, i: (lay.lat_seq_block + b, 0))],
        out_specs=pl.BlockSpec((tq, vw), lambda b, i: (b * nq + i, 0)),
        out_shape=jax.ShapeDtypeStruct((lay.n_lat, vw), BF),
        compiler_params=_params(2),
        name="mla_latent",
    )(q, k, k, v, v)


def _mla_context(q, k, v, lay):
    B, Lc = lay.B, lay.Lc
    kw, vw = MLA_H * LANES, MLA_H * MLA_DV
    blk = lambda w: pl.BlockSpec((Lc, w), lambda b: (b, 0))
    return pl.pallas_call(
        _mla_ctx_kernel,
        grid=(B,),
        in_specs=[blk(kw), blk(kw), blk(vw)],
        out_specs=blk(vw),
        out_shape=jax.ShapeDtypeStruct((lay.n_ctx, vw), BF),
        compiler_params=_params(1),
        name="mla_context",
    )(q, k, v)


def _ret_chunk(Lc, S):
    return 256 if Lc % 256 == 0 and S % 256 == 0 else RET_CHUNK


def _ret_kernel(dec_ref, qc_ref, ql_ref, kc_ref, kl_ref, vc_ref, vl_ref, gc_ref, gl_ref, gn_ref,
                oc_ref, ol_ref, kv_s, r_s, *, Lc, S, ch):
    nc, nl = Lc // ch, S // ch
    n = nc + nl
    lg_f = jnp.log(_sigmoid(dec_ref[0, 0:1, :]))
    lg_b = jnp.log(_sigmoid(dec_ref[0, 1:2, :]))
    pos_i = lax.broadcasted_iota(jnp.int32, (ch, ch), 0).astype(F32)
    pos_j = lax.broadcasted_iota(jnp.int32, (ch, ch), 1).astype(F32)
    diff = pos_i - pos_j
    fwd = diff >= 0.0
    mask = jnp.where(fwd, jnp.exp(jnp.where(fwd, diff, 0.0) * lg_f), jnp.exp(jnp.where(fwd, 0.0, -diff) * lg_b))
    pos = pos_i[:, 0:1]
    lf, lb = lg_f[:, :LANES], lg_b[:, :LANES]
    zeta = (jnp.exp((ch - 1.0 - pos) * lf), jnp.exp(pos * lb))
    xi = (jnp.exp((pos + 1.0) * lf), jnp.exp((ch - pos) * lb))
    decay = (jnp.exp(ch * lf), jnp.exp(ch * lb))

    def rows(c_ref, l_ref, c):
        if c < nc:
            return c_ref[c * ch:(c + 1) * ch, :]
        return l_ref[(c - nc) * ch:(c - nc + 1) * ch, :]

    for c in range(n):
        k = rows(kc_ref, kl_ref, c).astype(F32)
        kz = jnp.concatenate([(k * zeta[0]).astype(BF), (k * zeta[1]).astype(BF)], axis=1)
        kv_s[c] = _dot_tn(kz, rows(vc_ref, vl_ref, c))

    orders = (list(range(n)), list(range(nc - 1, -1, -1)) + list(range(n - 1, nc - 1, -1)))
    for d, order in enumerate(orders):
        R = jnp.zeros((RET_DK, RET_DV), F32)
        for c in order:
            r_s[c, d * RET_DK:(d + 1) * RET_DK, :] = R.astype(BF)
            R = decay[d] * R + kv_s[c, d * RET_DK:(d + 1) * RET_DK, :]

    for c in range(n):
        q = rows(qc_ref, ql_ref, c)
        s = _dot_nt(q, rows(kc_ref, kl_ref, c)) * mask
        qf = q.astype(F32)
        qx = jnp.concatenate([(qf * xi[0]).astype(BF), (qf * xi[1]).astype(BF)], axis=1)
        y = _dot(s.astype(BF), rows(vc_ref, vl_ref, c)) + _dot(qx, r_s[c])
        mu = jnp.mean(y, axis=-1, keepdims=True)
        yc = y - mu
        var = jnp.mean(yc * yc, axis=-1, keepdims=True)
        yn = yc * lax.rsqrt(var + LN_EPS) * gn_ref[...]
        g = rows(gc_ref, gl_ref, c).astype(F32)
        out = (g * _sigmoid(g) * yn).astype(BF)
        if c < nc:
            oc_ref[c * ch:(c + 1) * ch, :] = out
        else:
            ol_ref[(c - nc) * ch:(c - nc + 1) * ch, :] = out


def _retention(dq, dk, dv, dg, decay_lanes, gn, lay):
    B, S, Lc = lay.B, lay.S, lay.Lc
    ch = _ret_chunk(Lc, S)
    n = (Lc + S) // ch
    ctx = pl.BlockSpec((Lc, LANES), lambda b, h: (b, h))
    lat = pl.BlockSpec((S, LANES), lambda b, h: (lay.lat_seq_block + b, h))
    return pl.pallas_call(
        functools.partial(_ret_kernel, Lc=Lc, S=S, ch=ch),
        grid=(B, RET_H),
        in_specs=[pl.BlockSpec((1, 2, ch), lambda b, h: (h, 0, 0)),
                  ctx, lat, ctx, lat, ctx, lat, ctx, lat, pl.BlockSpec((1, LANES), lambda b, h: (0, h))],
        out_specs=[pl.BlockSpec((Lc, LANES), lambda b, h: (b, h)), pl.BlockSpec((S, LANES), lambda b, h: (b, h))],
        out_shape=[jax.ShapeDtypeStruct((lay.n_ctx, RET_H * RET_DV), BF), jax.ShapeDtypeStruct((lay.n_lat, RET_H * RET_DV), BF)],
        scratch_shapes=[pltpu.VMEM((n, 2 * RET_DK, RET_DV), F32), pltpu.VMEM((n, 2 * RET_DK, RET_DV), BF)],
        compiler_params=_params(2),
        name="retention",
    )(decay_lanes, dq, dq, dk, dk, dv, dv, dg, dg, gn)


def _merge_kernel(*refs, n_br, ct):
    x_ref, mod_ref = refs[:2]
    br_refs = refs[2:2 + N_BRANCH * n_br]
    wg_ref, wb_ref, wo_ref, g_ref, beta_ref, o_ref = refs[2 + N_BRANCH * n_br:]
    x = x_ref[...]
    h = (x * (1.0 + mod_ref[0, 4:5, :]) + mod_ref[0, 3:4, :]).astype(BF)
    acc = None
    for k in range(N_BRANCH):
        br = _rows(br_refs[k * n_br:(k + 1) * n_br], ct)
        gate = 0.5 * jnp.tanh(_dot(h, wg_ref[:, k * D_MODEL:(k + 1) * D_MODEL])) + 0.5
        t = gate * _dot(br, wb_ref[k])
        acc = t if acc is None else acc + t
    y = _dot(acc.astype(BF), wo_ref[...])
    z = DN_ALPHA * x + mod_ref[0, 5:6, :] * y
    o_ref[...] = _layer_norm(z, g_ref[...], beta_ref[...])


def _merge(x, mod, branches, w_gate, wb_all, wo_all, l, g, beta, lay, g_off):
    tm = lay.tm
    n = lay.tiles - g_off
    br_specs, br_args = [], []
    for br in branches:
        specs, args = _row_specs(br, BRANCH_W, lay, g_off)
        br_specs += specs
        br_args += args
    return pl.pallas_call(
        functools.partial(_merge_kernel, n_br=len(br_args) // N_BRANCH, ct=lay.ctx_tiles),
        grid=(n,),
        in_specs=[pl.BlockSpec((tm, D_MODEL), lambda i: (i + g_off, 0)),
                  pl.BlockSpec((1, 9, D_MODEL), lambda i: (lay.mod_index(i + g_off), 0, 0))] + br_specs + [
                  _resident(w_gate.shape),
                  pl.BlockSpec((None, N_BRANCH, BRANCH_W, D_MODEL), lambda i: (l, 0, 0, 0), pipeline_mode=pl.Buffered(1)),
                  pl.BlockSpec((None, D_MODEL, D_MODEL), lambda i: (l, 0, 0), pipeline_mode=pl.Buffered(1)),
                  _const((1, D_MODEL)), _const((1, D_MODEL))],
        out_specs=pl.BlockSpec((tm, D_MODEL), lambda i: (i, 0)),
        out_shape=jax.ShapeDtypeStruct((n * tm, D_MODEL), F32),
        compiler_params=_params(1),
        name="gated_merge",
    )(x, mod, *br_args, w_gate, wb_all, wo_all, g, beta)


def _rope_tables(S, head_dim, lead, reps, tail, tm):
    nf = head_dim // 4
    inv = ROPE_BASE ** (-jnp.arange(nf, dtype=F32) / nf)
    n_rows = S // GRID_W
    ar = jnp.arange(n_rows, dtype=F32)[:, None] * inv[None, :]
    ac = jnp.arange(GRID_W, dtype=F32)[:, None] * inv[None, :]

    def lanes(by_row, first, second, fill):
        n = first.shape[0]
        neutral = jnp.full((n, nf), fill, F32)
        head = jnp.concatenate([first, second, neutral, neutral] if by_row else [neutral, neutral, first, second], -1)
        pad = lambda w: jnp.full((n, w), fill, F32)
        return jnp.tile(jnp.concatenate([pad(lead), head, pad(tail)], -1), (1, reps))

    row_c = lanes(True, jnp.cos(ar), jnp.cos(ar), 1.0)
    row_s = lanes(True, -jnp.sin(ar), jnp.sin(ar), 0.0)
    col_c = lanes(False, jnp.cos(ac), jnp.cos(ac), 1.0)
    col_s = lanes(False, -jnp.sin(ac), jnp.sin(ac), 0.0)
    W = row_c.shape[1]
    per_tile = tm // GRID_W

    def by_tile(tab, fill):
        t = tab.reshape(S // tm, per_tile, W)
        t = jnp.concatenate([t, jnp.full((S // tm, SUBLANES - per_tile, W), fill, F32)], 1).reshape(-1, W)
        return jnp.concatenate([t, jnp.full((SUBLANES, W), fill, F32)], 0)

    ident = lambda fill: jnp.full((GRID_W, W), fill, F32)
    return (by_tile(row_c, 1.0), by_tile(row_s, 0.0),
            jnp.stack([col_c, ident(1.0)]), jnp.stack([col_s, ident(0.0)]))


def _block_diag(w):
    n, bw, _ = w.shape
    eye = jnp.eye(n, dtype=w.dtype)
    return (eye[:, None, :, None] * w[:, :, None, :]).reshape(n * bw, n * bw)


def _layer_weights(l, w_in, lru_w_a, lru_b_a, lru_w_x, lru_b_x, mla_w_uq, mla_w_ukv):
    cuts = [0]
    for s in IN_SIZES:
        cuts.append(cuts[-1] + s)
    cols = [w_in[l][:, cuts[i]:cuts[i + 1]] for i in range(len(IN_SIZES))]
    cols[7] = jnp.pad(cols[7], ((0, 0), (0, LANES - MLA_DR)))
    w_proj = jnp.concatenate(cols[:12], axis=1).astype(BF)
    w_gate = (0.5 * cols[12]).astype(BF)
    wg = (0.5 * jnp.concatenate([_block_diag(lru_w_a[l, 0]), _block_diag(lru_w_x[l, 0]),
                                 _block_diag(lru_w_a[l, 1]), _block_diag(lru_w_x[l, 1])], axis=1)).astype(BF)
    bg = 0.5 * jnp.concatenate([lru_b_a[l, 0], lru_b_x[l, 0], lru_b_a[l, 1], lru_b_x[l, 1]])[None, :]
    uq = mla_w_uq[l].reshape(MLA_RQ, MLA_H, MLA_DN + MLA_DR)
    wq = jnp.pad(uq, ((0, 0), (0, 0), (0, LANES - MLA_DN - MLA_DR))).reshape(MLA_RQ, MLA_H * LANES).astype(BF)
    ukv = mla_w_ukv[l].reshape(MLA_RKV, MLA_H, MLA_DN + MLA_DV)
    wk = jnp.pad(ukv[:, :, :MLA_DN], ((0, 0), (0, 0), (0, LANES - MLA_DN))).reshape(MLA_RKV, MLA_H * LANES).astype(BF)
    wv = ukv[:, :, MLA_DN:].reshape(MLA_RKV, MLA_H * MLA_DV).astype(BF)
    return w_proj, w_gate, wg, bg, wq, wk, wv


def kernel(x, c, ctx, c_ctx, w_mod, b_mod, ln_g, ln_b, ffn_w_in, ffn_w_out, w_in, lru_conv_w, lru_conv_b, lru_w_a, lru_b_a, lru_w_x, lru_b_x, lru_lambda, gqa_sink, mla_q_norm, mla_kv_norm, mla_w_uq, mla_w_ukv, ret_decay, ret_gn_g, w_branch, w_out):
    B, S, D = x.shape
    Lc = ctx.shape[1]
    lay = _Layout(B, S, Lc)
    tm = lay.tm

    n_cond = -(-(B + 1) // SUBLANES) * SUBLANES
    cond = jnp.concatenate([c, c_ctx[None, :], jnp.zeros((n_cond - B - 1, D), F32)], axis=0)
    mod_all = _modulation(cond, w_mod, b_mod)[:, :B + 1].reshape(DEPTH, B + 1, 9, D)

    tabs = [_rope_tables(S, head_dim, 0, reps, tail, tm) for head_dim, reps, tail in
            ((GQA_DH, GQA_HQ, 0), (GQA_DH, GQA_HKV, 0), (MLA_DR, 1, LANES - MLA_DR), (RET_DK, RET_H, 0))]
    mla_tab = _rope_tables(S, MLA_DR, MLA_DN, MLA_H, LANES - MLA_DN - MLA_DR, tm)

    tok = (ctx.reshape(B * Lc, D), x.reshape(B * S, D))
    row = lambda v: v[None, :]
    ct = lay.ctx_tiles
    w1_all, w2_all = ffn_w_in.astype(BF), ffn_w_out.astype(BF)
    wb_all, wo_all = w_branch.astype(BF), w_out.astype(BF)

    for l in range(DEPTH):
        last = l == DEPTH - 1
        mod = mod_all[l]
        w_proj, w_gate, wg, bg, wq, wk, wv = _layer_weights(
            l, w_in, lru_w_a, lru_b_a, lru_w_x, lru_b_x, mla_w_uq, mla_w_ukv)
        dec = jnp.broadcast_to(ret_decay[l].T[:, :, None], (RET_H, 2, _ret_chunk(Lc, S)))

        tok = _half_ffn(tok, mod, w1_all, w2_all, l, 0, row(ln_g[l, 0]), row(ln_b[l, 0]), lay, 0, 0)

        ax, ay, bq, bk, bv, cq, ckv, ckr, dq, dk, dv, dg = _mixer_proj(tok, mod, w_proj, tabs, lay)
        a_c, a_l = _lru(ax, ay, lru_conv_w[l], row(lru_conv_b[l]), wg, bg, lru_lambda[l], lay)
        b_out = _gqa(bq, bk, bv, gqa_sink[l], lay, ctx_queries=not last)
        mq, mk, mv = _mla_prep(cq, ckv, ckr, row(mla_q_norm[l]), row(mla_kv_norm[l]), wq, wk, wv, mla_tab, lay)
        c_l = _mla_latent(mq, mk, mv, lay)
        d_c, d_l = _retention(dq, dk, dv, dg, dec, row(ret_gn_g[l]), lay)

        if last:
            branches = (a_l, b_out[0], c_l, d_l)
            g_off = ct
        else:
            branches = ((a_c, a_l), (b_out[1], b_out[0]), (_mla_context(mq, mk, mv, lay), c_l), (d_c, d_l))
            g_off = 0
        tok = _merge(tok, mod, branches, w_gate, wb_all, wo_all, l, row(ln_g[l, 1]), row(ln_b[l, 1]), lay, g_off)
        tok = _half_ffn(tok, mod, w1_all, w2_all, l, 1, row(ln_g[l, 2]), row(ln_b[l, 2]), lay, 6, g_off)

    return tok.reshape(B, S, D)
```

```python
import functools

import jax
import jax.numpy as jnp
from jax import lax
from jax.experimental import pallas as pl
from jax.experimental.pallas import tpu as pltpu

D_MODEL = 1024
DEPTH = 2
GRID_W = 64
N_BRANCH = 4
BRANCH_W = 512
LRU_W = 512
LRU_BLOCKS = 8
LRU_BW = LRU_W // LRU_BLOCKS
LRU_C = 8.0
GQA_HQ = 8
GQA_HKV = 2
GQA_DH = 64
WINDOW = 128
ATT_BLOCK = 128
MLA_H = 8
MLA_RQ = 384
MLA_RKV = 256
MLA_DN = 64
MLA_DR = 32
MLA_DV = 64
RET_H = 4
RET_DK = 128
RET_DV = 128
RET_CHUNK = 128
FF = 2816
ROPE_BASE = 10000.0
LN_EPS = 1e-5
MASK_VALUE = -1e30
DN_ALPHA = (2 * DEPTH) ** 0.25
IN_SIZES = (LRU_W, LRU_W, GQA_HQ * GQA_DH, GQA_HKV * GQA_DH, GQA_HKV * GQA_DH, MLA_RQ, MLA_RKV, MLA_DR,
            RET_H * RET_DK, RET_H * RET_DK, RET_H * RET_DV, RET_H * RET_DV, N_BRANCH * D_MODEL)

LANES = 128
SUBLANES = 8
VMEM_LIMIT = 56 * 1024 * 1024
MXU_DIM = 256
FFN_CHUNKS = (6 * MXU_DIM, 5 * MXU_DIM)
LOG2E = 1.4426950408889634
BF = jnp.bfloat16
F32 = jnp.float32


def _params(n_axes):
    return pltpu.CompilerParams(dimension_semantics=("arbitrary",) * n_axes, vmem_limit_bytes=VMEM_LIMIT)


def _resident(shape):
    nd = len(shape)
    return pl.BlockSpec(shape, lambda *_: (0,) * nd, pipeline_mode=pl.Buffered(1))


def _const(shape):
    nd = len(shape)
    return pl.BlockSpec(shape, lambda *_: (0,) * nd)


def _layer_norm(z, g, b):
    mu = jnp.mean(z, axis=-1, keepdims=True)
    zc = z - mu
    var = jnp.mean(zc * zc, axis=-1, keepdims=True)
    return zc * lax.rsqrt(var + LN_EPS) * g + b


def _sigmoid(x):
    return 1.0 / (1.0 + jnp.exp(-x))


def _dot(a, b):
    return jnp.dot(a, b, preferred_element_type=F32)


def _dot_nt(a, b):
    return lax.dot_general(a, b, (((1,), (1,)), ((), ())), preferred_element_type=F32)


def _dot_tn(a, b):
    return lax.dot_general(a, b, (((0,), (0,)), ((), ())), preferred_element_type=F32)


def _rope(y, c, s, nf):
    w = y.shape[-1]
    lane = lax.broadcasted_iota(jnp.int32, y.shape, 1)
    partner = jnp.where((lane % (2 * nf)) < nf, pltpu.roll(y, w - nf, axis=1), pltpu.roll(y, nf, axis=1))
    return y * c + partner * s


class _Layout:
    def __init__(self, B, S, Lc):
        self.B, self.S, self.Lc = B, S, Lc
        self.n_ctx, self.n_lat = B * Lc, B * S
        self.n_tok = self.n_ctx + self.n_lat
        tm = 512
        while S % tm or self.n_ctx % tm:
            tm //= 2
        self.tm = tm
        self.ctx_tiles = self.n_ctx // tm
        self.tiles_per_seq = S // tm
        self.tiles = self.ctx_tiles + B * self.tiles_per_seq
        assert self.n_ctx % S == 0, "latent rows must start on a whole-sequence block"
        self.lat_seq_block = self.n_ctx // S
        self.lat_att_block = self.n_ctx // ATT_BLOCK

    def mod_index(self, g):
        return jnp.where(g < self.ctx_tiles, self.B, (g - self.ctx_tiles) // self.tiles_per_seq)

    def rope_index(self, g):
        return jnp.where(g < self.ctx_tiles, self.tiles_per_seq, (g - self.ctx_tiles) % self.tiles_per_seq)


def _mod_kernel(s_ref, w_ref, b_ref, o_ref):
    s = s_ref[...]
    s = s * _sigmoid(s)
    o_ref[0] = _dot(s.astype(BF), w_ref[0].astype(BF)) + b_ref[0]


def _modulation(cond, w_mod, b_mod):
    R = cond.shape[0]
    tn = 1024
    n9 = 9 * D_MODEL
    return pl.pallas_call(
        _mod_kernel,
        grid=(DEPTH, n9 // tn),
        in_specs=[pl.BlockSpec((R, D_MODEL), lambda l, j: (0, 0)),
                  pl.BlockSpec((1, D_MODEL, tn), lambda l, j: (l, 0, j)),
                  pl.BlockSpec((1, 1, tn), lambda l, j: (l, 0, j))],
        out_specs=pl.BlockSpec((1, R, tn), lambda l, j: (l, 0, j)),
        out_shape=jax.ShapeDtypeStruct((DEPTH, R, n9), F32),
        compiler_params=_params(2),
        name="modulation",
    )(cond, w_mod, b_mod.reshape(DEPTH, 1, n9))


def _row_specs(src, width, lay, g_off):
    tm, ct = lay.tm, lay.ctx_tiles
    if isinstance(src, tuple):
        assert g_off == 0
        return [pl.BlockSpec((tm, width), lambda i: (jnp.minimum(i, ct - 1), 0)),
                pl.BlockSpec((tm, width), lambda i: (jnp.maximum(i - ct, 0), 0))], list(src)
    assert src.shape[0] == lay.n_tok or (src.shape[0] == lay.n_lat and g_off == ct)
    off = g_off if src.shape[0] == lay.n_tok else 0
    return [pl.BlockSpec((tm, width), lambda i: (i + off, 0))], [src]


def _rows(refs, ct):
    if len(refs) == 2:
        return jnp.where(pl.program_id(0) < ct, refs[0][...], refs[1][...])
    return refs[0][...]


def _ffn_kernel(*refs, k0, n_x, ct):
    mod_ref, w1_ref, w2_ref, g_ref, b_ref, o_ref = refs[n_x:]
    x = _rows(refs[:n_x], ct)
    shift = mod_ref[0, k0:k0 + 1, :]
    scale = mod_ref[0, k0 + 1:k0 + 2, :]
    gate = mod_ref[0, k0 + 2:k0 + 3, :]
    xm = (x * (1.0 + scale) + shift).astype(BF)
    acc = jnp.zeros(x.shape, F32)
    assert sum(FFN_CHUNKS) == FF
    lo = 0
    for width in FFN_CHUNKS:
        a = _dot(xm, w1_ref[:, lo:lo + width])
        b = _dot(xm, w1_ref[:, FF + lo:FF + lo + width])
        h = (a * _sigmoid(a) * b).astype(BF)
        acc = acc + _dot(h, w2_ref[lo:lo + width, :])
        lo += width
    z = DN_ALPHA * x + (0.5 * gate) * acc
    o_ref[...] = _layer_norm(z, g_ref[...], b_ref[...])


def _half_ffn(x, mod, w1_all, w2_all, l, j, g, b, lay, k0, g_off):
    tm = lay.tm
    n = lay.tiles - g_off
    x_specs, x_args = _row_specs(x, D_MODEL, lay, g_off)
    pick = lambda *_: (l, j, 0, 0)
    return pl.pallas_call(
        functools.partial(_ffn_kernel, k0=k0, n_x=len(x_args), ct=lay.ctx_tiles),
        grid=(n,),
        in_specs=x_specs + [
            pl.BlockSpec((1, 9, D_MODEL), lambda i: (lay.mod_index(i + g_off), 0, 0)),
            pl.BlockSpec((None, None, D_MODEL, 2 * FF), pick, pipeline_mode=pl.Buffered(1)),
            pl.BlockSpec((None, None, FF, D_MODEL), pick, pipeline_mode=pl.Buffered(1)),
            _const((1, D_MODEL)), _const((1, D_MODEL))],
        out_specs=pl.BlockSpec((tm, D_MODEL), lambda i: (i, 0)),
        out_shape=jax.ShapeDtypeStruct((n * tm, D_MODEL), F32),
        compiler_params=_params(1),
        name="half_ffn",
    )(*x_args, mod, w1_all, w2_all, g, b)


_W_LRU, _W_GQ, _W_GKV, _W_RET = LRU_W, GQA_HQ * GQA_DH, GQA_HKV * GQA_DH, RET_H * RET_DK
_OFF_AX, _OFF_AY, _OFF_BQ, _OFF_BKV = 0, 512, 1024, 1536
_OFF_MLA, _W_MLA = 1792, MLA_RQ + MLA_RKV + LANES
_OFF_DQ, _OFF_DK, _OFF_DV, _OFF_DG = 2560, 3072, 3584, 4096
_PROJ_OUT_WIDTHS = (_W_LRU, _W_LRU, _W_GQ, _W_GKV, _W_GKV, MLA_H * LANES, MLA_H * LANES, MLA_H * MLA_DV,
                    _W_RET, _W_RET, _W_RET, _W_RET)
_N_ROPE_TABLES = 5


def _rope_tile(y, tab, nf):
    rc_ref, rs_ref, cc_ref, cs_ref = tab
    parts = []
    for g in range(y.shape[0] // GRID_W):
        c = rc_ref[g:g + 1, :] * cc_ref[...]
        s = rs_ref[g:g + 1, :] + cs_ref[...]
        parts.append(_rope(y[g * GRID_W:(g + 1) * GRID_W, :], c, s, nf))
    return jnp.concatenate(parts, axis=0)


def _rms(x, g):
    return x * lax.rsqrt(jnp.mean(x * x, axis=-1, keepdims=True) + LN_EPS) * g


def _proj_kernel(x_ref, mod_ref, w_ref, wq_ref, wk_ref, wv_ref, gq_ref, gkv_ref, *refs):
    tabs = [refs[4 * i:4 * i + 4] for i in range(_N_ROPE_TABLES)]
    t_gq, t_gk, t_kr, t_ret, t_mq = tabs
    ax_o, ay_o, bq_o, bk_o, bv_o, mq_o, mk_o, mv_o, dq_o, dk_o, dv_o, dg_o = refs[4 * _N_ROPE_TABLES:]
    x = x_ref[...]
    h = (x * (1.0 + mod_ref[0, 4:5, :]) + mod_ref[0, 3:4, :]).astype(BF)
    proj = lambda lo, width: _dot(h, w_ref[:, lo:lo + width])

    ax_o[...] = proj(_OFF_AX, _W_LRU).astype(BF)
    ay_o[...] = proj(_OFF_AY, _W_LRU).astype(BF)
    bq_o[...] = (_rope_tile(proj(_OFF_BQ, _W_GQ), t_gq, GQA_DH // 4) * (GQA_DH ** -0.5 * LOG2E)).astype(BF)
    kv = proj(_OFF_BKV, 2 * _W_GKV)
    bk_o[...] = _rope_tile(kv[:, :_W_GKV], t_gk, GQA_DH // 4).astype(BF)
    bv_o[...] = kv[:, _W_GKV:].astype(BF)

    c = proj(_OFF_MLA, _W_MLA)
    yq = _rms(c[:, :MLA_RQ], gq_ref[...]).astype(BF)
    mq_o[...] = (_rope_tile(_dot(yq, wq_ref[...]), t_mq, MLA_DR // 4) * ((MLA_DN + MLA_DR) ** -0.5 * LOG2E)).astype(BF)
    ykv = _rms(c[:, MLA_RQ:MLA_RQ + MLA_RKV], gkv_ref[...]).astype(BF)
    kr = _rope_tile(c[:, MLA_RQ + MLA_RKV:], t_kr, MLA_DR // 4)
    kr_all = pltpu.roll(jnp.concatenate([kr] * MLA_H, axis=1), MLA_DN, axis=1)
    mk_o[...] = (_dot(ykv, wk_ref[...]) + kr_all).astype(BF)
    mv_o[...] = _dot(ykv, wv_ref[...]).astype(BF)

    dq_o[...] = _rope_tile(proj(_OFF_DQ, _W_RET), t_ret, RET_DK // 4).astype(BF)
    dk_o[...] = (_rope_tile(proj(_OFF_DK, _W_RET), t_ret, RET_DK // 4) * RET_DK ** -0.5).astype(BF)
    dv_o[...] = proj(_OFF_DV, _W_RET).astype(BF)
    dg_o[...] = proj(_OFF_DG, _W_RET).astype(BF)


def _rope_specs(tab, lay):
    w = tab[0].shape[1]
    row = pl.BlockSpec((SUBLANES, w), lambda i: (lay.rope_index(i), 0))
    col = pl.BlockSpec((None, GRID_W, w), lambda i: (jnp.where(i < lay.ctx_tiles, 1, 0), 0, 0))
    return [row, row, col, col]


def _mixer_proj(x, mod, w_all, l, wq, wk, wv, gq, gkv, tabs, lay):
    tm = lay.tm
    row = lambda i: (i, 0)
    assert len(tabs) == _N_ROPE_TABLES
    tab_specs, tab_args = [], []
    for tab in tabs:
        tab_specs += _rope_specs(tab, lay)
        tab_args += list(tab)
    return pl.pallas_call(
        _proj_kernel,
        grid=(lay.tiles,),
        in_specs=[pl.BlockSpec((tm, D_MODEL), row),
                  pl.BlockSpec((1, 9, D_MODEL), lambda i: (lay.mod_index(i), 0, 0)),
                  pl.BlockSpec((None,) + w_all.shape[1:], lambda i: (l, 0, 0), pipeline_mode=pl.Buffered(1)),
                  _const(wq.shape), _const(wk.shape), _const(wv.shape), _const(gq.shape), _const(gkv.shape)] + tab_specs,
        out_specs=[pl.BlockSpec((tm, w), row) for w in _PROJ_OUT_WIDTHS],
        out_shape=[jax.ShapeDtypeStruct((lay.n_tok, w), BF) for w in _PROJ_OUT_WIDTHS],
        compiler_params=_params(1),
        name="mixer_proj",
    )(x, mod, w_all, wq, wk, wv, gq, gkv, *tab_args)


def _conv4(x, w_ref, b_ref):
    T = x.shape[0]
    row = lax.broadcasted_iota(jnp.int32, x.shape, 0)
    y = b_ref[...] + w_ref[2:3, :] * x
    y = y + w_ref[0:1, :] * jnp.where(row >= 2, pltpu.roll(x, 2, axis=0), 0.0)
    y = y + w_ref[1:2, :] * jnp.where(row >= 1, pltpu.roll(x, 1, axis=0), 0.0)
    y = y + w_ref[3:4, :] * jnp.where(row < T - 1, pltpu.roll(x, T - 1, axis=0), 0.0)
    return y


def _scan8(a, b, reverse):
    row = lax.broadcasted_iota(jnp.int32, a.shape, 0)
    for s in (1, 2, 4):
        if reverse:
            keep = row < SUBLANES - s
            a_sh = pltpu.roll(a, SUBLANES - s, axis=0)
            b_sh = pltpu.roll(b, SUBLANES - s, axis=0)
        else:
            keep = row >= s
            a_sh = pltpu.roll(a, s, axis=0)
            b_sh = pltpu.roll(b, s, axis=0)
        b = jnp.where(keep, b + a * b_sh, b)
        a = jnp.where(keep, a * a_sh, a)
    return a, b


def _lru_segment(T):
    for seg in (36, 44, 28, 20, 12, 52, 60, 4, 8, 16, 32):
        if T % (SUBLANES * seg) == 0:
            return seg
    raise ValueError(f"no scan segment length for {T} rows")


def _lru_kernel(axc_ref, axl_ref, ayc_ref, ayl_ref, cw_ref, cb_ref, wg_ref, bg_ref, lam_ref,
                oc_ref, ol_ref, xa_s, a_s, b_s, h0_s, h1_s, ac_s, bc_s, *, Lc, S, rows, seg):
    W = LRU_W
    T = Lc + S
    NS = W // LANES
    G = SUBLANES * seg
    xa_s[0:Lc, :] = _conv4(axc_ref[...].astype(F32), cw_ref, cb_ref)
    xa_s[Lc:T, :] = _conv4(axl_ref[...].astype(F32), cw_ref, cb_ref)
    row8 = lax.broadcasted_iota(jnp.int32, (SUBLANES, LANES), 0)
    zero8 = jnp.zeros((SUBLANES, LANES), F32)

    for d in range(2):
        reverse = d == 1
        h_s = h1_s if reverse else h0_s
        neg = -lam_ref[d:d + 1, :]
        softplus = jnp.maximum(neg, 0.0) + jnp.log(1.0 + jnp.exp(-jnp.abs(neg)))
        c = (-0.5 * LRU_C * 1.4426950408889634) * softplus

        def gates(i, carry):
            r0 = pl.multiple_of(i * rows, rows)
            dst = pl.multiple_of(jnp.where(r0 < Lc, r0 + S, r0 - Lc), rows) if reverse else r0
            xa = xa_s[pl.ds(r0, rows), :]
            xb = xa.astype(BF)
            lo = 2 * d * W
            t_r = jnp.tanh(_dot(xb, wg_ref[:, lo:lo + W]) + bg_ref[:, lo:lo + W])
            t_i = jnp.tanh(_dot(xb, wg_ref[:, lo + W:lo + 2 * W]) + bg_ref[:, lo + W:lo + 2 * W])
            a = jnp.exp2(c * t_r + c)
            y = 1.0 - a * a
            root = jnp.where(y > 0.0, y * lax.rsqrt(y), 0.0)
            b = (root * (0.5 * xa)) * (t_i + 1.0)
            for k in range(NS):
                a_s[k, pl.ds(dst, rows), :] = a[:, k * LANES:(k + 1) * LANES]
                b_s[k, pl.ds(dst, rows), :] = b[:, k * LANES:(k + 1) * LANES]
            return carry

        lax.fori_loop(0, T // rows, gates, 0)

        def group(j, h_in):
            base = ((T // G - 1 - j) if reverse else j) * G
            acc_a, acc_b = [None] * NS, [None] * NS
            for i in (range(seg - 1, -1, -1) if reverse else range(seg)):
                for k in range(NS):
                    a = a_s[k, pl.ds(base + i, SUBLANES, stride=seg), :]
                    b = b_s[k, pl.ds(base + i, SUBLANES, stride=seg), :]
                    if acc_a[k] is None:
                        acc_a[k], acc_b[k] = a, b
                    else:
                        acc_b[k] = a * acc_b[k] + b
                        acc_a[k] = a * acc_a[k]
                    ac_s[k, i] = acc_a[k]
                    bc_s[k, i] = acc_b[k]
            h_out, enter = [], []
            for k in range(NS):
                tot_a, tot_b = _scan8(acc_a[k], acc_b[k], reverse)
                after = tot_a * h_in[k] + tot_b
                if reverse:
                    enter.append(jnp.where(row8 == SUBLANES - 1, h_in[k], pltpu.roll(after, SUBLANES - 1, axis=0)))
                    h_out.append(jnp.broadcast_to(after[0:1, :], after.shape))
                else:
                    enter.append(jnp.where(row8 == 0, h_in[k], pltpu.roll(after, 1, axis=0)))
                    h_out.append(jnp.broadcast_to(after[SUBLANES - 1:SUBLANES, :], after.shape))
            for i in range(seg):
                for k in range(NS):
                    h_s[k, pl.ds(base + i, SUBLANES, stride=seg), :] = ac_s[k, i] * enter[k] + bc_s[k, i]
            return tuple(h_out)

        lax.fori_loop(0, T // G, group, (zero8,) * NS)

    def gelu(v):
        return 0.5 * v * (1.0 + jnp.tanh(0.7978845608028654 * (v + 0.044715 * v * v * v)))

    for k in range(NS):
        sl = slice(k * LANES, (k + 1) * LANES)
        hc = h0_s[k, 0:Lc, :] + h1_s[k, S:T, :]
        hl = h0_s[k, Lc:T, :] + h1_s[k, 0:S, :]
        oc_ref[:, sl] = (hc * gelu(ayc_ref[:, sl].astype(F32))).astype(BF)
        ol_ref[:, sl] = (hl * gelu(ayl_ref[:, sl].astype(F32))).astype(BF)


def _lru(ax, ay, conv_w, conv_b, wg, bg, lam, lay):
    B, S, Lc = lay.B, lay.S, lay.Lc
    W = LRU_W
    T = Lc + S
    rows = 256
    while Lc % rows or S % rows:
        rows //= 2
    seg = _lru_segment(T)
    ctx = pl.BlockSpec((Lc, W), lambda b: (b, 0))
    lat = pl.BlockSpec((S, W), lambda b: (b + lay.lat_seq_block, 0))
    slabs = pltpu.VMEM((W // LANES, T, LANES), F32)
    part = pltpu.VMEM((W // LANES, seg, SUBLANES, LANES), F32)
    return pl.pallas_call(
        functools.partial(_lru_kernel, Lc=Lc, S=S, rows=rows, seg=seg),
        grid=(B,),
        in_specs=[ctx, lat, ctx, lat, _const((4, W)), _const((1, W)), _const(wg.shape), _const((1, 4 * W)), _const((2, W))],
        out_specs=[pl.BlockSpec((Lc, W), lambda b: (b, 0)), pl.BlockSpec((S, W), lambda b: (b, 0))],
        out_shape=[jax.ShapeDtypeStruct((lay.n_ctx, W), BF), jax.ShapeDtypeStruct((lay.n_lat, W), BF)],
        scratch_shapes=[pltpu.VMEM((T, W), F32), slabs, slabs, slabs, slabs, part, part],
        compiler_params=_params(1),
        name="rglru",
    )(ax, ax, ay, ay, conv_w, conv_b, wg, bg, lam)


def _split_heads_pair(x, ones_lane):
    xf = x.astype(F32)
    sw = pltpu.roll(xf, GQA_DH, axis=1)
    lane = lax.broadcasted_iota(jnp.int32, xf.shape, 1)
    low = lane < GQA_DH
    lo_fill = jnp.where(lane == LANES - 1, 1.0, 0.0) if ones_lane else jnp.zeros_like(xf)
    hi_fill = jnp.where(lane == 0, 1.0, 0.0) if ones_lane else jnp.zeros_like(xf)
    head0 = (jnp.where(low, xf, lo_fill).astype(BF), jnp.where(low, hi_fill, sw).astype(BF))
    head1 = (jnp.where(low, sw, lo_fill).astype(BF), jnp.where(low, hi_fill, xf).astype(BF))
    return head0, head1


def _gqa_kernel(sink_ref, *refs, S, Lc, ctx_queries):
    if ctx_queries:
        q_ref, k_ref, v_ref, kx_ref, vx_ref, qx_ref, o_ref, ox_ref, ks_s, vs_s, kxs_s, vxs_s = refs
    else:
        q_ref, k_ref, v_ref, kx_ref, vx_ref, o_ref, ks_s, vs_s, kxs_s, vxs_s = refs
    blk = ATT_BLOCK
    nb = S // blk
    win = min(3 * blk, S)
    for src, dst, is_value in ((k_ref, ks_s, False), (v_ref, vs_s, True), (kx_ref, kxs_s, False), (vx_ref, vxs_s, True)):
        heads = _split_heads_pair(src[...], is_value)
        for j in range(GQA_HKV):
            for half in range(2):
                dst[2 * j + half] = heads[j][half]

    row = lax.broadcasted_iota(jnp.int32, (2 * blk, win), 0)
    col = lax.broadcasted_iota(jnp.int32, (2 * blk, win), 1)
    rel = (row % blk) - col
    top = row[:, 0:1] < blk
    low = lax.broadcasted_iota(jnp.int32, (1, LANES), 1) < GQA_DH

    def attend(q, kstart, valid):
        outs = []
        for j in range(GQA_HKV):
            q2 = jnp.concatenate([q[:, 2 * j * LANES:(2 * j + 1) * LANES], q[:, (2 * j + 1) * LANES:(2 * j + 2) * LANES]], axis=0)
            acc = jnp.zeros((2 * blk, LANES), F32)
            for half in range(2):
                i = 2 * j + half
                sink = jnp.where(top, sink_ref[4 * j + half] * LOG2E, sink_ref[4 * j + 2 + half] * LOG2E)
                s_c = _dot_nt(q2, kxs_s[i])
                m = jnp.maximum(jnp.max(s_c, axis=-1, keepdims=True), sink)
                if kstart is not None:
                    s_w = jnp.where(valid, _dot_nt(q2, ks_s[i, pl.ds(kstart, win), :]), MASK_VALUE)
                    m = jnp.maximum(m, jnp.max(s_w, axis=-1, keepdims=True))
                t = _dot(jnp.exp2(s_c - m).astype(BF), vxs_s[i])
                if kstart is not None:
                    t = t + _dot(jnp.exp2(s_w - m).astype(BF), vs_s[i, pl.ds(kstart, win), :])
                ones_lane = LANES - 1 if half == 0 else 0
                den = t[:, ones_lane:ones_lane + 1] + jnp.exp2(sink - m)
                acc = acc + jnp.where(low if half == 0 else jnp.logical_not(low), t * (1.0 / den), 0.0)
            outs += [acc[0:blk].astype(BF), acc[blk:2 * blk].astype(BF)]
        return jnp.concatenate(outs, axis=1)

    def block(n, carry):
        q0 = pl.multiple_of(n * blk, blk)
        kstart = pl.multiple_of(jnp.clip((n - 1) * blk, 0, S - win), blk)
        dist = rel + (q0 - kstart)
        valid = (dist <= WINDOW) & (dist >= -WINDOW)
        o_ref[pl.ds(q0, blk), :] = attend(q_ref[pl.ds(q0, blk), :], kstart, valid)
        return carry

    lax.fori_loop(0, nb, block, 0, unroll=2)
    if ctx_queries:
        for n in range(Lc // blk):
            ox_ref[n * blk:(n + 1) * blk, :] = attend(qx_ref[n * blk:(n + 1) * blk, :], None, None)


def _gqa(q, k, v, sink, lay, ctx_queries):
    B, S, Lc = lay.B, lay.S, lay.Lc
    W = GQA_HQ * GQA_DH
    lat = lambda w: pl.BlockSpec((S, w), lambda b: (lay.lat_seq_block + b, 0))
    ctx = lambda w: pl.BlockSpec((Lc, w), lambda b: (b, 0))
    in_specs = [pl.BlockSpec(memory_space=pltpu.SMEM), lat(W), lat(LANES), lat(LANES), ctx(LANES), ctx(LANES)]
    out_specs = [pl.BlockSpec((S, W), lambda b: (b, 0))]
    out_shape = [jax.ShapeDtypeStruct((lay.n_lat, W), BF)]
    args = [sink, q, k, v, k, v]
    if ctx_queries:
        in_specs.append(ctx(W))
        out_specs.append(ctx(W))
        out_shape.append(jax.ShapeDtypeStruct((lay.n_ctx, W), BF))
        args.append(q)
    return pl.pallas_call(
        functools.partial(_gqa_kernel, S=S, Lc=Lc, ctx_queries=ctx_queries),
        grid=(B,),
        in_specs=in_specs,
        out_specs=out_specs,
        out_shape=out_shape,
        scratch_shapes=[pltpu.VMEM((4, S, LANES), BF), pltpu.VMEM((4, S, LANES), BF),
                        pltpu.VMEM((4, Lc, LANES), BF), pltpu.VMEM((4, Lc, LANES), BF)],
        compiler_params=_params(1),
        name="gqa_window",
    )(*args)


def _mla_heads(q_ref, kv_refs, o_ref):
    lane = lax.broadcasted_iota(jnp.int32, (1, LANES), 1)
    low = lane < MLA_DV
    for pair in range(MLA_H // 2):
        vsl = slice(pair * LANES, (pair + 1) * LANES)
        acc = None
        for half in range(2):
            h = 2 * pair + half
            hsl = slice(h * LANES, (h + 1) * LANES)
            qh = q_ref[:, hsl]
            scores = [_dot_nt(qh, k_ref[:, hsl]) for k_ref, _ in kv_refs]
            m = functools.reduce(jnp.maximum, [jnp.max(s, axis=-1, keepdims=True) for s in scores])
            keep = low if half == 0 else jnp.logical_not(low)
            ones_lane = LANES - 1 if half == 0 else 0
            t = None
            for s, (_, v_ref) in zip(scores, kv_refs):
                vh = jnp.where(keep, v_ref[:, vsl], jnp.zeros((), BF))
                vh = jnp.where(lane == ones_lane, jnp.ones((), BF), vh)
                u = _dot(jnp.exp2(s - m).astype(BF), vh)
                t = u if t is None else t + u
            o = jnp.where(keep, t * (1.0 / t[:, ones_lane:ones_lane + 1]), 0.0)
            acc = o if acc is None else acc + o
        o_ref[:, vsl] = acc.astype(BF)


def _mla_lat_kernel(q_ref, kc_ref, kl_ref, vc_ref, vl_ref, o_ref):
    _mla_heads(q_ref, ((kc_ref, vc_ref), (kl_ref, vl_ref)), o_ref)


def _mla_ctx_kernel(q_ref, kc_ref, vc_ref, o_ref):
    _mla_heads(q_ref, ((kc_ref, vc_ref),), o_ref)


def _mla_latent(q, k, v, lay):
    B, S, Lc = lay.B, lay.S, lay.Lc
    tq = min(512, S)
    nq = S // tq
    lat_q0 = lay.n_ctx // tq
    kw, vw = MLA_H * LANES, MLA_H * MLA_DV
    return pl.pallas_call(
        _mla_lat_kernel,
        grid=(B, nq),
        in_specs=[pl.BlockSpec((tq, kw), lambda b, i: (lat_q0 + b * nq + i, 0)),
                  pl.BlockSpec((Lc, kw), lambda b, i: (b, 0)),
                  pl.BlockSpec((S, kw), lambda b, i: (lay.lat_seq_block + b, 0)),
                  pl.BlockSpec((Lc, vw), lambda b, i: (b, 0)),
                  pl.BlockSpec((S, vw), lambda b, i: (lay.lat_seq_block + b, 0))],
        out_specs=pl.BlockSpec((tq, vw), lambda b, i: (b * nq + i, 0)),
        out_shape=jax.ShapeDtypeStruct((lay.n_lat, vw), BF),
        compiler_params=_params(2),
        name="mla_latent",
    )(q, k, k, v, v)


def _mla_context(q, k, v, lay):
    B, Lc = lay.B, lay.Lc
    kw, vw = MLA_H * LANES, MLA_H * MLA_DV
    blk = lambda w: pl.BlockSpec((Lc, w), lambda b: (b, 0))
    return pl.pallas_call(
        _mla_ctx_kernel,
        grid=(B,),
        in_specs=[blk(kw), blk(kw), blk(vw)],
        out_specs=blk(vw),
        out_shape=jax.ShapeDtypeStruct((lay.n_ctx, vw), BF),
        compiler_params=_params(1),
        name="mla_context",
    )(q, k, v)


def _ret_chunk(Lc, S):
    return 256 if Lc % 256 == 0 and S % 256 == 0 else RET_CHUNK


def _ret_kernel(dec_ref, qc_ref, ql_ref, kc_ref, kl_ref, vc_ref, vl_ref, gc_ref, gl_ref, gn_ref,
                oc_ref, ol_ref, kv_s, r_s, *, Lc, S, ch):
    nc, nl = Lc // ch, S // ch
    n = nc + nl
    lg_f = jnp.log(_sigmoid(dec_ref[0, 0:1, :]))
    lg_b = jnp.log(_sigmoid(dec_ref[0, 1:2, :]))
    pos_i = lax.broadcasted_iota(jnp.int32, (ch, ch), 0).astype(F32)
    pos_j = lax.broadcasted_iota(jnp.int32, (ch, ch), 1).astype(F32)
    diff = pos_i - pos_j
    fwd = diff >= 0.0
    mask = jnp.where(fwd, jnp.exp(jnp.where(fwd, diff, 0.0) * lg_f), jnp.exp(jnp.where(fwd, 0.0, -diff) * lg_b))
    pos = pos_i[:, 0:1]
    lf, lb = lg_f[:, :LANES], lg_b[:, :LANES]
    zeta = (jnp.exp((ch - 1.0 - pos) * lf), jnp.exp(pos * lb))
    xi = (jnp.exp((pos + 1.0) * lf), jnp.exp((ch - pos) * lb))
    decay = (jnp.exp(ch * lf), jnp.exp(ch * lb))

    def rows(c_ref, l_ref, c):
        if c < nc:
            return c_ref[c * ch:(c + 1) * ch, :]
        return l_ref[(c - nc) * ch:(c - nc + 1) * ch, :]

    for c in range(n):
        k = rows(kc_ref, kl_ref, c).astype(F32)
        kz = jnp.concatenate([(k * zeta[0]).astype(BF), (k * zeta[1]).astype(BF)], axis=1)
        kv_s[c] = _dot_tn(kz, rows(vc_ref, vl_ref, c))

    orders = (list(range(n)), list(range(nc - 1, -1, -1)) + list(range(n - 1, nc - 1, -1)))
    for d, order in enumerate(orders):
        R = jnp.zeros((RET_DK, RET_DV), F32)
        for c in order:
            r_s[c, d * RET_DK:(d + 1) * RET_DK, :] = R.astype(BF)
            R = decay[d] * R + kv_s[c, d * RET_DK:(d + 1) * RET_DK, :]

    for c in range(n):
        q = rows(qc_ref, ql_ref, c)
        s = _dot_nt(q, rows(kc_ref, kl_ref, c)) * mask
        qf = q.astype(F32)
        qx = jnp.concatenate([(qf * xi[0]).astype(BF), (qf * xi[1]).astype(BF)], axis=1)
        y = _dot(s.astype(BF), rows(vc_ref, vl_ref, c)) + _dot(qx, r_s[c])
        mu = jnp.mean(y, axis=-1, keepdims=True)
        yc = y - mu
        var = jnp.mean(yc * yc, axis=-1, keepdims=True)
        yn = yc * lax.rsqrt(var + LN_EPS) * gn_ref[...]
        g = rows(gc_ref, gl_ref, c).astype(F32)
        out = (g * _sigmoid(g) * yn).astype(BF)
        if c < nc:
            oc_ref[c * ch:(c + 1) * ch, :] = out
        else:
            ol_ref[(c - nc) * ch:(c - nc + 1) * ch, :] = out


def _retention(dq, dk, dv, dg, decay_lanes, gn, lay):
    B, S, Lc = lay.B, lay.S, lay.Lc
    ch = _ret_chunk(Lc, S)
    n = (Lc + S) // ch
    ctx = pl.BlockSpec((Lc, LANES), lambda b, h: (b, h))
    lat = pl.BlockSpec((S, LANES), lambda b, h: (lay.lat_seq_block + b, h))
    return pl.pallas_call(
        functools.partial(_ret_kernel, Lc=Lc, S=S, ch=ch),
        grid=(B, RET_H),
        in_specs=[pl.BlockSpec((1, 2, ch), lambda b, h: (h, 0, 0)),
                  ctx, lat, ctx, lat, ctx, lat, ctx, lat, pl.BlockSpec((1, LANES), lambda b, h: (0, h))],
        out_specs=[pl.BlockSpec((Lc, LANES), lambda b, h: (b, h)), pl.BlockSpec((S, LANES), lambda b, h: (b, h))],
        out_shape=[jax.ShapeDtypeStruct((lay.n_ctx, RET_H * RET_DV), BF), jax.ShapeDtypeStruct((lay.n_lat, RET_H * RET_DV), BF)],
        scratch_shapes=[pltpu.VMEM((n, 2 * RET_DK, RET_DV), F32), pltpu.VMEM((n, 2 * RET_DK, RET_DV), BF)],
        compiler_params=_params(2),
        name="retention",
    )(decay_lanes, dq, dq, dk, dk, dv, dv, dg, dg, gn)


def _merge_kernel(*refs, n_br, ct):
    x_ref, mod_ref = refs[:2]
    br_refs = refs[2:2 + N_BRANCH * n_br]
    wg_ref, wb_ref, wo_ref, g_ref, beta_ref, o_ref = refs[2 + N_BRANCH * n_br:]
    x = x_ref[...]
    h = (x * (1.0 + mod_ref[0, 4:5, :]) + mod_ref[0, 3:4, :]).astype(BF)
    acc = None
    for k in range(N_BRANCH):
        br = _rows(br_refs[k * n_br:(k + 1) * n_br], ct)
        gate = 0.5 * jnp.tanh(_dot(h, wg_ref[:, k * D_MODEL:(k + 1) * D_MODEL])) + 0.5
        t = gate * _dot(br, wb_ref[k])
        acc = t if acc is None else acc + t
    y = _dot(acc.astype(BF), wo_ref[...])
    z = DN_ALPHA * x + mod_ref[0, 5:6, :] * y
    o_ref[...] = _layer_norm(z, g_ref[...], beta_ref[...])


def _merge(x, mod, branches, w_gate, wb_all, wo_all, l, g, beta, lay, g_off):
    tm = lay.tm
    n = lay.tiles - g_off
    br_specs, br_args = [], []
    for br in branches:
        specs, args = _row_specs(br, BRANCH_W, lay, g_off)
        br_specs += specs
        br_args += args
    return pl.pallas_call(
        functools.partial(_merge_kernel, n_br=len(br_args) // N_BRANCH, ct=lay.ctx_tiles),
        grid=(n,),
        in_specs=[pl.BlockSpec((tm, D_MODEL), lambda i: (i + g_off, 0)),
                  pl.BlockSpec((1, 9, D_MODEL), lambda i: (lay.mod_index(i + g_off), 0, 0))] + br_specs + [
                  pl.BlockSpec((None,) + w_gate.shape[1:], lambda i: (l, 0, 0), pipeline_mode=pl.Buffered(1)),
                  pl.BlockSpec((None, N_BRANCH, BRANCH_W, D_MODEL), lambda i: (l, 0, 0, 0), pipeline_mode=pl.Buffered(1)),
                  pl.BlockSpec((None, D_MODEL, D_MODEL), lambda i: (l, 0, 0), pipeline_mode=pl.Buffered(1)),
                  _const((1, D_MODEL)), _const((1, D_MODEL))],
        out_specs=pl.BlockSpec((tm, D_MODEL), lambda i: (i, 0)),
        out_shape=jax.ShapeDtypeStruct((n * tm, D_MODEL), F32),
        compiler_params=_params(1),
        name="gated_merge",
    )(x, mod, *br_args, w_gate, wb_all, wo_all, g, beta)


def _rope_tables(S, head_dim, lead, reps, tail, tm):
    nf = head_dim // 4
    inv = ROPE_BASE ** (-jnp.arange(nf, dtype=F32) / nf)
    n_rows = S // GRID_W
    ar = jnp.arange(n_rows, dtype=F32)[:, None] * inv[None, :]
    ac = jnp.arange(GRID_W, dtype=F32)[:, None] * inv[None, :]

    def lanes(by_row, first, second, fill):
        n = first.shape[0]
        neutral = jnp.full((n, nf), fill, F32)
        head = jnp.concatenate([first, second, neutral, neutral] if by_row else [neutral, neutral, first, second], -1)
        pad = lambda w: jnp.full((n, w), fill, F32)
        return jnp.tile(jnp.concatenate([pad(lead), head, pad(tail)], -1), (1, reps))

    row_c = lanes(True, jnp.cos(ar), jnp.cos(ar), 1.0)
    row_s = lanes(True, -jnp.sin(ar), jnp.sin(ar), 0.0)
    col_c = lanes(False, jnp.cos(ac), jnp.cos(ac), 1.0)
    col_s = lanes(False, -jnp.sin(ac), jnp.sin(ac), 0.0)
    W = row_c.shape[1]
    per_tile = tm // GRID_W

    def by_tile(tab, fill):
        t = tab.reshape(S // tm, per_tile, W)
        t = jnp.concatenate([t, jnp.full((S // tm, SUBLANES - per_tile, W), fill, F32)], 1).reshape(-1, W)
        return jnp.concatenate([t, jnp.full((SUBLANES, W), fill, F32)], 0)

    ident = lambda fill: jnp.full((GRID_W, W), fill, F32)
    return (by_tile(row_c, 1.0), by_tile(row_s, 0.0),
            jnp.stack([col_c, ident(1.0)]), jnp.stack([col_s, ident(0.0)]))


def _mixer_weights(w_in, lru_w_a, lru_b_a, lru_w_x, lru_b_x, mla_w_uq, mla_w_ukv):
    wb = w_in.astype(BF)
    cut = sum(IN_SIZES[:8])
    gate0 = sum(IN_SIZES[:12])
    w_proj = jnp.concatenate([wb[:, :, :cut], jnp.zeros((DEPTH, D_MODEL, LANES - MLA_DR), BF), wb[:, :, cut:gate0]], axis=2)
    w_gate = wb[:, :, gate0:] * jnp.asarray(0.5, BF)
    blocks = jnp.stack([lru_w_a[:, 0], lru_w_x[:, 0], lru_w_a[:, 1], lru_w_x[:, 1]], axis=1)
    eye = jnp.eye(LRU_BLOCKS, dtype=F32)
    dense = (0.5 * eye)[None, None, :, None, :, None] * blocks[:, :, :, :, None, :]
    wg = dense.reshape(DEPTH, 4, LRU_W, LRU_W).transpose(0, 2, 1, 3).reshape(DEPTH, LRU_W, 4 * LRU_W).astype(BF)
    bg = 0.5 * jnp.stack([lru_b_a[:, 0], lru_b_x[:, 0], lru_b_a[:, 1], lru_b_x[:, 1]], axis=1).reshape(DEPTH, 1, 4 * LRU_W)
    uq = mla_w_uq.reshape(DEPTH, MLA_RQ, MLA_H, MLA_DN + MLA_DR)
    wq = jnp.pad(uq, ((0, 0), (0, 0), (0, 0), (0, LANES - MLA_DN - MLA_DR))).reshape(DEPTH, MLA_RQ, MLA_H * LANES).astype(BF)
    ukv = mla_w_ukv.reshape(DEPTH, MLA_RKV, MLA_H, MLA_DN + MLA_DV)
    wk = jnp.pad(ukv[..., :MLA_DN], ((0, 0), (0, 0), (0, 0), (0, LANES - MLA_DN))).reshape(DEPTH, MLA_RKV, MLA_H * LANES).astype(BF)
    wv = ukv[..., MLA_DN:].reshape(DEPTH, MLA_RKV, MLA_H * MLA_DV).astype(BF)
    return w_proj, w_gate, wg, bg, wq, wk, wv


def kernel(x, c, ctx, c_ctx, w_mod, b_mod, ln_g, ln_b, ffn_w_in, ffn_w_out, w_in, lru_conv_w, lru_conv_b, lru_w_a, lru_b_a, lru_w_x, lru_b_x, lru_lambda, gqa_sink, mla_q_norm, mla_kv_norm, mla_w_uq, mla_w_ukv, ret_decay, ret_gn_g, w_branch, w_out):
    B, S, D = x.shape
    Lc = ctx.shape[1]
    lay = _Layout(B, S, Lc)
    tm = lay.tm

    n_cond = -(-(B + 1) // SUBLANES) * SUBLANES
    cond = jnp.concatenate([c, c_ctx[None, :], jnp.zeros((n_cond - B - 1, D), F32)], axis=0)
    mod_all = _modulation(cond, w_mod, b_mod)[:, :B + 1].reshape(DEPTH, B + 1, 9, D)

    tabs = [_rope_tables(S, head_dim, lead, reps, tail, tm) for head_dim, lead, reps, tail in
            ((GQA_DH, 0, GQA_HQ, 0), (GQA_DH, 0, GQA_HKV, 0), (MLA_DR, 0, 1, LANES - MLA_DR), (RET_DK, 0, RET_H, 0),
             (MLA_DR, MLA_DN, MLA_H, LANES - MLA_DN - MLA_DR))]

    tok = (ctx.reshape(B * Lc, D), x.reshape(B * S, D))
    row = lambda v: v[None, :]
    ct = lay.ctx_tiles
    w1_all, w2_all = ffn_w_in.astype(BF), ffn_w_out.astype(BF)
    wb_all, wo_all = w_branch.astype(BF), w_out.astype(BF)
    w_proj, w_gate, wg, bg, wq, wk, wv = _mixer_weights(w_in, lru_w_a, lru_b_a, lru_w_x, lru_b_x, mla_w_uq, mla_w_ukv)

    for l in range(DEPTH):
        last = l == DEPTH - 1
        mod = mod_all[l]
        dec = jnp.broadcast_to(ret_decay[l].T[:, :, None], (RET_H, 2, _ret_chunk(Lc, S)))

        tok = _half_ffn(tok, mod, w1_all, w2_all, l, 0, row(ln_g[l, 0]), row(ln_b[l, 0]), lay, 0, 0)

        ax, ay, bq, bk, bv, mq, mk, mv, dq, dk, dv, dg = _mixer_proj(
            tok, mod, w_proj, l, wq[l], wk[l], wv[l], row(mla_q_norm[l]), row(mla_kv_norm[l]), tabs, lay)
        a_c, a_l = _lru(ax, ay, lru_conv_w[l], row(lru_conv_b[l]), wg[l], bg[l], lru_lambda[l], lay)
        b_out = _gqa(bq, bk, bv, gqa_sink[l], lay, ctx_queries=not last)
        c_l = _mla_latent(mq, mk, mv, lay)
        d_c, d_l = _retention(dq, dk, dv, dg, dec, row(ret_gn_g[l]), lay)

        if last:
            branches = (a_l, b_out[0], c_l, d_l)
            g_off = ct
        else:
            branches = ((a_c, a_l), (b_out[1], b_out[0]), (_mla_context(mq, mk, mv, lay), c_l), (d_c, d_l))
            g_off = 0
        tok = _merge(tok, mod, branches, w_gate, wb_all, wo_all, l, row(ln_g[l, 1]), row(ln_b[l, 1]), lay, g_off)
        tok = _half_ffn(tok, mod, w1_all, w2_all, l, 1, row(ln_g[l, 2]), row(ln_b[l, 2]), lay, 6, g_off)

    return tok.reshape(B, S, D)
```

```python
import functools

import jax
import jax.numpy as jnp
import numpy as np
from jax import lax
from jax.experimental import pallas as pl
from jax.experimental.pallas import tpu as pltpu

D_MODEL = 1024
DEPTH = 2
GRID_W = 64
N_BRANCH = 4
BRANCH_W = 512
LRU_W = 512
LRU_BLOCKS = 8
LRU_BW = LRU_W // LRU_BLOCKS
LRU_C = 8.0
GQA_HQ = 8
GQA_HKV = 2
GQA_DH = 64
WINDOW = 128
ATT_BLOCK = 128
MLA_H = 8
MLA_RQ = 384
MLA_RKV = 256
MLA_DN = 64
MLA_DR = 32
MLA_DV = 64
RET_H = 4
RET_DK = 128
RET_DV = 128
RET_CHUNK = 128
FF = 2816
ROPE_BASE = 10000.0
LN_EPS = 1e-5
MASK_VALUE = -1e30
DN_ALPHA = (2 * DEPTH) ** 0.25
IN_SIZES = (LRU_W, LRU_W, GQA_HQ * GQA_DH, GQA_HKV * GQA_DH, GQA_HKV * GQA_DH, MLA_RQ, MLA_RKV, MLA_DR,
            RET_H * RET_DK, RET_H * RET_DK, RET_H * RET_DV, RET_H * RET_DV, N_BRANCH * D_MODEL)

LANES = 128
SUBLANES = 8
VMEM_LIMIT = 56 * 1024 * 1024
MXU_DIM = 256
FFN_CHUNKS = (6 * MXU_DIM, 5 * MXU_DIM)
LOG2E = 1.4426950408889634
BF = jnp.bfloat16
F32 = jnp.float32


def _params(n_axes):
    return pltpu.CompilerParams(dimension_semantics=("arbitrary",) * n_axes, vmem_limit_bytes=VMEM_LIMIT)


def _resident(shape):
    nd = len(shape)
    return pl.BlockSpec(shape, lambda *_: (0,) * nd, pipeline_mode=pl.Buffered(1))


def _const(shape):
    nd = len(shape)
    return pl.BlockSpec(shape, lambda *_: (0,) * nd)


def _layer_norm(z, g, b):
    mu = jnp.mean(z, axis=-1, keepdims=True)
    zc = z - mu
    var = jnp.mean(zc * zc, axis=-1, keepdims=True)
    return zc * lax.rsqrt(var + LN_EPS) * g + b


def _sigmoid(x):
    return 1.0 / (1.0 + jnp.exp(-x))


def _dot(a, b):
    return jnp.dot(a, b, preferred_element_type=F32)


def _dot_nt(a, b):
    return lax.dot_general(a, b, (((1,), (1,)), ((), ())), preferred_element_type=F32)


def _dot_tn(a, b):
    return lax.dot_general(a, b, (((0,), (0,)), ((), ())), preferred_element_type=F32)


def _rope(y, c, s, nf):
    w = y.shape[-1]
    lane = lax.broadcasted_iota(jnp.int32, y.shape, 1)
    partner = jnp.where((lane % (2 * nf)) < nf, pltpu.roll(y, w - nf, axis=1), pltpu.roll(y, nf, axis=1))
    return y * c + partner * s


class _Layout:
    def __init__(self, B, S, Lc):
        self.B, self.S, self.Lc = B, S, Lc
        self.n_ctx, self.n_lat = B * Lc, B * S
        self.n_tok = self.n_ctx + self.n_lat
        tm = 512
        while S % tm or self.n_ctx % tm:
            tm //= 2
        self.tm = tm
        self.ctx_tiles = self.n_ctx // tm
        self.tiles_per_seq = S // tm
        self.tiles = self.ctx_tiles + B * self.tiles_per_seq
        assert self.n_ctx % S == 0, "latent rows must start on a whole-sequence block"
        self.lat_seq_block = self.n_ctx // S
        self.lat_att_block = self.n_ctx // ATT_BLOCK

    def mod_index(self, g):
        return jnp.where(g < self.ctx_tiles, self.B, (g - self.ctx_tiles) // self.tiles_per_seq)

    def rope_index(self, g):
        return jnp.where(g < self.ctx_tiles, self.tiles_per_seq, (g - self.ctx_tiles) % self.tiles_per_seq)


def _mod_kernel(s_ref, w_ref, b_ref, o_ref):
    s = s_ref[...]
    s = s * _sigmoid(s)
    o_ref[0] = _dot(s.astype(BF), w_ref[0].astype(BF)) + b_ref[0]


def _modulation(cond, w_mod, b_mod):
    R = cond.shape[0]
    tn = 1024
    n9 = 9 * D_MODEL
    return pl.pallas_call(
        _mod_kernel,
        grid=(DEPTH, n9 // tn),
        in_specs=[pl.BlockSpec((R, D_MODEL), lambda l, j: (0, 0)),
                  pl.BlockSpec((1, D_MODEL, tn), lambda l, j: (l, 0, j)),
                  pl.BlockSpec((1, 1, tn), lambda l, j: (l, 0, j))],
        out_specs=pl.BlockSpec((1, R, tn), lambda l, j: (l, 0, j)),
        out_shape=jax.ShapeDtypeStruct((DEPTH, R, n9), F32),
        compiler_params=_params(2),
        name="modulation",
    )(cond, w_mod, b_mod.reshape(DEPTH, 1, n9))


def _row_specs(src, width, lay, g_off, tile=lambda i: i):
    tm, ct = lay.tm, lay.ctx_tiles
    if isinstance(src, tuple):
        assert g_off == 0
        return [pl.BlockSpec((tm, width), lambda i: (jnp.minimum(tile(i), ct - 1), 0)),
                pl.BlockSpec((tm, width), lambda i: (jnp.maximum(tile(i) - ct, 0), 0))], list(src)
    assert src.shape[0] == lay.n_tok or (src.shape[0] == lay.n_lat and g_off == ct)
    off = g_off if src.shape[0] == lay.n_tok else 0
    return [pl.BlockSpec((tm, width), lambda i: (tile(i) + off, 0))], [src]


def _rows(refs, t, ct):
    if len(refs) == 2:
        return jnp.where(t < ct, refs[0][...], refs[1][...])
    return refs[0][...]


def _ffn_kernel(*refs, k0, n_x, ct):
    mod_ref, w1_ref, w2_ref, g_ref, b_ref, o_ref = refs[n_x:]
    x = _rows(refs[:n_x], pl.program_id(0), ct)
    shift = mod_ref[0, k0:k0 + 1, :]
    scale = mod_ref[0, k0 + 1:k0 + 2, :]
    gate = mod_ref[0, k0 + 2:k0 + 3, :]
    xm = (x * (1.0 + scale) + shift).astype(BF)
    acc = jnp.zeros(x.shape, F32)
    assert sum(FFN_CHUNKS) == FF
    lo = 0
    for width in FFN_CHUNKS:
        a = _dot(xm, w1_ref[:, lo:lo + width])
        b = _dot(xm, w1_ref[:, FF + lo:FF + lo + width])
        h = (a * _sigmoid(a) * b).astype(BF)
        acc = acc + _dot(h, w2_ref[lo:lo + width, :])
        lo += width
    z = DN_ALPHA * x + (0.5 * gate) * acc
    o_ref[...] = _layer_norm(z, g_ref[...], b_ref[...])


def _half_ffn(x, mod, w1_all, w2_all, l, j, g, b, lay, k0, g_off):
    tm = lay.tm
    n = lay.tiles - g_off
    x_specs, x_args = _row_specs(x, D_MODEL, lay, g_off)
    pick = lambda *_: (l, j, 0, 0)
    return pl.pallas_call(
        functools.partial(_ffn_kernel, k0=k0, n_x=len(x_args), ct=lay.ctx_tiles),
        grid=(n,),
        in_specs=x_specs + [
            pl.BlockSpec((1, 9, D_MODEL), lambda i: (lay.mod_index(i + g_off), 0, 0)),
            pl.BlockSpec((None, None, D_MODEL, 2 * FF), pick, pipeline_mode=pl.Buffered(1)),
            pl.BlockSpec((None, None, FF, D_MODEL), pick, pipeline_mode=pl.Buffered(1)),
            _const((1, D_MODEL)), _const((1, D_MODEL))],
        out_specs=pl.BlockSpec((tm, D_MODEL), lambda i: (i, 0)),
        out_shape=jax.ShapeDtypeStruct((n * tm, D_MODEL), F32),
        compiler_params=_params(1),
        name="half_ffn",
    )(*x_args, mod, w1_all, w2_all, g, b)


_W_LRU, _W_GQ, _W_GKV, _W_RET = LRU_W, GQA_HQ * GQA_DH, GQA_HKV * GQA_DH, RET_H * RET_DK
_OFF_AX, _OFF_AY, _OFF_BQ, _OFF_BKV = 0, 512, 1024, 1536
_OFF_MLA, _W_MLA = 1792, MLA_RQ + MLA_RKV + LANES
_W_HEAD = _OFF_MLA + _W_MLA
_OFF_DQ, _OFF_DK, _OFF_DV, _OFF_DG = 0, 512, 1024, 1536
_PROJ_OUT_WIDTHS = (_W_LRU, _W_LRU, _W_GQ, _W_GKV, _W_GKV, MLA_H * LANES, MLA_H * LANES, MLA_H * MLA_DV,
                    _W_RET, _W_RET, _W_RET, _W_RET)
_N_ROPE_TABLES = 5


def _rope_tile(y, tab, nf):
    rc_ref, rs_ref, cc_ref, cs_ref = tab
    parts = []
    for g in range(y.shape[0] // GRID_W):
        c = rc_ref[g:g + 1, :] * cc_ref[...]
        s = rs_ref[g:g + 1, :] + cs_ref[...]
        parts.append(_rope(y[g * GRID_W:(g + 1) * GRID_W, :], c, s, nf))
    return jnp.concatenate(parts, axis=0)


def _rms(x, g):
    return x * lax.rsqrt(jnp.mean(x * x, axis=-1, keepdims=True) + LN_EPS) * g


def _proj_kernel(x_ref, mod_ref, w_ref, wt_ref, wq_ref, wk_ref, wv_ref, gq_ref, gkv_ref, *refs):
    tabs = [refs[4 * i:4 * i + 4] for i in range(_N_ROPE_TABLES)]
    t_gq, t_gk, t_kr, t_ret, t_mq = tabs
    ax_o, ay_o, bq_o, bk_o, bv_o, mq_o, mk_o, mv_o, dq_o, dk_o, dv_o, dg_o = refs[4 * _N_ROPE_TABLES:]
    x = x_ref[...]
    h = (x * (1.0 + mod_ref[0, 4:5, :]) + mod_ref[0, 3:4, :]).astype(BF)
    proj = lambda lo, width: _dot(h, w_ref[:, lo:lo + width])
    tail = lambda lo, width: _dot(h, wt_ref[:, lo:lo + width])

    ax_o[...] = proj(_OFF_AX, _W_LRU).astype(BF)
    ay_o[...] = proj(_OFF_AY, _W_LRU).astype(BF)
    bq_o[...] = (_rope_tile(proj(_OFF_BQ, _W_GQ), t_gq, GQA_DH // 4) * (GQA_DH ** -0.5 * LOG2E)).astype(BF)
    kv = proj(_OFF_BKV, 2 * _W_GKV)
    bk_o[...] = _rope_tile(kv[:, :_W_GKV], t_gk, GQA_DH // 4).astype(BF)
    bv_o[...] = kv[:, _W_GKV:].astype(BF)

    c = proj(_OFF_MLA, _W_MLA)
    yq = _rms(c[:, :MLA_RQ], gq_ref[...]).astype(BF)
    mq_o[...] = (_rope_tile(_dot(yq, wq_ref[...]), t_mq, MLA_DR // 4) * ((MLA_DN + MLA_DR) ** -0.5 * LOG2E)).astype(BF)
    ykv = _rms(c[:, MLA_RQ:MLA_RQ + MLA_RKV], gkv_ref[...]).astype(BF)
    kr = c[:, MLA_RQ + MLA_RKV:]
    kr = jnp.where(lax.broadcasted_iota(jnp.int32, kr.shape, 1) < MLA_DR, kr, 0.0)
    kr = _rope_tile(kr, t_kr, MLA_DR // 4)
    kr_all = pltpu.roll(jnp.concatenate([kr] * MLA_H, axis=1), MLA_DN, axis=1)
    mk_o[...] = (_dot(ykv, wk_ref[...]) + kr_all).astype(BF)
    mv_o[...] = _dot(ykv, wv_ref[...]).astype(BF)

    dq_o[...] = _rope_tile(tail(_OFF_DQ, _W_RET), t_ret, RET_DK // 4).astype(BF)
    dk_o[...] = (_rope_tile(tail(_OFF_DK, _W_RET), t_ret, RET_DK // 4) * RET_DK ** -0.5).astype(BF)
    dv_o[...] = tail(_OFF_DV, _W_RET).astype(BF)
    dg_o[...] = tail(_OFF_DG, _W_RET).astype(BF)


def _rope_specs(tab, lay):
    w = tab[0].shape[1]
    row = pl.BlockSpec((SUBLANES, w), lambda i: (lay.rope_index(i), 0))
    col = pl.BlockSpec((None, GRID_W, w), lambda i: (jnp.where(i < lay.ctx_tiles, 1, 0), 0, 0))
    return [row, row, col, col]


def _mixer_proj(x, mod, w_all, l, wq, wk, wv, gq, gkv, tabs, lay):
    tm = lay.tm
    row = lambda i: (i, 0)
    w_head, w_tail = w_all
    assert len(tabs) == _N_ROPE_TABLES
    tab_specs, tab_args = [], []
    for tab in tabs:
        tab_specs += _rope_specs(tab, lay)
        tab_args += list(tab)
    return pl.pallas_call(
        _proj_kernel,
        grid=(lay.tiles,),
        in_specs=[pl.BlockSpec((tm, D_MODEL), row),
                  pl.BlockSpec((1, 9, D_MODEL), lambda i: (lay.mod_index(i), 0, 0)),
                  pl.BlockSpec((None, D_MODEL, _W_HEAD), lambda i: (l, 0, 0), pipeline_mode=pl.Buffered(1)),
                  pl.BlockSpec((None,) + w_tail.shape[1:], lambda i: (l, 0, 0), pipeline_mode=pl.Buffered(1)),
                  _const(wq.shape), _const(wk.shape), _const(wv.shape), _const(gq.shape), _const(gkv.shape)] + tab_specs,
        out_specs=[pl.BlockSpec((tm, w), row) for w in _PROJ_OUT_WIDTHS],
        out_shape=[jax.ShapeDtypeStruct((lay.n_tok, w), BF) for w in _PROJ_OUT_WIDTHS],
        compiler_params=_params(1),
        name="mixer_proj",
    )(x, mod, w_head, w_tail, wq, wk, wv, gq, gkv, *tab_args)


def _conv4(x, w_ref, b_ref):
    T = x.shape[0]
    row = lax.broadcasted_iota(jnp.int32, x.shape, 0)
    y = b_ref[...] + w_ref[2:3, :] * x
    y = y + w_ref[0:1, :] * jnp.where(row >= 2, pltpu.roll(x, 2, axis=0), 0.0)
    y = y + w_ref[1:2, :] * jnp.where(row >= 1, pltpu.roll(x, 1, axis=0), 0.0)
    y = y + w_ref[3:4, :] * jnp.where(row < T - 1, pltpu.roll(x, T - 1, axis=0), 0.0)
    return y


def _scan8(a, b, reverse):
    row = lax.broadcasted_iota(jnp.int32, a.shape, 0)
    for s in (1, 2, 4):
        if reverse:
            keep = row < SUBLANES - s
            a_sh = pltpu.roll(a, SUBLANES - s, axis=0)
            b_sh = pltpu.roll(b, SUBLANES - s, axis=0)
        else:
            keep = row >= s
            a_sh = pltpu.roll(a, s, axis=0)
            b_sh = pltpu.roll(b, s, axis=0)
        b = jnp.where(keep, b + a * b_sh, b)
        a = jnp.where(keep, a * a_sh, a)
    return a, b


def _lru_segment(T):
    for seg in (36, 44, 28, 20, 12, 52, 60, 4, 8, 16, 32):
        if T % (SUBLANES * seg) == 0:
            return seg
    raise ValueError(f"no scan segment length for {T} rows")


def _lru_kernel(axc_ref, axl_ref, ayc_ref, ayl_ref, cw_ref, cb_ref, wg_ref, bg_ref, lam_ref,
                oc_ref, ol_ref, xa_s, a_s, b_s, h0_s, h1_s, ac_s, bc_s, *, Lc, S, rows, seg):
    W = LRU_W
    T = Lc + S
    NS = W // LANES
    G = SUBLANES * seg
    xa_s[0:Lc, :] = _conv4(axc_ref[...].astype(F32), cw_ref, cb_ref)
    xa_s[Lc:T, :] = _conv4(axl_ref[...].astype(F32), cw_ref, cb_ref)
    row8 = lax.broadcasted_iota(jnp.int32, (SUBLANES, LANES), 0)
    zero8 = jnp.zeros((SUBLANES, LANES), F32)

    for d in range(2):
        reverse = d == 1
        h_s = h1_s if reverse else h0_s
        neg = -lam_ref[d:d + 1, :]
        softplus = jnp.maximum(neg, 0.0) + jnp.log(1.0 + jnp.exp(-jnp.abs(neg)))
        c = (-0.5 * LRU_C * 1.4426950408889634) * softplus

        def gates(i, carry):
            r0 = pl.multiple_of(i * rows, rows)
            dst = pl.multiple_of(jnp.where(r0 < Lc, r0 + S, r0 - Lc), rows) if reverse else r0
            xa = xa_s[pl.ds(r0, rows), :]
            xb = xa.astype(BF)
            lo = 2 * d * W
            t_r = jnp.tanh(_dot(xb, wg_ref[:, lo:lo + W]) + bg_ref[:, lo:lo + W])
            t_i = jnp.tanh(_dot(xb, wg_ref[:, lo + W:lo + 2 * W]) + bg_ref[:, lo + W:lo + 2 * W])
            a = jnp.exp2(c * t_r + c)
            y = 1.0 - a * a
            root = jnp.where(y > 0.0, y * lax.rsqrt(y), 0.0)
            b = (root * (0.5 * xa)) * (t_i + 1.0)
            for k in range(NS):
                a_s[k, pl.ds(dst, rows), :] = a[:, k * LANES:(k + 1) * LANES]
                b_s[k, pl.ds(dst, rows), :] = b[:, k * LANES:(k + 1) * LANES]
            return carry

        lax.fori_loop(0, T // rows, gates, 0)

        def group(j, h_in):
            base = ((T // G - 1 - j) if reverse else j) * G
            acc_a, acc_b = [None] * NS, [None] * NS
            for i in (range(seg - 1, -1, -1) if reverse else range(seg)):
                for k in range(NS):
                    a = a_s[k, pl.ds(base + i, SUBLANES, stride=seg), :]
                    b = b_s[k, pl.ds(base + i, SUBLANES, stride=seg), :]
                    if acc_a[k] is None:
                        acc_a[k], acc_b[k] = a, b
                    else:
                        acc_b[k] = a * acc_b[k] + b
                        acc_a[k] = a * acc_a[k]
                    ac_s[k, i] = acc_a[k]
                    bc_s[k, i] = acc_b[k]
            h_out, enter = [], []
            for k in range(NS):
                tot_a, tot_b = _scan8(acc_a[k], acc_b[k], reverse)
                after = tot_a * h_in[k] + tot_b
                if reverse:
                    enter.append(jnp.where(row8 == SUBLANES - 1, h_in[k], pltpu.roll(after, SUBLANES - 1, axis=0)))
                    h_out.append(jnp.broadcast_to(after[0:1, :], after.shape))
                else:
                    enter.append(jnp.where(row8 == 0, h_in[k], pltpu.roll(after, 1, axis=0)))
                    h_out.append(jnp.broadcast_to(after[SUBLANES - 1:SUBLANES, :], after.shape))
            for i in range(seg):
                for k in range(NS):
                    h_s[k, pl.ds(base + i, SUBLANES, stride=seg), :] = ac_s[k, i] * enter[k] + bc_s[k, i]
            return tuple(h_out)

        lax.fori_loop(0, T // G, group, (zero8,) * NS)

    def gelu(v):
        return 0.5 * v * (1.0 + jnp.tanh(0.7978845608028654 * (v + 0.044715 * v * v * v)))

    for k in range(NS):
        sl = slice(k * LANES, (k + 1) * LANES)
        hc = h0_s[k, 0:Lc, :] + h1_s[k, S:T, :]
        hl = h0_s[k, Lc:T, :] + h1_s[k, 0:S, :]
        oc_ref[:, sl] = (hc * gelu(ayc_ref[:, sl].astype(F32))).astype(BF)
        ol_ref[:, sl] = (hl * gelu(ayl_ref[:, sl].astype(F32))).astype(BF)


def _lru(ax, ay, conv_w, conv_b, wg, bg, lam, lay):
    B, S, Lc = lay.B, lay.S, lay.Lc
    W = LRU_W
    T = Lc + S
    rows = 256
    while Lc % rows or S % rows:
        rows //= 2
    seg = _lru_segment(T)
    ctx = pl.BlockSpec((Lc, W), lambda b: (b, 0))
    lat = pl.BlockSpec((S, W), lambda b: (b + lay.lat_seq_block, 0))
    slabs = pltpu.VMEM((W // LANES, T, LANES), F32)
    part = pltpu.VMEM((W // LANES, seg, SUBLANES, LANES), F32)
    return pl.pallas_call(
        functools.partial(_lru_kernel, Lc=Lc, S=S, rows=rows, seg=seg),
        grid=(B,),
        in_specs=[ctx, lat, ctx, lat, _const((4, W)), _const((1, W)), _const(wg.shape), _const((1, 4 * W)), _const((2, W))],
        out_specs=[pl.BlockSpec((Lc, W), lambda b: (b, 0)), pl.BlockSpec((S, W), lambda b: (b, 0))],
        out_shape=[jax.ShapeDtypeStruct((lay.n_ctx, W), BF), jax.ShapeDtypeStruct((lay.n_lat, W), BF)],
        scratch_shapes=[pltpu.VMEM((T, W), F32), slabs, slabs, slabs, slabs, part, part],
        compiler_params=_params(1),
        name="rglru",
    )(ax, ax, ay, ay, conv_w, conv_b, wg, bg, lam)


def _split_heads_pair(x, ones_lane):
    xf = x.astype(F32)
    sw = pltpu.roll(xf, GQA_DH, axis=1)
    lane = lax.broadcasted_iota(jnp.int32, xf.shape, 1)
    low = lane < GQA_DH
    lo_fill = jnp.where(lane == LANES - 1, 1.0, 0.0) if ones_lane else jnp.zeros_like(xf)
    hi_fill = jnp.where(lane == 0, 1.0, 0.0) if ones_lane else jnp.zeros_like(xf)
    head0 = (jnp.where(low, xf, lo_fill).astype(BF), jnp.where(low, hi_fill, sw).astype(BF))
    head1 = (jnp.where(low, sw, lo_fill).astype(BF), jnp.where(low, hi_fill, xf).astype(BF))
    return head0, head1


def _gqa_kernel(sink_ref, *refs, S, Lc, ctx_queries):
    if ctx_queries:
        q_ref, k_ref, v_ref, kx_ref, vx_ref, qx_ref, o_ref, ox_ref, ks_s, vs_s, kxs_s, vxs_s = refs
    else:
        q_ref, k_ref, v_ref, kx_ref, vx_ref, o_ref, ks_s, vs_s, kxs_s, vxs_s = refs
    blk = ATT_BLOCK
    nb = S // blk
    win = min(3 * blk, S)
    for src, dst, is_value in ((k_ref, ks_s, False), (v_ref, vs_s, True), (kx_ref, kxs_s, False), (vx_ref, vxs_s, True)):
        heads = _split_heads_pair(src[...], is_value)
        for j in range(GQA_HKV):
            for half in range(2):
                dst[2 * j + half] = heads[j][half]

    row = lax.broadcasted_iota(jnp.int32, (2 * blk, win), 0)
    col = lax.broadcasted_iota(jnp.int32, (2 * blk, win), 1)
    rel = (row % blk) - col
    top = row[:, 0:1] < blk
    low = lax.broadcasted_iota(jnp.int32, (1, LANES), 1) < GQA_DH

    def attend(q, kstart, valid):
        outs = []
        for j in range(GQA_HKV):
            q2 = jnp.concatenate([q[:, 2 * j * LANES:(2 * j + 1) * LANES], q[:, (2 * j + 1) * LANES:(2 * j + 2) * LANES]], axis=0)
            acc = jnp.zeros((2 * blk, LANES), F32)
            for half in range(2):
                i = 2 * j + half
                sink = jnp.where(top, sink_ref[4 * j + half] * LOG2E, sink_ref[4 * j + 2 + half] * LOG2E)
                s_c = _dot_nt(q2, kxs_s[i])
                m = jnp.maximum(jnp.max(s_c, axis=-1, keepdims=True), sink)
                if kstart is not None:
                    s_w = jnp.where(valid, _dot_nt(q2, ks_s[i, pl.ds(kstart, win), :]), MASK_VALUE)
                    m = jnp.maximum(m, jnp.max(s_w, axis=-1, keepdims=True))
                t = _dot(jnp.exp2(s_c - m).astype(BF), vxs_s[i])
                if kstart is not None:
                    t = t + _dot(jnp.exp2(s_w - m).astype(BF), vs_s[i, pl.ds(kstart, win), :])
                ones_lane = LANES - 1 if half == 0 else 0
                den = t[:, ones_lane:ones_lane + 1] + jnp.exp2(sink - m)
                acc = acc + jnp.where(low if half == 0 else jnp.logical_not(low), t * (1.0 / den), 0.0)
            outs += [acc[0:blk].astype(BF), acc[blk:2 * blk].astype(BF)]
        return jnp.concatenate(outs, axis=1)

    def block(n, carry):
        q0 = pl.multiple_of(n * blk, blk)
        kstart = pl.multiple_of(jnp.clip((n - 1) * blk, 0, S - win), blk)
        dist = rel + (q0 - kstart)
        valid = (dist <= WINDOW) & (dist >= -WINDOW)
        o_ref[pl.ds(q0, blk), :] = attend(q_ref[pl.ds(q0, blk), :], kstart, valid)
        return carry

    lax.fori_loop(0, nb, block, 0, unroll=2)
    if ctx_queries:
        for n in range(Lc // blk):
            ox_ref[n * blk:(n + 1) * blk, :] = attend(qx_ref[n * blk:(n + 1) * blk, :], None, None)


def _gqa(q, k, v, sink, lay, ctx_queries):
    B, S, Lc = lay.B, lay.S, lay.Lc
    W = GQA_HQ * GQA_DH
    lat = lambda w: pl.BlockSpec((S, w), lambda b: (lay.lat_seq_block + b, 0))
    ctx = lambda w: pl.BlockSpec((Lc, w), lambda b: (b, 0))
    in_specs = [pl.BlockSpec(memory_space=pltpu.SMEM), lat(W), lat(LANES), lat(LANES), ctx(LANES), ctx(LANES)]
    out_specs = [pl.BlockSpec((S, W), lambda b: (b, 0))]
    out_shape = [jax.ShapeDtypeStruct((lay.n_lat, W), BF)]
    args = [sink, q, k, v, k, v]
    if ctx_queries:
        in_specs.append(ctx(W))
        out_specs.append(ctx(W))
        out_shape.append(jax.ShapeDtypeStruct((lay.n_ctx, W), BF))
        args.append(q)
    return pl.pallas_call(
        functools.partial(_gqa_kernel, S=S, Lc=Lc, ctx_queries=ctx_queries),
        grid=(B,),
        in_specs=in_specs,
        out_specs=out_specs,
        out_shape=out_shape,
        scratch_shapes=[pltpu.VMEM((4, S, LANES), BF), pltpu.VMEM((4, S, LANES), BF),
                        pltpu.VMEM((4, Lc, LANES), BF), pltpu.VMEM((4, Lc, LANES), BF)],
        compiler_params=_params(1),
        name="gqa_window",
    )(*args)


def _mla_heads(q_ref, kv_refs, o_ref):
    lane = lax.broadcasted_iota(jnp.int32, (1, LANES), 1)
    low = lane < MLA_DV
    for pair in range(MLA_H // 2):
        vsl = slice(pair * LANES, (pair + 1) * LANES)
        acc = None
        for half in range(2):
            h = 2 * pair + half
            hsl = slice(h * LANES, (h + 1) * LANES)
            qh = q_ref[:, hsl]
            scores = [_dot_nt(qh, k_ref[:, hsl]) for k_ref, _ in kv_refs]
            m = functools.reduce(jnp.maximum, [jnp.max(s, axis=-1, keepdims=True) for s in scores])
            keep = low if half == 0 else jnp.logical_not(low)
            ones_lane = LANES - 1 if half == 0 else 0
            t = None
            for s, (_, v_ref) in zip(scores, kv_refs):
                vh = jnp.where(keep, v_ref[:, vsl], jnp.zeros((), BF))
                vh = jnp.where(lane == ones_lane, jnp.ones((), BF), vh)
                u = _dot(jnp.exp2(s - m).astype(BF), vh)
                t = u if t is None else t + u
            o = jnp.where(keep, t * (1.0 / t[:, ones_lane:ones_lane + 1]), 0.0)
            acc = o if acc is None else acc + o
        o_ref[:, vsl] = acc.astype(BF)


def _mla_lat_kernel(q_ref, kc_ref, kl_ref, vc_ref, vl_ref, o_ref):
    _mla_heads(q_ref, ((kc_ref, vc_ref), (kl_ref, vl_ref)), o_ref)


def _mla_ctx_kernel(q_ref, kc_ref, vc_ref, o_ref):
    _mla_heads(q_ref, ((kc_ref, vc_ref),), o_ref)


def _mla_latent(q, k, v, lay):
    B, S, Lc = lay.B, lay.S, lay.Lc
    tq = min(512, S)
    nq = S // tq
    lat_q0 = lay.n_ctx // tq
    kw, vw = MLA_H * LANES, MLA_H * MLA_DV
    return pl.pallas_call(
        _mla_lat_kernel,
        grid=(B, nq),
        in_specs=[pl.BlockSpec((tq, kw), lambda b, i: (lat_q0 + b * nq + i, 0)),
                  pl.BlockSpec((Lc, kw), lambda b, i: (b, 0)),
                  pl.BlockSpec((S, kw), lambda b, i: (lay.lat_seq_block + b, 0)),
                  pl.BlockSpec((Lc, vw), lambda b, i: (b, 0)),
                  pl.BlockSpec((S, vw), lambda b, i: (lay.lat_seq_block + b, 0))],
        out_specs=pl.BlockSpec((tq, vw), lambda b, i: (b * nq + i, 0)),
        out_shape=jax.ShapeDtypeStruct((lay.n_lat, vw), BF),
        compiler_params=_params(2),
        name="mla_latent",
    )(q, k, k, v, v)


def _mla_context(q, k, v, lay):
    B, Lc = lay.B, lay.Lc
    kw, vw = MLA_H * LANES, MLA_H * MLA_DV
    blk = lambda w: pl.BlockSpec((Lc, w), lambda b: (b, 0))
    return pl.pallas_call(
        _mla_ctx_kernel,
        grid=(B,),
        in_specs=[blk(kw), blk(kw), blk(vw)],
        out_specs=blk(vw),
        out_shape=jax.ShapeDtypeStruct((lay.n_ctx, vw), BF),
        compiler_params=_params(1),
        name="mla_context",
    )(q, k, v)


def _ret_chunk(Lc, S):
    return 256 if Lc % 256 == 0 and S % 256 == 0 else RET_CHUNK


def _ret_kernel(dec_ref, qc_ref, ql_ref, kc_ref, kl_ref, vc_ref, vl_ref, gc_ref, gl_ref, gn_ref,
                oc_ref, ol_ref, kv_s, r_s, *, Lc, S, ch):
    nc, nl = Lc // ch, S // ch
    n = nc + nl
    lg_f = jnp.log(_sigmoid(dec_ref[0, 0:1, :]))
    lg_b = jnp.log(_sigmoid(dec_ref[0, 1:2, :]))
    pos_i = lax.broadcasted_iota(jnp.int32, (ch, ch), 0).astype(F32)
    pos_j = lax.broadcasted_iota(jnp.int32, (ch, ch), 1).astype(F32)
    diff = pos_i - pos_j
    fwd = diff >= 0.0
    mask = jnp.where(fwd, jnp.exp(jnp.where(fwd, diff, 0.0) * lg_f), jnp.exp(jnp.where(fwd, 0.0, -diff) * lg_b))
    pos = pos_i[:, 0:1]
    lf, lb = lg_f[:, :LANES], lg_b[:, :LANES]
    zeta = (jnp.exp((ch - 1.0 - pos) * lf), jnp.exp(pos * lb))
    xi = (jnp.exp((pos + 1.0) * lf), jnp.exp((ch - pos) * lb))
    decay = (jnp.exp(ch * lf), jnp.exp(ch * lb))

    def rows(c_ref, l_ref, c):
        if c < nc:
            return c_ref[c * ch:(c + 1) * ch, :]
        return l_ref[(c - nc) * ch:(c - nc + 1) * ch, :]

    for c in range(n):
        k = rows(kc_ref, kl_ref, c).astype(F32)
        kz = jnp.concatenate([(k * zeta[0]).astype(BF), (k * zeta[1]).astype(BF)], axis=1)
        kv_s[c] = _dot_tn(kz, rows(vc_ref, vl_ref, c))

    orders = (list(range(n)), list(range(nc - 1, -1, -1)) + list(range(n - 1, nc - 1, -1)))
    for d, order in enumerate(orders):
        R = jnp.zeros((RET_DK, RET_DV), F32)
        for c in order:
            r_s[c, d * RET_DK:(d + 1) * RET_DK, :] = R.astype(BF)
            R = decay[d] * R + kv_s[c, d * RET_DK:(d + 1) * RET_DK, :]

    for c in range(n):
        q = rows(qc_ref, ql_ref, c)
        s = _dot_nt(q, rows(kc_ref, kl_ref, c)) * mask
        qf = q.astype(F32)
        qx = jnp.concatenate([(qf * xi[0]).astype(BF), (qf * xi[1]).astype(BF)], axis=1)
        y = _dot(s.astype(BF), rows(vc_ref, vl_ref, c)) + _dot(qx, r_s[c])
        mu = jnp.mean(y, axis=-1, keepdims=True)
        yc = y - mu
        var = jnp.mean(yc * yc, axis=-1, keepdims=True)
        yn = yc * lax.rsqrt(var + LN_EPS) * gn_ref[...]
        g = rows(gc_ref, gl_ref, c).astype(F32)
        out = (g * _sigmoid(g) * yn).astype(BF)
        if c < nc:
            oc_ref[c * ch:(c + 1) * ch, :] = out
        else:
            ol_ref[(c - nc) * ch:(c - nc + 1) * ch, :] = out


def _retention(dq, dk, dv, dg, decay_lanes, gn, lay):
    B, S, Lc = lay.B, lay.S, lay.Lc
    ch = _ret_chunk(Lc, S)
    n = (Lc + S) // ch
    ctx = pl.BlockSpec((Lc, LANES), lambda b, h: (b, h))
    lat = pl.BlockSpec((S, LANES), lambda b, h: (lay.lat_seq_block + b, h))
    return pl.pallas_call(
        functools.partial(_ret_kernel, Lc=Lc, S=S, ch=ch),
        grid=(B, RET_H),
        in_specs=[pl.BlockSpec((1, 2, ch), lambda b, h: (h, 0, 0)),
                  ctx, lat, ctx, lat, ctx, lat, ctx, lat, pl.BlockSpec((1, LANES), lambda b, h: (0, h))],
        out_specs=[pl.BlockSpec((Lc, LANES), lambda b, h: (b, h)), pl.BlockSpec((S, LANES), lambda b, h: (b, h))],
        out_shape=[jax.ShapeDtypeStruct((lay.n_ctx, RET_H * RET_DV), BF), jax.ShapeDtypeStruct((lay.n_lat, RET_H * RET_DV), BF)],
        scratch_shapes=[pltpu.VMEM((n, 2 * RET_DK, RET_DV), F32), pltpu.VMEM((n, 2 * RET_DK, RET_DV), BF)],
        compiler_params=_params(2),
        name="retention",
    )(decay_lanes, dq, dq, dk, dk, dv, dv, dg, dg, gn)


def _merge_kernel(*refs, n_br, ct):
    x_ref, mod_ref = refs[:2]
    br_refs = refs[2:2 + N_BRANCH * n_br]
    wg_ref, wb_ref, wo_ref, g_ref, beta_ref, o_ref = refs[2 + N_BRANCH * n_br:]
    x = x_ref[...]
    h = (x * (1.0 + mod_ref[0, 4:5, :]) + mod_ref[0, 3:4, :]).astype(BF)
    acc = None
    for k in range(N_BRANCH):
        br = _rows(br_refs[k * n_br:(k + 1) * n_br], pl.program_id(0), ct)
        gate = 0.5 * jnp.tanh(_dot(h, wg_ref[:, k * D_MODEL:(k + 1) * D_MODEL])) + 0.5
        t = gate * _dot(br, wb_ref[k])
        acc = t if acc is None else acc + t
    y = _dot(acc.astype(BF), wo_ref[...])
    z = DN_ALPHA * x + mod_ref[0, 5:6, :] * y
    o_ref[...] = _layer_norm(z, g_ref[...], beta_ref[...])


def _merge(x, mod, branches, w_gate, wb_all, wo_all, l, g, beta, lay, g_off):
    tm = lay.tm
    n = lay.tiles - g_off
    br_specs, br_args = [], []
    for br in branches:
        specs, args = _row_specs(br, BRANCH_W, lay, g_off)
        br_specs += specs
        br_args += args
    return pl.pallas_call(
        functools.partial(_merge_kernel, n_br=len(br_args) // N_BRANCH, ct=lay.ctx_tiles),
        grid=(n,),
        in_specs=[pl.BlockSpec((tm, D_MODEL), lambda i: (i + g_off, 0)),
                  pl.BlockSpec((1, 9, D_MODEL), lambda i: (lay.mod_index(i + g_off), 0, 0))] + br_specs + [
                  pl.BlockSpec((None,) + w_gate.shape[1:], lambda i: (l, 0, 0), pipeline_mode=pl.Buffered(1)),
                  pl.BlockSpec((None, N_BRANCH, BRANCH_W, D_MODEL), lambda i: (l, 0, 0, 0), pipeline_mode=pl.Buffered(1)),
                  pl.BlockSpec((None, D_MODEL, D_MODEL), lambda i: (l, 0, 0), pipeline_mode=pl.Buffered(1)),
                  _const((1, D_MODEL)), _const((1, D_MODEL))],
        out_specs=pl.BlockSpec((tm, D_MODEL), lambda i: (i, 0)),
        out_shape=jax.ShapeDtypeStruct((n * tm, D_MODEL), F32),
        compiler_params=_params(1),
        name="gated_merge",
    )(x, mod, *br_args, w_gate, wb_all, wo_all, g, beta)


def _rope_tables(S, head_dim, lead, reps, tail, tm):
    nf = head_dim // 4
    inv = ROPE_BASE ** (-np.arange(nf, dtype=np.float64) / nf)
    n_rows = S // GRID_W
    ar = np.arange(n_rows, dtype=np.float64)[:, None] * inv[None, :]
    ac = np.arange(GRID_W, dtype=np.float64)[:, None] * inv[None, :]

    def lanes(by_row, first, second, fill):
        n = first.shape[0]
        neutral = np.full((n, nf), fill)
        head = np.concatenate([first, second, neutral, neutral] if by_row else [neutral, neutral, first, second], -1)
        pad = lambda w: np.full((n, w), fill)
        return np.tile(np.concatenate([pad(lead), head, pad(tail)], -1), (1, reps))

    row_c = lanes(True, np.cos(ar), np.cos(ar), 1.0)
    row_s = lanes(True, -np.sin(ar), np.sin(ar), 0.0)
    col_c = lanes(False, np.cos(ac), np.cos(ac), 1.0)
    col_s = lanes(False, -np.sin(ac), np.sin(ac), 0.0)
    W = row_c.shape[1]
    per_tile = tm // GRID_W

    def by_tile(tab, fill):
        t = tab.reshape(S // tm, per_tile, W)
        t = np.concatenate([t, np.full((S // tm, SUBLANES - per_tile, W), fill)], 1).reshape(-1, W)
        return np.concatenate([t, np.full((SUBLANES, W), fill)], 0).astype(np.float32)

    ident = lambda fill: np.full((GRID_W, W), fill)
    return (by_tile(row_c, 1.0), by_tile(row_s, 0.0),
            np.stack([col_c, ident(1.0)]).astype(np.float32), np.stack([col_s, ident(0.0)]).astype(np.float32))


def _mixer_weights(w_in, lru_w_a, lru_b_a, lru_w_x, lru_b_x, mla_w_uq, mla_w_ukv):
    wb = w_in.astype(BF)
    cut = sum(IN_SIZES[:8])
    gate0 = sum(IN_SIZES[:12])
    assert cut + LANES - MLA_DR == _W_HEAD
    w_proj = (wb, wb[:, :, cut:gate0])
    w_gate = wb[:, :, gate0:] * jnp.asarray(0.5, BF)
    blocks = jnp.stack([lru_w_a[:, 0], lru_w_x[:, 0], lru_w_a[:, 1], lru_w_x[:, 1]], axis=1)
    eye = jnp.eye(LRU_BLOCKS, dtype=F32)
    dense = (0.5 * eye)[None, None, :, None, :, None] * blocks[:, :, :, :, None, :]
    wg = dense.reshape(DEPTH, 4, LRU_W, LRU_W).transpose(0, 2, 1, 3).reshape(DEPTH, LRU_W, 4 * LRU_W).astype(BF)
    bg = 0.5 * jnp.stack([lru_b_a[:, 0], lru_b_x[:, 0], lru_b_a[:, 1], lru_b_x[:, 1]], axis=1).reshape(DEPTH, 1, 4 * LRU_W)
    uq = mla_w_uq.reshape(DEPTH, MLA_RQ, MLA_H, MLA_DN + MLA_DR)
    wq = jnp.pad(uq, ((0, 0), (0, 0), (0, 0), (0, LANES - MLA_DN - MLA_DR))).reshape(DEPTH, MLA_RQ, MLA_H * LANES).astype(BF)
    ukv = mla_w_ukv.reshape(DEPTH, MLA_RKV, MLA_H, MLA_DN + MLA_DV)
    wk = jnp.pad(ukv[..., :MLA_DN], ((0, 0), (0, 0), (0, 0), (0, LANES - MLA_DN))).reshape(DEPTH, MLA_RKV, MLA_H * LANES).astype(BF)
    wv = ukv[..., MLA_DN:].reshape(DEPTH, MLA_RKV, MLA_H * MLA_DV).astype(BF)
    return w_proj, w_gate, wg, bg, wq, wk, wv


def kernel(x, c, ctx, c_ctx, w_mod, b_mod, ln_g, ln_b, ffn_w_in, ffn_w_out, w_in, lru_conv_w, lru_conv_b, lru_w_a, lru_b_a, lru_w_x, lru_b_x, lru_lambda, gqa_sink, mla_q_norm, mla_kv_norm, mla_w_uq, mla_w_ukv, ret_decay, ret_gn_g, w_branch, w_out):
    B, S, D = x.shape
    Lc = ctx.shape[1]
    lay = _Layout(B, S, Lc)
    tm = lay.tm

    n_cond = -(-(B + 1) // SUBLANES) * SUBLANES
    cond = jnp.concatenate([c, c_ctx[None, :], jnp.zeros((n_cond - B - 1, D), F32)], axis=0)
    mod_all = _modulation(cond, w_mod, b_mod)[:, :B + 1].reshape(DEPTH, B + 1, 9, D)

    tabs = [_rope_tables(S, head_dim, lead, reps, tail, tm) for head_dim, lead, reps, tail in
            ((GQA_DH, 0, GQA_HQ, 0), (GQA_DH, 0, GQA_HKV, 0), (MLA_DR, 0, 1, LANES - MLA_DR), (RET_DK, 0, RET_H, 0),
             (MLA_DR, MLA_DN, MLA_H, LANES - MLA_DN - MLA_DR))]

    tok = (ctx.reshape(B * Lc, D), x.reshape(B * S, D))
    row = lambda v: v[None, :]
    ct = lay.ctx_tiles
    w1_all, w2_all = ffn_w_in.astype(BF), ffn_w_out.astype(BF)
    wb_all, wo_all = w_branch.astype(BF), w_out.astype(BF)
    w_proj, w_gate, wg, bg, wq, wk, wv = _mixer_weights(w_in, lru_w_a, lru_b_a, lru_w_x, lru_b_x, mla_w_uq, mla_w_ukv)

    for l in range(DEPTH):
        last = l == DEPTH - 1
        mod = mod_all[l]
        dec = jnp.broadcast_to(ret_decay[l].T[:, :, None], (RET_H, 2, _ret_chunk(Lc, S)))

        tok = _half_ffn(tok, mod, w1_all, w2_all, l, 0, row(ln_g[l, 0]), row(ln_b[l, 0]), lay, 0, 0)

        ax, ay, bq, bk, bv, mq, mk, mv, dq, dk, dv, dg = _mixer_proj(
            tok, mod, w_proj, l, wq[l], wk[l], wv[l], row(mla_q_norm[l]), row(mla_kv_norm[l]), tabs, lay)
        a_c, a_l = _lru(ax, ay, lru_conv_w[l], row(lru_conv_b[l]), wg[l], bg[l], lru_lambda[l], lay)
        b_out = _gqa(bq, bk, bv, gqa_sink[l], lay, ctx_queries=not last)
        c_l = _mla_latent(mq, mk, mv, lay)
        d_c, d_l = _retention(dq, dk, dv, dg, dec, row(ret_gn_g[l]), lay)

        if last:
            branches = (a_l, b_out[0], c_l, d_l)
            g_off = ct
        else:
            branches = ((a_c, a_l), (b_out[1], b_out[0]), (_mla_context(mq, mk, mv, lay), c_l), (d_c, d_l))
            g_off = 0
        tok = _merge(tok, mod, branches, w_gate, wb_all, wo_all, l, row(ln_g[l, 1]), row(ln_b[l, 1]), lay, g_off)
        tok = _half_ffn(tok, mod, w1_all, w2_all, l, 1, row(ln_g[l, 2]), row(ln_b[l, 2]), lay, 6, g_off)

    return tok.reshape(B, S, D)
```

```python
import functools

import jax
import jax.numpy as jnp
import numpy as np
from jax import lax
from jax.experimental import pallas as pl
from jax.experimental.pallas import tpu as pltpu

D_MODEL = 1024
DEPTH = 2
GRID_W = 64
N_BRANCH = 4
BRANCH_W = 512
LRU_W = 512
LRU_BLOCKS = 8
LRU_BW = LRU_W // LRU_BLOCKS
LRU_C = 8.0
GQA_HQ = 8
GQA_HKV = 2
GQA_DH = 64
WINDOW = 128
ATT_BLOCK = 128
MLA_H = 8
MLA_RQ = 384
MLA_RKV = 256
MLA_DN = 64
MLA_DR = 32
MLA_DV = 64
RET_H = 4
RET_DK = 128
RET_DV = 128
RET_CHUNK = 128
FF = 2816
ROPE_BASE = 10000.0
LN_EPS = 1e-5
MASK_VALUE = -1e30
DN_ALPHA = (2 * DEPTH) ** 0.25
IN_SIZES = (LRU_W, LRU_W, GQA_HQ * GQA_DH, GQA_HKV * GQA_DH, GQA_HKV * GQA_DH, MLA_RQ, MLA_RKV, MLA_DR,
            RET_H * RET_DK, RET_H * RET_DK, RET_H * RET_DV, RET_H * RET_DV, N_BRANCH * D_MODEL)

LANES = 128
SUBLANES = 8
VMEM_LIMIT = 56 * 1024 * 1024
MXU_DIM = 256
FFN_CHUNKS = (6 * MXU_DIM, 5 * MXU_DIM)
LOG2E = 1.4426950408889634
BF = jnp.bfloat16
F32 = jnp.float32


def _params(n_axes):
    return pltpu.CompilerParams(dimension_semantics=("arbitrary",) * n_axes, vmem_limit_bytes=VMEM_LIMIT)


def _resident(shape):
    nd = len(shape)
    return pl.BlockSpec(shape, lambda *_: (0,) * nd, pipeline_mode=pl.Buffered(1))


def _const(shape):
    nd = len(shape)
    return pl.BlockSpec(shape, lambda *_: (0,) * nd)


def _layer_norm(z, g, b):
    mu = jnp.mean(z, axis=-1, keepdims=True)
    zc = z - mu
    var = jnp.mean(zc * zc, axis=-1, keepdims=True)
    return zc * lax.rsqrt(var + LN_EPS) * g + b


def _sigmoid(x):
    return 1.0 / (1.0 + jnp.exp(-x))


def _dot(a, b):
    return jnp.dot(a, b, preferred_element_type=F32)


def _dot_nt(a, b):
    return lax.dot_general(a, b, (((1,), (1,)), ((), ())), preferred_element_type=F32)


def _dot_tn(a, b):
    return lax.dot_general(a, b, (((0,), (0,)), ((), ())), preferred_element_type=F32)


def _rope(y, c, s, nf):
    w = y.shape[-1]
    lane = lax.broadcasted_iota(jnp.int32, y.shape, 1)
    partner = jnp.where((lane % (2 * nf)) < nf, pltpu.roll(y, w - nf, axis=1), pltpu.roll(y, nf, axis=1))
    return y * c + partner * s


class _Layout:
    def __init__(self, B, S, Lc):
        self.B, self.S, self.Lc = B, S, Lc
        self.n_ctx, self.n_lat = B * Lc, B * S
        self.n_tok = self.n_ctx + self.n_lat
        tm = 512
        while S % tm or self.n_ctx % tm:
            tm //= 2
        self.tm = tm
        self.ctx_tiles = self.n_ctx // tm
        self.tiles_per_seq = S // tm
        self.tiles = self.ctx_tiles + B * self.tiles_per_seq
        assert self.n_ctx % S == 0, "latent rows must start on a whole-sequence block"
        self.lat_seq_block = self.n_ctx // S
        self.lat_att_block = self.n_ctx // ATT_BLOCK

    def mod_index(self, g):
        return jnp.where(g < self.ctx_tiles, self.B, (g - self.ctx_tiles) // self.tiles_per_seq)

    def rope_index(self, g):
        return jnp.where(g < self.ctx_tiles, self.tiles_per_seq, (g - self.ctx_tiles) % self.tiles_per_seq)


def _mod_kernel(s_ref, w_ref, b_ref, o_ref):
    s = s_ref[...]
    s = s * _sigmoid(s)
    o_ref[0] = _dot(s.astype(BF), w_ref[0].astype(BF)) + b_ref[0]


def _modulation(cond, w_mod, b_mod):
    R = cond.shape[0]
    tn = 1024
    n9 = 9 * D_MODEL
    return pl.pallas_call(
        _mod_kernel,
        grid=(DEPTH, n9 // tn),
        in_specs=[pl.BlockSpec((R, D_MODEL), lambda l, j: (0, 0)),
                  pl.BlockSpec((1, D_MODEL, tn), lambda l, j: (l, 0, j)),
                  pl.BlockSpec((1, 1, tn), lambda l, j: (l, 0, j))],
        out_specs=pl.BlockSpec((1, R, tn), lambda l, j: (l, 0, j)),
        out_shape=jax.ShapeDtypeStruct((DEPTH, R, n9), F32),
        compiler_params=_params(2),
        name="modulation",
    )(cond, w_mod, b_mod.reshape(DEPTH, 1, n9))


def _row_specs(src, width, lay, g_off, tile=lambda i: i):
    tm, ct = lay.tm, lay.ctx_tiles
    if isinstance(src, tuple):
        assert g_off == 0
        return [pl.BlockSpec((tm, width), lambda i: (jnp.minimum(tile(i), ct - 1), 0)),
                pl.BlockSpec((tm, width), lambda i: (jnp.maximum(tile(i) - ct, 0), 0))], list(src)
    assert src.shape[0] == lay.n_tok or (src.shape[0] == lay.n_lat and g_off == ct)
    off = g_off if src.shape[0] == lay.n_tok else 0
    return [pl.BlockSpec((tm, width), lambda i: (tile(i) + off, 0))], [src]


def _rows(refs, t, ct):
    if len(refs) == 2:
        return jnp.where(t < ct, refs[0][...], refs[1][...])
    return refs[0][...]


def _ffn_kernel(*refs, k0, n_x, ct):
    mod_ref, w1_ref, w2_ref, g_ref, b_ref, o_ref = refs[n_x:]
    x = _rows(refs[:n_x], pl.program_id(0), ct)
    shift = mod_ref[0, k0:k0 + 1, :]
    scale = mod_ref[0, k0 + 1:k0 + 2, :]
    gate = mod_ref[0, k0 + 2:k0 + 3, :]
    xm = (x * (1.0 + scale) + shift).astype(BF)
    acc = jnp.zeros(x.shape, F32)
    assert sum(FFN_CHUNKS) == FF
    lo = 0
    for width in FFN_CHUNKS:
        a = _dot(xm, w1_ref[:, lo:lo + width])
        b = _dot(xm, w1_ref[:, FF + lo:FF + lo + width])
        h = (a * _sigmoid(a) * b).astype(BF)
        acc = acc + _dot(h, w2_ref[lo:lo + width, :])
        lo += width
    z = DN_ALPHA * x + (0.5 * gate) * acc
    o_ref[...] = _layer_norm(z, g_ref[...], b_ref[...])


def _half_ffn(x, mod, w1_all, w2_all, l, j, g, b, lay, k0, g_off):
    tm = lay.tm
    n = lay.tiles - g_off
    x_specs, x_args = _row_specs(x, D_MODEL, lay, g_off)
    pick = lambda *_: (l, j, 0, 0)
    return pl.pallas_call(
        functools.partial(_ffn_kernel, k0=k0, n_x=len(x_args), ct=lay.ctx_tiles),
        grid=(n,),
        in_specs=x_specs + [
            pl.BlockSpec((1, 9, D_MODEL), lambda i: (lay.mod_index(i + g_off), 0, 0)),
            pl.BlockSpec((None, None, D_MODEL, 2 * FF), pick, pipeline_mode=pl.Buffered(1)),
            pl.BlockSpec((None, None, FF, D_MODEL), pick, pipeline_mode=pl.Buffered(1)),
            _const((1, D_MODEL)), _const((1, D_MODEL))],
        out_specs=pl.BlockSpec((tm, D_MODEL), lambda i: (i, 0)),
        out_shape=jax.ShapeDtypeStruct((n * tm, D_MODEL), F32),
        compiler_params=_params(1),
        name="half_ffn",
    )(*x_args, mod, w1_all, w2_all, g, b)


_W_LRU, _W_GQ, _W_GKV, _W_RET = LRU_W, GQA_HQ * GQA_DH, GQA_HKV * GQA_DH, RET_H * RET_DK
_OFF_AX, _OFF_AY, _OFF_BQ, _OFF_BKV = 0, 512, 1024, 1536
_OFF_MLA, _W_MLA = 1792, MLA_RQ + MLA_RKV + LANES
_W_HEAD = _OFF_MLA + _W_MLA
_OFF_DQ, _OFF_DK, _OFF_DV, _OFF_DG = 0, 512, 1024, 1536
_PROJ_OUT_WIDTHS = (_W_LRU, _W_LRU, _W_GQ, _W_GKV, _W_GKV, MLA_H * LANES, MLA_H * LANES, MLA_H * MLA_DV,
                    _W_RET, _W_RET, _W_RET, _W_RET)
_N_ROPE_TABLES = 5


def _rope_tile(y, tab, nf):
    rc_ref, rs_ref, cc_ref, cs_ref = tab
    parts = []
    for g in range(y.shape[0] // GRID_W):
        c = rc_ref[g:g + 1, :] * cc_ref[...]
        s = rs_ref[g:g + 1, :] + cs_ref[...]
        parts.append(_rope(y[g * GRID_W:(g + 1) * GRID_W, :], c, s, nf))
    return jnp.concatenate(parts, axis=0)


def _rms(x, g):
    return x * lax.rsqrt(jnp.mean(x * x, axis=-1, keepdims=True) + LN_EPS) * g


def _proj_kernel(x_ref, mod_ref, w_ref, wt_ref, wq_ref, wk_ref, wv_ref, gq_ref, gkv_ref, *refs):
    tabs = [refs[4 * i:4 * i + 4] for i in range(_N_ROPE_TABLES)]
    t_gq, t_gk, t_kr, t_ret, t_mq = tabs
    ax_o, ay_o, bq_o, bk_o, bv_o, mq_o, mk_o, mv_o, dq_o, dk_o, dv_o, dg_o = refs[4 * _N_ROPE_TABLES:]
    x = x_ref[...]
    h = (x * (1.0 + mod_ref[0, 4:5, :]) + mod_ref[0, 3:4, :]).astype(BF)
    proj = lambda lo, width: _dot(h, w_ref[:, lo:lo + width])
    tail = lambda lo, width: _dot(h, wt_ref[:, lo:lo + width])

    ax_o[...] = proj(_OFF_AX, _W_LRU).astype(BF)
    ay_o[...] = proj(_OFF_AY, _W_LRU).astype(BF)
    bq_o[...] = (_rope_tile(proj(_OFF_BQ, _W_GQ), t_gq, GQA_DH // 4) * (GQA_DH ** -0.5 * LOG2E)).astype(BF)
    kv = proj(_OFF_BKV, 2 * _W_GKV)
    bk_o[...] = _rope_tile(kv[:, :_W_GKV], t_gk, GQA_DH // 4).astype(BF)
    bv_o[...] = kv[:, _W_GKV:].astype(BF)

    c = proj(_OFF_MLA, _W_MLA)
    yq = _rms(c[:, :MLA_RQ], gq_ref[...]).astype(BF)
    mq_o[...] = (_rope_tile(_dot(yq, wq_ref[...]), t_mq, MLA_DR // 4) * ((MLA_DN + MLA_DR) ** -0.5 * LOG2E)).astype(BF)
    ykv = _rms(c[:, MLA_RQ:MLA_RQ + MLA_RKV], gkv_ref[...]).astype(BF)
    kr = c[:, MLA_RQ + MLA_RKV:]
    kr = jnp.where(lax.broadcasted_iota(jnp.int32, kr.shape, 1) < MLA_DR, kr, 0.0)
    kr = _rope_tile(kr, t_kr, MLA_DR // 4)
    kr_all = pltpu.roll(jnp.concatenate([kr] * MLA_H, axis=1), MLA_DN, axis=1)
    mk_o[...] = (_dot(ykv, wk_ref[...]) + kr_all).astype(BF)
    mv_o[...] = _dot(ykv, wv_ref[...]).astype(BF)

    dq_o[...] = _rope_tile(tail(_OFF_DQ, _W_RET), t_ret, RET_DK // 4).astype(BF)
    dk_o[...] = (_rope_tile(tail(_OFF_DK, _W_RET), t_ret, RET_DK // 4) * RET_DK ** -0.5).astype(BF)
    dv_o[...] = tail(_OFF_DV, _W_RET).astype(BF)
    dg_o[...] = tail(_OFF_DG, _W_RET).astype(BF)


def _rope_specs(tab, lay):
    w = tab[0].shape[1]
    row = pl.BlockSpec((SUBLANES, w), lambda i: (lay.rope_index(i), 0))
    col = pl.BlockSpec((None, GRID_W, w), lambda i: (jnp.where(i < lay.ctx_tiles, 1, 0), 0, 0))
    return [row, row, col, col]


def _mixer_proj(x, mod, w_all, l, wq, wk, wv, gq, gkv, tabs, lay):
    tm = lay.tm
    row = lambda i: (i, 0)
    w_head, w_tail = w_all
    assert len(tabs) == _N_ROPE_TABLES
    tab_specs, tab_args = [], []
    for tab in tabs:
        tab_specs += _rope_specs(tab, lay)
        tab_args += list(tab)
    return pl.pallas_call(
        _proj_kernel,
        grid=(lay.tiles,),
        in_specs=[pl.BlockSpec((tm, D_MODEL), row),
                  pl.BlockSpec((1, 9, D_MODEL), lambda i: (lay.mod_index(i), 0, 0)),
                  pl.BlockSpec((None, D_MODEL, _W_HEAD), lambda i: (l, 0, 0), pipeline_mode=pl.Buffered(1)),
                  pl.BlockSpec((None,) + w_tail.shape[1:], lambda i: (l, 0, 0), pipeline_mode=pl.Buffered(1)),
                  _const(wq.shape), _const(wk.shape), _const(wv.shape), _const(gq.shape), _const(gkv.shape)] + tab_specs,
        out_specs=[pl.BlockSpec((tm, w), row) for w in _PROJ_OUT_WIDTHS],
        out_shape=[jax.ShapeDtypeStruct((lay.n_tok, w), BF) for w in _PROJ_OUT_WIDTHS],
        compiler_params=_params(1),
        name="mixer_proj",
    )(x, mod, w_head, w_tail, wq, wk, wv, gq, gkv, *tab_args)


def _conv4(xp_s, base, n, w_ref, b_ref):
    y = b_ref[...] + w_ref[2:3, :] * xp_s[base:base + n, :]
    y = y + w_ref[0:1, :] * xp_s[base - 2:base - 2 + n, :]
    y = y + w_ref[1:2, :] * xp_s[base - 1:base - 1 + n, :]
    y = y + w_ref[3:4, :] * xp_s[base + 1:base + 1 + n, :]
    return y


def _scan8(a, b, reverse):
    row = lax.broadcasted_iota(jnp.int32, a.shape, 0)
    for s in (1, 2, 4):
        if reverse:
            keep = row < SUBLANES - s
            a_sh = pltpu.roll(a, SUBLANES - s, axis=0)
            b_sh = pltpu.roll(b, SUBLANES - s, axis=0)
        else:
            keep = row >= s
            a_sh = pltpu.roll(a, s, axis=0)
            b_sh = pltpu.roll(b, s, axis=0)
        b = jnp.where(keep, b + a * b_sh, b)
        a = jnp.where(keep, a * a_sh, a)
    return a, b


def _lru_segment(T):
    for seg in (36, 44, 28, 20, 12, 52, 60, 4, 8, 16, 32):
        if T % (SUBLANES * seg) == 0:
            return seg
    raise ValueError(f"no scan segment length for {T} rows")


def _lru_kernel(axc_ref, axl_ref, ayc_ref, ayl_ref, cw_ref, cb_ref, wg_ref, bg_ref, lam_ref,
                oc_ref, ol_ref, xp_s, xa_s, a_s, b_s, h0_s, h1_s, ac_s, bc_s, *, Lc, S, rows, seg):
    W = LRU_W
    T = Lc + S
    NS = W // LANES
    G = SUBLANES * seg
    P = SUBLANES
    zpad = jnp.zeros((P, W), F32)
    xp_s[0:P, :] = zpad
    xp_s[P:P + Lc, :] = axc_ref[...].astype(F32)
    xp_s[P + Lc:2 * P + Lc, :] = zpad
    xp_s[2 * P + Lc:2 * P + T, :] = axl_ref[...].astype(F32)
    xp_s[2 * P + T:3 * P + T, :] = zpad
    xa_s[0:Lc, :] = _conv4(xp_s, P, Lc, cw_ref, cb_ref)
    xa_s[Lc:T, :] = _conv4(xp_s, 2 * P + Lc, S, cw_ref, cb_ref)
    row8 = lax.broadcasted_iota(jnp.int32, (SUBLANES, LANES), 0)
    zero8 = jnp.zeros((SUBLANES, LANES), F32)

    for d in range(2):
        reverse = d == 1
        h_s = h1_s if reverse else h0_s
        neg = -lam_ref[d:d + 1, :]
        softplus = jnp.maximum(neg, 0.0) + jnp.log(1.0 + jnp.exp(-jnp.abs(neg)))
        c = (-0.5 * LRU_C * 1.4426950408889634) * softplus

        def gates(i, carry):
            r0 = pl.multiple_of(i * rows, rows)
            dst = pl.multiple_of(jnp.where(r0 < Lc, r0 + S, r0 - Lc), rows) if reverse else r0
            xa = xa_s[pl.ds(r0, rows), :]
            xb = xa.astype(BF)
            lo = 2 * d * W
            t_r = jnp.tanh(_dot(xb, wg_ref[:, lo:lo + W]) + bg_ref[:, lo:lo + W])
            t_i = jnp.tanh(_dot(xb, wg_ref[:, lo + W:lo + 2 * W]) + bg_ref[:, lo + W:lo + 2 * W])
            a = jnp.exp2(c * t_r + c)
            y = 1.0 - a * a
            root = jnp.where(y > 0.0, y * lax.rsqrt(y), 0.0)
            b = (root * (0.5 * xa)) * (t_i + 1.0)
            for k in range(NS):
                a_s[k, pl.ds(dst, rows), :] = a[:, k * LANES:(k + 1) * LANES]
                b_s[k, pl.ds(dst, rows), :] = b[:, k * LANES:(k + 1) * LANES]
            return carry

        lax.fori_loop(0, T // rows, gates, 0, unroll=3 if (T // rows) % 3 == 0 else 1)

        def group(j, h_in):
            base = ((T // G - 1 - j) if reverse else j) * G
            acc_a, acc_b = [None] * NS, [None] * NS
            for i in (range(seg - 1, -1, -1) if reverse else range(seg)):
                for k in range(NS):
                    a = a_s[k, pl.ds(base + i, SUBLANES, stride=seg), :]
                    b = b_s[k, pl.ds(base + i, SUBLANES, stride=seg), :]
                    if acc_a[k] is None:
                        acc_a[k], acc_b[k] = a, b
                    else:
                        acc_b[k] = a * acc_b[k] + b
                        acc_a[k] = a * acc_a[k]
                    ac_s[k, i] = acc_a[k]
                    bc_s[k, i] = acc_b[k]
            h_out, enter = [], []
            for k in range(NS):
                tot_a, tot_b = _scan8(acc_a[k], acc_b[k], reverse)
                after = tot_a * h_in[k] + tot_b
                if reverse:
                    enter.append(jnp.where(row8 == SUBLANES - 1, h_in[k], pltpu.roll(after, SUBLANES - 1, axis=0)))
                    h_out.append(jnp.broadcast_to(after[0:1, :], after.shape))
                else:
                    enter.append(jnp.where(row8 == 0, h_in[k], pltpu.roll(after, 1, axis=0)))
                    h_out.append(jnp.broadcast_to(after[SUBLANES - 1:SUBLANES, :], after.shape))
            for i in range(seg):
                for k in range(NS):
                    h_s[k, pl.ds(base + i, SUBLANES, stride=seg), :] = ac_s[k, i] * enter[k] + bc_s[k, i]
            return tuple(h_out)

        lax.fori_loop(0, T // G, group, (zero8,) * NS)

    def gelu(v):
        c = 0.7978845608028654
        hv = 0.5 * v
        return hv + hv * jnp.tanh(v * (c + (c * 0.044715) * (v * v)))

    for k in range(NS):
        sl = slice(k * LANES, (k + 1) * LANES)
        hc = h0_s[k, 0:Lc, :] + h1_s[k, S:T, :]
        hl = h0_s[k, Lc:T, :] + h1_s[k, 0:S, :]
        oc_ref[:, sl] = (hc * gelu(ayc_ref[:, sl].astype(F32))).astype(BF)
        ol_ref[:, sl] = (hl * gelu(ayl_ref[:, sl].astype(F32))).astype(BF)


def _lru(ax, ay, conv_w, conv_b, wg, bg, lam, lay):
    B, S, Lc = lay.B, lay.S, lay.Lc
    W = LRU_W
    T = Lc + S
    rows = 256
    while Lc % rows or S % rows:
        rows //= 2
    seg = _lru_segment(T)
    ctx = pl.BlockSpec((Lc, W), lambda b: (b, 0))
    lat = pl.BlockSpec((S, W), lambda b: (b + lay.lat_seq_block, 0))
    slabs = pltpu.VMEM((W // LANES, T, LANES), F32)
    part = pltpu.VMEM((W // LANES, seg, SUBLANES, LANES), F32)
    return pl.pallas_call(
        functools.partial(_lru_kernel, Lc=Lc, S=S, rows=rows, seg=seg),
        grid=(B,),
        in_specs=[ctx, lat, ctx, lat, _const((4, W)), _const((1, W)), _const(wg.shape), _const((1, 4 * W)), _const((2, W))],
        out_specs=[pl.BlockSpec((Lc, W), lambda b: (b, 0)), pl.BlockSpec((S, W), lambda b: (b, 0))],
        out_shape=[jax.ShapeDtypeStruct((lay.n_ctx, W), BF), jax.ShapeDtypeStruct((lay.n_lat, W), BF)],
        scratch_shapes=[pltpu.VMEM((T + 3 * SUBLANES, W), F32), pltpu.VMEM((T, W), F32), slabs, slabs, slabs, slabs, part, part],
        compiler_params=_params(1),
        name="rglru",
    )(ax, ax, ay, ay, conv_w, conv_b, wg, bg, lam)


def _split_heads_pair(x, ones_lane):
    xf = x.astype(F32)
    sw = pltpu.roll(xf, GQA_DH, axis=1)
    lane = lax.broadcasted_iota(jnp.int32, xf.shape, 1)
    low = lane < GQA_DH
    lo_fill = jnp.where(lane == LANES - 1, 1.0, 0.0) if ones_lane else jnp.zeros_like(xf)
    hi_fill = jnp.where(lane == 0, 1.0, 0.0) if ones_lane else jnp.zeros_like(xf)
    head0 = (jnp.where(low, xf, lo_fill).astype(BF), jnp.where(low, hi_fill, sw).astype(BF))
    head1 = (jnp.where(low, sw, lo_fill).astype(BF), jnp.where(low, hi_fill, xf).astype(BF))
    return head0, head1


def _gqa_kernel(sink_ref, *refs, S, Lc, ctx_queries):
    if ctx_queries:
        q_ref, k_ref, v_ref, kx_ref, vx_ref, qx_ref, o_ref, ox_ref, ks_s, vs_s, kxs_s, vxs_s = refs
    else:
        q_ref, k_ref, v_ref, kx_ref, vx_ref, o_ref, ks_s, vs_s, kxs_s, vxs_s = refs
    blk = ATT_BLOCK
    nb = S // blk
    win = min(3 * blk, S)
    for src, dst, is_value in ((k_ref, ks_s, False), (v_ref, vs_s, True), (kx_ref, kxs_s, False), (vx_ref, vxs_s, True)):
        heads = _split_heads_pair(src[...], is_value)
        for j in range(GQA_HKV):
            for half in range(2):
                dst[2 * j + half] = heads[j][half]

    row = lax.broadcasted_iota(jnp.int32, (2 * blk, win), 0)
    col = lax.broadcasted_iota(jnp.int32, (2 * blk, win), 1)
    rel = (row % blk) - col
    top = row[:, 0:1] < blk
    low = lax.broadcasted_iota(jnp.int32, (1, LANES), 1) < GQA_DH

    def attend(q, kstart, valid):
        outs = []
        for j in range(GQA_HKV):
            q2 = jnp.concatenate([q[:, 2 * j * LANES:(2 * j + 1) * LANES], q[:, (2 * j + 1) * LANES:(2 * j + 2) * LANES]], axis=0)
            acc = jnp.zeros((2 * blk, LANES), F32)
            for half in range(2):
                i = 2 * j + half
                sink = jnp.where(top, sink_ref[4 * j + half] * LOG2E, sink_ref[4 * j + 2 + half] * LOG2E)
                s_c = _dot_nt(q2, kxs_s[i])
                m = jnp.maximum(jnp.max(s_c, axis=-1, keepdims=True), sink)
                if kstart is not None:
                    s_w = jnp.where(valid, _dot_nt(q2, ks_s[i, pl.ds(kstart, win), :]), MASK_VALUE)
                    m = jnp.maximum(m, jnp.max(s_w, axis=-1, keepdims=True))
                t = _dot(jnp.exp2(s_c - m).astype(BF), vxs_s[i])
                if kstart is not None:
                    t = t + _dot(jnp.exp2(s_w - m).astype(BF), vs_s[i, pl.ds(kstart, win), :])
                ones_lane = LANES - 1 if half == 0 else 0
                den = t[:, ones_lane:ones_lane + 1] + jnp.exp2(sink - m)
                acc = acc + jnp.where(low if half == 0 else jnp.logical_not(low), t * (1.0 / den), 0.0)
            outs += [acc[0:blk].astype(BF), acc[blk:2 * blk].astype(BF)]
        return jnp.concatenate(outs, axis=1)

    def block(n, carry):
        q0 = pl.multiple_of(n * blk, blk)
        kstart = pl.multiple_of(jnp.clip((n - 1) * blk, 0, S - win), blk)
        dist = rel + (q0 - kstart)
        valid = (dist <= WINDOW) & (dist >= -WINDOW)
        o_ref[pl.ds(q0, blk), :] = attend(q_ref[pl.ds(q0, blk), :], kstart, valid)
        return carry

    lax.fori_loop(0, nb, block, 0, unroll=2)
    if ctx_queries:
        for n in range(Lc // blk):
            ox_ref[n * blk:(n + 1) * blk, :] = attend(qx_ref[n * blk:(n + 1) * blk, :], None, None)


def _gqa(q, k, v, sink, lay, ctx_queries):
    B, S, Lc = lay.B, lay.S, lay.Lc
    W = GQA_HQ * GQA_DH
    lat = lambda w: pl.BlockSpec((S, w), lambda b: (lay.lat_seq_block + b, 0))
    ctx = lambda w: pl.BlockSpec((Lc, w), lambda b: (b, 0))
    in_specs = [pl.BlockSpec(memory_space=pltpu.SMEM), lat(W), lat(LANES), lat(LANES), ctx(LANES), ctx(LANES)]
    out_specs = [pl.BlockSpec((S, W), lambda b: (b, 0))]
    out_shape = [jax.ShapeDtypeStruct((lay.n_lat, W), BF)]
    args = [sink, q, k, v, k, v]
    if ctx_queries:
        in_specs.append(ctx(W))
        out_specs.append(ctx(W))
        out_shape.append(jax.ShapeDtypeStruct((lay.n_ctx, W), BF))
        args.append(q)
    return pl.pallas_call(
        functools.partial(_gqa_kernel, S=S, Lc=Lc, ctx_queries=ctx_queries),
        grid=(B,),
        in_specs=in_specs,
        out_specs=out_specs,
        out_shape=out_shape,
        scratch_shapes=[pltpu.VMEM((4, S, LANES), BF), pltpu.VMEM((4, S, LANES), BF),
                        pltpu.VMEM((4, Lc, LANES), BF), pltpu.VMEM((4, Lc, LANES), BF)],
        compiler_params=_params(1),
        name="gqa_window",
    )(*args)


def _mla_heads(q_ref, kv_refs, o_ref):
    lane = lax.broadcasted_iota(jnp.int32, (1, LANES), 1)
    low = lane < MLA_DV
    for pair in range(MLA_H // 2):
        vsl = slice(pair * LANES, (pair + 1) * LANES)
        acc = None
        for half in range(2):
            h = 2 * pair + half
            hsl = slice(h * LANES, (h + 1) * LANES)
            qh = q_ref[:, hsl]
            scores = [_dot_nt(qh, k_ref[:, hsl]) for k_ref, _ in kv_refs]
            m = functools.reduce(jnp.maximum, [jnp.max(s, axis=-1, keepdims=True) for s in scores])
            keep = low if half == 0 else jnp.logical_not(low)
            ones_lane = LANES - 1 if half == 0 else 0
            t = None
            for s, (_, v_ref) in zip(scores, kv_refs):
                vh = jnp.where(keep, v_ref[:, vsl], jnp.zeros((), BF))
                vh = jnp.where(lane == ones_lane, jnp.ones((), BF), vh)
                u = _dot(jnp.exp2(s - m).astype(BF), vh)
                t = u if t is None else t + u
            o = jnp.where(keep, t * (1.0 / t[:, ones_lane:ones_lane + 1]), 0.0)
            acc = o if acc is None else acc + o
        o_ref[:, vsl] = acc.astype(BF)


def _mla_lat_kernel(q_ref, kc_ref, kl_ref, vc_ref, vl_ref, o_ref):
    _mla_heads(q_ref, ((kc_ref, vc_ref), (kl_ref, vl_ref)), o_ref)


def _mla_ctx_kernel(q_ref, kc_ref, vc_ref, o_ref):
    _mla_heads(q_ref, ((kc_ref, vc_ref),), o_ref)


def _mla_latent(q, k, v, lay):
    B, S, Lc = lay.B, lay.S, lay.Lc
    tq = min(512, S)
    nq = S // tq
    lat_q0 = lay.n_ctx // tq
    kw, vw = MLA_H * LANES, MLA_H * MLA_DV
    return pl.pallas_call(
        _mla_lat_kernel,
        grid=(B, nq),
        in_specs=[pl.BlockSpec((tq, kw), lambda b, i: (lat_q0 + b * nq + i, 0)),
                  pl.BlockSpec((Lc, kw), lambda b, i: (b, 0)),
                  pl.BlockSpec((S, kw), lambda b, i: (lay.lat_seq_block + b, 0)),
                  pl.BlockSpec((Lc, vw), lambda b, i: (b, 0)),
                  pl.BlockSpec((S, vw), lambda b, i: (lay.lat_seq_block + b, 0))],
        out_specs=pl.BlockSpec((tq, vw), lambda b, i: (b * nq + i, 0)),
        out_shape=jax.ShapeDtypeStruct((lay.n_lat, vw), BF),
        compiler_params=_params(2),
        name="mla_latent",
    )(q, k, k, v, v)


def _mla_context(q, k, v, lay):
    B, Lc = lay.B, lay.Lc
    kw, vw = MLA_H * LANES, MLA_H * MLA_DV
    blk = lambda w: pl.BlockSpec((Lc, w), lambda b: (b, 0))
    return pl.pallas_call(
        _mla_ctx_kernel,
        grid=(B,),
        in_specs=[blk(kw), blk(kw), blk(vw)],
        out_specs=blk(vw),
        out_shape=jax.ShapeDtypeStruct((lay.n_ctx, vw), BF),
        compiler_params=_params(1),
        name="mla_context",
    )(q, k, v)


def _ret_chunk(Lc, S):
    return 256 if Lc % 256 == 0 and S % 256 == 0 else RET_CHUNK


def _ret_kernel(dec_ref, qc_ref, ql_ref, kc_ref, kl_ref, vc_ref, vl_ref, gc_ref, gl_ref, gn_ref,
                oc_ref, ol_ref, kv_s, r_s, *, Lc, S, ch):
    nc, nl = Lc // ch, S // ch
    n = nc + nl
    lg_f = jnp.log(_sigmoid(dec_ref[0, 0:1, :]))
    lg_b = jnp.log(_sigmoid(dec_ref[0, 1:2, :]))
    pos_i = lax.broadcasted_iota(jnp.int32, (ch, ch), 0).astype(F32)
    pos_j = lax.broadcasted_iota(jnp.int32, (ch, ch), 1).astype(F32)
    diff = pos_i - pos_j
    fwd = diff >= 0.0
    mask = jnp.where(fwd, jnp.exp(jnp.where(fwd, diff, 0.0) * lg_f), jnp.exp(jnp.where(fwd, 0.0, -diff) * lg_b))
    pos = pos_i[:, 0:1]
    lf, lb = lg_f[:, :LANES], lg_b[:, :LANES]
    zeta = (jnp.exp((ch - 1.0 - pos) * lf), jnp.exp(pos * lb))
    xi = (jnp.exp((pos + 1.0) * lf), jnp.exp((ch - pos) * lb))
    decay = (jnp.exp(ch * lf), jnp.exp(ch * lb))

    def rows(c_ref, l_ref, c):
        if c < nc:
            return c_ref[c * ch:(c + 1) * ch, :]
        return l_ref[(c - nc) * ch:(c - nc + 1) * ch, :]

    for c in range(n):
        k = rows(kc_ref, kl_ref, c).astype(F32)
        kz = jnp.concatenate([(k * zeta[0]).astype(BF), (k * zeta[1]).astype(BF)], axis=1)
        kv_s[c] = _dot_tn(kz, rows(vc_ref, vl_ref, c))

    orders = (list(range(n)), list(range(nc - 1, -1, -1)) + list(range(n - 1, nc - 1, -1)))
    for d, order in enumerate(orders):
        R = jnp.zeros((RET_DK, RET_DV), F32)
        for c in order:
            r_s[c, d * RET_DK:(d + 1) * RET_DK, :] = R.astype(BF)
            R = decay[d] * R + kv_s[c, d * RET_DK:(d + 1) * RET_DK, :]

    for c in range(n):
        q = rows(qc_ref, ql_ref, c)
        s = _dot_nt(q, rows(kc_ref, kl_ref, c)) * mask
        qf = q.astype(F32)
        qx = jnp.concatenate([(qf * xi[0]).astype(BF), (qf * xi[1]).astype(BF)], axis=1)
        y = _dot(s.astype(BF), rows(vc_ref, vl_ref, c)) + _dot(qx, r_s[c])
        mu = jnp.mean(y, axis=-1, keepdims=True)
        yc = y - mu
        var = jnp.mean(yc * yc, axis=-1, keepdims=True)
        yn = yc * lax.rsqrt(var + LN_EPS) * gn_ref[...]
        g = rows(gc_ref, gl_ref, c).astype(F32)
        out = (g * _sigmoid(g) * yn).astype(BF)
        if c < nc:
            oc_ref[c * ch:(c + 1) * ch, :] = out
        else:
            ol_ref[(c - nc) * ch:(c - nc + 1) * ch, :] = out


def _retention(dq, dk, dv, dg, decay_lanes, gn, lay):
    B, S, Lc = lay.B, lay.S, lay.Lc
    ch = _ret_chunk(Lc, S)
    n = (Lc + S) // ch
    ctx = pl.BlockSpec((Lc, LANES), lambda b, h: (b, h))
    lat = pl.BlockSpec((S, LANES), lambda b, h: (lay.lat_seq_block + b, h))
    return pl.pallas_call(
        functools.partial(_ret_kernel, Lc=Lc, S=S, ch=ch),
        grid=(B, RET_H),
        in_specs=[pl.BlockSpec((1, 2, ch), lambda b, h: (h, 0, 0)),
                  ctx, lat, ctx, lat, ctx, lat, ctx, lat, pl.BlockSpec((1, LANES), lambda b, h: (0, h))],
        out_specs=[pl.BlockSpec((Lc, LANES), lambda b, h: (b, h)), pl.BlockSpec((S, LANES), lambda b, h: (b, h))],
        out_shape=[jax.ShapeDtypeStruct((lay.n_ctx, RET_H * RET_DV), BF), jax.ShapeDtypeStruct((lay.n_lat, RET_H * RET_DV), BF)],
        scratch_shapes=[pltpu.VMEM((n, 2 * RET_DK, RET_DV), F32), pltpu.VMEM((n, 2 * RET_DK, RET_DV), BF)],
        compiler_params=_params(2),
        name="retention",
    )(decay_lanes, dq, dq, dk, dk, dv, dv, dg, dg, gn)


def _merge_kernel(*refs, n_br, ct):
    x_ref, mod_ref = refs[:2]
    br_refs = refs[2:2 + N_BRANCH * n_br]
    wg_ref, wb_ref, wo_ref, g_ref, beta_ref, o_ref = refs[2 + N_BRANCH * n_br:]
    x = x_ref[...]
    h = (x * (1.0 + mod_ref[0, 4:5, :]) + mod_ref[0, 3:4, :]).astype(BF)
    acc = None
    for k in range(N_BRANCH):
        br = _rows(br_refs[k * n_br:(k + 1) * n_br], pl.program_id(0), ct)
        gate = 0.5 * jnp.tanh(_dot(h, wg_ref[:, k * D_MODEL:(k + 1) * D_MODEL])) + 0.5
        t = gate * _dot(br, wb_ref[k])
        acc = t if acc is None else acc + t
    y = _dot(acc.astype(BF), wo_ref[...])
    z = DN_ALPHA * x + mod_ref[0, 5:6, :] * y
    o_ref[...] = _layer_norm(z, g_ref[...], beta_ref[...])


def _merge(x, mod, branches, w_gate, wb_all, wo_all, l, g, beta, lay, g_off):
    tm = lay.tm
    n = lay.tiles - g_off
    br_specs, br_args = [], []
    for br in branches:
        specs, args = _row_specs(br, BRANCH_W, lay, g_off)
        br_specs += specs
        br_args += args
    return pl.pallas_call(
        functools.partial(_merge_kernel, n_br=len(br_args) // N_BRANCH, ct=lay.ctx_tiles),
        grid=(n,),
        in_specs=[pl.BlockSpec((tm, D_MODEL), lambda i: (i + g_off, 0)),
                  pl.BlockSpec((1, 9, D_MODEL), lambda i: (lay.mod_index(i + g_off), 0, 0))] + br_specs + [
                  pl.BlockSpec((None,) + w_gate.shape[1:], lambda i: (l, 0, 0), pipeline_mode=pl.Buffered(1)),
                  pl.BlockSpec((None, N_BRANCH, BRANCH_W, D_MODEL), lambda i: (l, 0, 0, 0), pipeline_mode=pl.Buffered(1)),
                  pl.BlockSpec((None, D_MODEL, D_MODEL), lambda i: (l, 0, 0), pipeline_mode=pl.Buffered(1)),
                  _const((1, D_MODEL)), _const((1, D_MODEL))],
        out_specs=pl.BlockSpec((tm, D_MODEL), lambda i: (i, 0)),
        out_shape=jax.ShapeDtypeStruct((n * tm, D_MODEL), F32),
        compiler_params=_params(1),
        name="gated_merge",
    )(x, mod, *br_args, w_gate, wb_all, wo_all, g, beta)


def _rope_tables(S, head_dim, lead, reps, tail, tm):
    nf = head_dim // 4
    inv = ROPE_BASE ** (-np.arange(nf, dtype=np.float64) / nf)
    n_rows = S // GRID_W
    ar = np.arange(n_rows, dtype=np.float64)[:, None] * inv[None, :]
    ac = np.arange(GRID_W, dtype=np.float64)[:, None] * inv[None, :]

    def lanes(by_row, first, second, fill):
        n = first.shape[0]
        neutral = np.full((n, nf), fill)
        head = np.concatenate([first, second, neutral, neutral] if by_row else [neutral, neutral, first, second], -1)
        pad = lambda w: np.full((n, w), fill)
        return np.tile(np.concatenate([pad(lead), head, pad(tail)], -1), (1, reps))

    row_c = lanes(True, np.cos(ar), np.cos(ar), 1.0)
    row_s = lanes(True, -np.sin(ar), np.sin(ar), 0.0)
    col_c = lanes(False, np.cos(ac), np.cos(ac), 1.0)
    col_s = lanes(False, -np.sin(ac), np.sin(ac), 0.0)
    W = row_c.shape[1]
    per_tile = tm // GRID_W

    def by_tile(tab, fill):
        t = tab.reshape(S // tm, per_tile, W)
        t = np.concatenate([t, np.full((S // tm, SUBLANES - per_tile, W), fill)], 1).reshape(-1, W)
        return np.concatenate([t, np.full((SUBLANES, W), fill)], 0).astype(np.float32)

    ident = lambda fill: np.full((GRID_W, W), fill)
    return (by_tile(row_c, 1.0), by_tile(row_s, 0.0),
            np.stack([col_c, ident(1.0)]).astype(np.float32), np.stack([col_s, ident(0.0)]).astype(np.float32))


def _mixer_weights(w_in, lru_w_a, lru_b_a, lru_w_x, lru_b_x, mla_w_uq, mla_w_ukv):
    wb = w_in.astype(BF)
    cut = sum(IN_SIZES[:8])
    gate0 = sum(IN_SIZES[:12])
    assert cut + LANES - MLA_DR == _W_HEAD
    w_proj = (wb, wb[:, :, cut:gate0])
    w_gate = wb[:, :, gate0:] * jnp.asarray(0.5, BF)
    blocks = jnp.stack([lru_w_a[:, 0], lru_w_x[:, 0], lru_w_a[:, 1], lru_w_x[:, 1]], axis=1)
    eye = jnp.eye(LRU_BLOCKS, dtype=F32)
    dense = (0.5 * eye)[None, None, :, None, :, None] * blocks[:, :, :, :, None, :]
    wg = dense.reshape(DEPTH, 4, LRU_W, LRU_W).transpose(0, 2, 1, 3).reshape(DEPTH, LRU_W, 4 * LRU_W).astype(BF)
    bg = 0.5 * jnp.stack([lru_b_a[:, 0], lru_b_x[:, 0], lru_b_a[:, 1], lru_b_x[:, 1]], axis=1).reshape(DEPTH, 1, 4 * LRU_W)
    uq = mla_w_uq.reshape(DEPTH, MLA_RQ, MLA_H, MLA_DN + MLA_DR)
    wq = jnp.pad(uq, ((0, 0), (0, 0), (0, 0), (0, LANES - MLA_DN - MLA_DR))).reshape(DEPTH, MLA_RQ, MLA_H * LANES).astype(BF)
    ukv = mla_w_ukv.reshape(DEPTH, MLA_RKV, MLA_H, MLA_DN + MLA_DV)
    wk = jnp.pad(ukv[..., :MLA_DN], ((0, 0), (0, 0), (0, 0), (0, LANES - MLA_DN))).reshape(DEPTH, MLA_RKV, MLA_H * LANES).astype(BF)
    wv = ukv[..., MLA_DN:].reshape(DEPTH, MLA_RKV, MLA_H * MLA_DV).astype(BF)
    return w_proj, w_gate, wg, bg, wq, wk, wv


def kernel(x, c, ctx, c_ctx, w_mod, b_mod, ln_g, ln_b, ffn_w_in, ffn_w_out, w_in, lru_conv_w, lru_conv_b, lru_w_a, lru_b_a, lru_w_x, lru_b_x, lru_lambda, gqa_sink, mla_q_norm, mla_kv_norm, mla_w_uq, mla_w_ukv, ret_decay, ret_gn_g, w_branch, w_out):
    B, S, D = x.shape
    Lc = ctx.shape[1]
    lay = _Layout(B, S, Lc)
    tm = lay.tm

    n_cond = -(-(B + 1) // SUBLANES) * SUBLANES
    cond = jnp.concatenate([c, c_ctx[None, :], jnp.zeros((n_cond - B - 1, D), F32)], axis=0)
    mod_all = _modulation(cond, w_mod, b_mod)[:, :B + 1].reshape(DEPTH, B + 1, 9, D)

    tabs = [_rope_tables(S, head_dim, lead, reps, tail, tm) for head_dim, lead, reps, tail in
            ((GQA_DH, 0, GQA_HQ, 0), (GQA_DH, 0, GQA_HKV, 0), (MLA_DR, 0, 1, LANES - MLA_DR), (RET_DK, 0, RET_H, 0),
             (MLA_DR, MLA_DN, MLA_H, LANES - MLA_DN - MLA_DR))]

    tok = (ctx.reshape(B * Lc, D), x.reshape(B * S, D))
    row = lambda v: v[None, :]
    ct = lay.ctx_tiles
    w1_all, w2_all = ffn_w_in.astype(BF), ffn_w_out.astype(BF)
    wb_all, wo_all = w_branch.astype(BF), w_out.astype(BF)
    w_proj, w_gate, wg, bg, wq, wk, wv = _mixer_weights(w_in, lru_w_a, lru_b_a, lru_w_x, lru_b_x, mla_w_uq, mla_w_ukv)

    for l in range(DEPTH):
        last = l == DEPTH - 1
        mod = mod_all[l]
        dec = jnp.broadcast_to(ret_decay[l].T[:, :, None], (RET_H, 2, _ret_chunk(Lc, S)))

        tok = _half_ffn(tok, mod, w1_all, w2_all, l, 0, row(ln_g[l, 0]), row(ln_b[l, 0]), lay, 0, 0)

        ax, ay, bq, bk, bv, mq, mk, mv, dq, dk, dv, dg = _mixer_proj(
            tok, mod, w_proj, l, wq[l], wk[l], wv[l], row(mla_q_norm[l]), row(mla_kv_norm[l]), tabs, lay)
        a_c, a_l = _lru(ax, ay, lru_conv_w[l], row(lru_conv_b[l]), wg[l], bg[l], lru_lambda[l], lay)
        b_out = _gqa(bq, bk, bv, gqa_sink[l], lay, ctx_queries=not last)
        c_l = _mla_latent(mq, mk, mv, lay)
        d_c, d_l = _retention(dq, dk, dv, dg, dec, row(ret_gn_g[l]), lay)

        if last:
            branches = (a_l, b_out[0], c_l, d_l)
            g_off = ct
        else:
            branches = ((a_c, a_l), (b_out[1], b_out[0]), (_mla_context(mq, mk, mv, lay), c_l), (d_c, d_l))
            g_off = 0
        tok = _merge(tok, mod, branches, w_gate, wb_all, wo_all, l, row(ln_g[l, 1]), row(ln_b[l, 1]), lay, g_off)
        tok = _half_ffn(tok, mod, w1_all, w2_all, l, 1, row(ln_g[l, 2]), row(ln_b[l, 2]), lay, 6, g_off)

    return tok.reshape(B, S, D)
```

```python
import functools

import jax
import jax.numpy as jnp
import numpy as np
from jax import lax
from jax.experimental import pallas as pl
from jax.experimental.pallas import tpu as pltpu

D_MODEL = 1024
DEPTH = 2
GRID_W = 64
N_BRANCH = 4
BRANCH_W = 512
LRU_W = 512
LRU_BLOCKS = 8
LRU_BW = LRU_W // LRU_BLOCKS
LRU_C = 8.0
GQA_HQ = 8
GQA_HKV = 2
GQA_DH = 64
WINDOW = 128
ATT_BLOCK = 128
MLA_H = 8
MLA_RQ = 384
MLA_RKV = 256
MLA_DN = 64
MLA_DR = 32
MLA_DV = 64
RET_H = 4
RET_DK = 128
RET_DV = 128
RET_CHUNK = 128
FF = 2816
ROPE_BASE = 10000.0
LN_EPS = 1e-5
MASK_VALUE = -1e30
DN_ALPHA = (2 * DEPTH) ** 0.25
IN_SIZES = (LRU_W, LRU_W, GQA_HQ * GQA_DH, GQA_HKV * GQA_DH, GQA_HKV * GQA_DH, MLA_RQ, MLA_RKV, MLA_DR,
            RET_H * RET_DK, RET_H * RET_DK, RET_H * RET_DV, RET_H * RET_DV, N_BRANCH * D_MODEL)

LANES = 128
SUBLANES = 8
VMEM_LIMIT = 56 * 1024 * 1024
MXU_DIM = 256
FFN_CHUNKS = (6 * MXU_DIM, 5 * MXU_DIM)
LOG2E = 1.4426950408889634
BF = jnp.bfloat16
F32 = jnp.float32


def _params(n_axes):
    return pltpu.CompilerParams(dimension_semantics=("arbitrary",) * n_axes, vmem_limit_bytes=VMEM_LIMIT)


def _resident(shape):
    nd = len(shape)
    return pl.BlockSpec(shape, lambda *_: (0,) * nd, pipeline_mode=pl.Buffered(1))


def _const(shape):
    nd = len(shape)
    return pl.BlockSpec(shape, lambda *_: (0,) * nd)


def _layer_norm(z, g, b):
    mu = jnp.mean(z, axis=-1, keepdims=True)
    zc = z - mu
    var = jnp.mean(zc * zc, axis=-1, keepdims=True)
    return zc * lax.rsqrt(var + LN_EPS) * g + b


def _sigmoid(x):
    return 1.0 / (1.0 + jnp.exp(-x))


def _dot(a, b):
    return jnp.dot(a, b, preferred_element_type=F32)


def _dot_nt(a, b):
    return lax.dot_general(a, b, (((1,), (1,)), ((), ())), preferred_element_type=F32)


def _dot_tn(a, b):
    return lax.dot_general(a, b, (((0,), (0,)), ((), ())), preferred_element_type=F32)


def _rope(y, c, s, nf):
    w = y.shape[-1]
    lane = lax.broadcasted_iota(jnp.int32, y.shape, 1)
    partner = jnp.where((lane % (2 * nf)) < nf, pltpu.roll(y, w - nf, axis=1), pltpu.roll(y, nf, axis=1))
    return y * c + partner * s


class _Layout:
    def __init__(self, B, S, Lc):
        self.B, self.S, self.Lc = B, S, Lc
        self.n_ctx, self.n_lat = B * Lc, B * S
        self.n_tok = self.n_ctx + self.n_lat
        tm = 512
        while S % tm or self.n_ctx % tm:
            tm //= 2
        self.tm = tm
        self.ctx_tiles = self.n_ctx // tm
        self.tiles_per_seq = S // tm
        self.tiles = self.ctx_tiles + B * self.tiles_per_seq
        assert self.n_ctx % S == 0, "latent rows must start on a whole-sequence block"
        self.lat_seq_block = self.n_ctx // S
        self.lat_att_block = self.n_ctx // ATT_BLOCK

    def mod_index(self, g):
        return jnp.where(g < self.ctx_tiles, self.B, (g - self.ctx_tiles) // self.tiles_per_seq)

    def rope_index(self, g):
        return jnp.where(g < self.ctx_tiles, self.tiles_per_seq, (g - self.ctx_tiles) % self.tiles_per_seq)


def _mod_kernel(s_ref, w_ref, b_ref, o_ref):
    s = s_ref[...]
    s = s * _sigmoid(s)
    o_ref[0] = _dot(s.astype(BF), w_ref[0].astype(BF)) + b_ref[0]


def _modulation(cond, w_mod, b_mod):
    R = cond.shape[0]
    tn = 1024
    n9 = 9 * D_MODEL
    return pl.pallas_call(
        _mod_kernel,
        grid=(DEPTH, n9 // tn),
        in_specs=[pl.BlockSpec((R, D_MODEL), lambda l, j: (0, 0)),
                  pl.BlockSpec((1, D_MODEL, tn), lambda l, j: (l, 0, j)),
                  pl.BlockSpec((1, 1, tn), lambda l, j: (l, 0, j))],
        out_specs=pl.BlockSpec((1, R, tn), lambda l, j: (l, 0, j)),
        out_shape=jax.ShapeDtypeStruct((DEPTH, R, n9), F32),
        compiler_params=_params(2),
        name="modulation",
    )(cond, w_mod, b_mod.reshape(DEPTH, 1, n9))


def _row_specs(src, width, lay, g_off, tile=lambda i: i):
    tm, ct = lay.tm, lay.ctx_tiles
    if isinstance(src, tuple):
        assert g_off == 0
        return [pl.BlockSpec((tm, width), lambda i: (jnp.minimum(tile(i), ct - 1), 0)),
                pl.BlockSpec((tm, width), lambda i: (jnp.maximum(tile(i) - ct, 0), 0))], list(src)
    assert src.shape[0] == lay.n_tok or (src.shape[0] == lay.n_lat and g_off == ct)
    off = g_off if src.shape[0] == lay.n_tok else 0
    return [pl.BlockSpec((tm, width), lambda i: (tile(i) + off, 0))], [src]


def _rows(refs, t, ct):
    if len(refs) == 2:
        return jnp.where(t < ct, refs[0][...], refs[1][...])
    return refs[0][...]


def _ffn_kernel(*refs, k0, n_x, ct):
    mod_ref, w1_ref, w2_ref, g_ref, b_ref, o_ref = refs[n_x:]
    x = _rows(refs[:n_x], pl.program_id(0), ct)
    shift = mod_ref[0, k0:k0 + 1, :]
    scale = mod_ref[0, k0 + 1:k0 + 2, :]
    gate = mod_ref[0, k0 + 2:k0 + 3, :]
    xm = (x * (1.0 + scale) + shift).astype(BF)
    acc = jnp.zeros(x.shape, F32)
    assert sum(FFN_CHUNKS) == FF
    lo = 0
    up = []
    for width in FFN_CHUNKS:
        up.append((lo, width, _dot(xm, w1_ref[:, lo:lo + width]), _dot(xm, w1_ref[:, FF + lo:FF + lo + width])))
        lo += width
    for lo, width, a, b in up:
        h = (a * _sigmoid(a) * b).astype(BF)
        acc = acc + _dot(h, w2_ref[lo:lo + width, :])
    z = DN_ALPHA * x + (0.5 * gate) * acc
    o_ref[...] = _layer_norm(z, g_ref[...], b_ref[...])


def _half_ffn(x, mod, w1_all, w2_all, l, j, g, b, lay, k0, g_off):
    tm = lay.tm
    n = lay.tiles - g_off
    x_specs, x_args = _row_specs(x, D_MODEL, lay, g_off)
    pick = lambda *_: (l, j, 0, 0)
    return pl.pallas_call(
        functools.partial(_ffn_kernel, k0=k0, n_x=len(x_args), ct=lay.ctx_tiles),
        grid=(n,),
        in_specs=x_specs + [
            pl.BlockSpec((1, 9, D_MODEL), lambda i: (lay.mod_index(i + g_off), 0, 0)),
            pl.BlockSpec((None, None, D_MODEL, 2 * FF), pick, pipeline_mode=pl.Buffered(1)),
            pl.BlockSpec((None, None, FF, D_MODEL), pick, pipeline_mode=pl.Buffered(1)),
            _const((1, D_MODEL)), _const((1, D_MODEL))],
        out_specs=pl.BlockSpec((tm, D_MODEL), lambda i: (i, 0)),
        out_shape=jax.ShapeDtypeStruct((n * tm, D_MODEL), F32),
        compiler_params=_params(1),
        name="half_ffn",
    )(*x_args, mod, w1_all, w2_all, g, b)


_W_LRU, _W_GQ, _W_GKV, _W_RET = LRU_W, GQA_HQ * GQA_DH, GQA_HKV * GQA_DH, RET_H * RET_DK
_OFF_AX, _OFF_AY, _OFF_BQ, _OFF_BKV = 0, 512, 1024, 1536
_OFF_MLA, _W_MLA = 1792, MLA_RQ + MLA_RKV + LANES
_W_HEAD = _OFF_MLA + _W_MLA
_OFF_DQ, _OFF_DK, _OFF_DV, _OFF_DG = 0, 512, 1024, 1536
_PROJ_OUT_WIDTHS = (_W_LRU, _W_LRU, _W_GQ, _W_GKV, _W_GKV, MLA_H * LANES, MLA_H * LANES, MLA_H * MLA_DV,
                    _W_RET, _W_RET, _W_RET, _W_RET)
_N_ROPE_TABLES = 5


def _rope_tile(y, tab, nf):
    rc_ref, rs_ref, cc_ref, cs_ref = tab
    parts = []
    for g in range(y.shape[0] // GRID_W):
        c = rc_ref[g:g + 1, :] * cc_ref[...]
        s = rs_ref[g:g + 1, :] + cs_ref[...]
        parts.append(_rope(y[g * GRID_W:(g + 1) * GRID_W, :], c, s, nf))
    return jnp.concatenate(parts, axis=0)


def _rms(x, g):
    return x * lax.rsqrt(jnp.mean(x * x, axis=-1, keepdims=True) + LN_EPS) * g


def _proj_kernel(x_ref, mod_ref, w_ref, wt_ref, wq_ref, wk_ref, wv_ref, gq_ref, gkv_ref, *refs):
    tabs = [refs[4 * i:4 * i + 4] for i in range(_N_ROPE_TABLES)]
    t_gq, t_gk, t_kr, t_ret, t_mq = tabs
    ax_o, ay_o, bq_o, bk_o, bv_o, mq_o, mk_o, mv_o, dq_o, dk_o, dv_o, dg_o = refs[4 * _N_ROPE_TABLES:]
    x = x_ref[...]
    h = (x * (1.0 + mod_ref[0, 4:5, :]) + mod_ref[0, 3:4, :]).astype(BF)
    proj = lambda lo, width: _dot(h, w_ref[:, lo:lo + width])
    tail = lambda lo, width: _dot(h, wt_ref[:, lo:lo + width])

    c = proj(_OFF_MLA, _W_MLA)
    y_bq = proj(_OFF_BQ, _W_GQ)
    yq = _rms(c[:, :MLA_RQ], gq_ref[...]).astype(BF)
    ykv = _rms(c[:, MLA_RQ:MLA_RQ + MLA_RKV], gkv_ref[...]).astype(BF)
    y_mq = _dot(yq, wq_ref[...])
    bq_o[...] = (_rope_tile(y_bq, t_gq, GQA_DH // 4) * (GQA_DH ** -0.5 * LOG2E)).astype(BF)
    y_mk = _dot(ykv, wk_ref[...])
    mq_o[...] = (_rope_tile(y_mq, t_mq, MLA_DR // 4) * ((MLA_DN + MLA_DR) ** -0.5 * LOG2E)).astype(BF)
    y_mv = _dot(ykv, wv_ref[...])
    kr = c[:, MLA_RQ + MLA_RKV:]
    kr = jnp.where(lax.broadcasted_iota(jnp.int32, kr.shape, 1) < MLA_DR, kr, 0.0)
    kr = _rope_tile(kr, t_kr, MLA_DR // 4)
    kr_all = pltpu.roll(jnp.concatenate([kr] * MLA_H, axis=1), MLA_DN, axis=1)
    mk_o[...] = (y_mk + kr_all).astype(BF)
    y_dq = tail(_OFF_DQ, _W_RET)
    mv_o[...] = y_mv.astype(BF)
    y_dk = tail(_OFF_DK, _W_RET)
    dq_o[...] = _rope_tile(y_dq, t_ret, RET_DK // 4).astype(BF)
    kv = proj(_OFF_BKV, 2 * _W_GKV)
    dk_o[...] = (_rope_tile(y_dk, t_ret, RET_DK // 4) * RET_DK ** -0.5).astype(BF)
    y_ax = proj(_OFF_AX, _W_LRU)
    bk_o[...] = _rope_tile(kv[:, :_W_GKV], t_gk, GQA_DH // 4).astype(BF)
    bv_o[...] = kv[:, _W_GKV:].astype(BF)
    y_ay = proj(_OFF_AY, _W_LRU)
    ax_o[...] = y_ax.astype(BF)
    y_dv = tail(_OFF_DV, _W_RET)
    ay_o[...] = y_ay.astype(BF)
    y_dg = tail(_OFF_DG, _W_RET)
    dv_o[...] = y_dv.astype(BF)
    dg_o[...] = y_dg.astype(BF)


def _rope_specs(tab, lay):
    w = tab[0].shape[1]
    row = pl.BlockSpec((SUBLANES, w), lambda i: (lay.rope_index(i), 0))
    col = pl.BlockSpec((None, GRID_W, w), lambda i: (jnp.where(i < lay.ctx_tiles, 1, 0), 0, 0))
    return [row, row, col, col]


def _mixer_proj(x, mod, w_all, l, wq, wk, wv, gq, gkv, tabs, lay):
    tm = lay.tm
    row = lambda i: (i, 0)
    w_head, w_tail = w_all
    assert len(tabs) == _N_ROPE_TABLES
    tab_specs, tab_args = [], []
    for tab in tabs:
        tab_specs += _rope_specs(tab, lay)
        tab_args += list(tab)
    return pl.pallas_call(
        _proj_kernel,
        grid=(lay.tiles,),
        in_specs=[pl.BlockSpec((tm, D_MODEL), row),
                  pl.BlockSpec((1, 9, D_MODEL), lambda i: (lay.mod_index(i), 0, 0)),
                  pl.BlockSpec((None, D_MODEL, _W_HEAD), lambda i: (l, 0, 0), pipeline_mode=pl.Buffered(1)),
                  pl.BlockSpec((None,) + w_tail.shape[1:], lambda i: (l, 0, 0), pipeline_mode=pl.Buffered(1)),
                  _const(wq.shape), _const(wk.shape), _const(wv.shape), _const(gq.shape), _const(gkv.shape)] + tab_specs,
        out_specs=[pl.BlockSpec((tm, w), row) for w in _PROJ_OUT_WIDTHS],
        out_shape=[jax.ShapeDtypeStruct((lay.n_tok, w), BF) for w in _PROJ_OUT_WIDTHS],
        compiler_params=_params(1),
        name="mixer_proj",
    )(x, mod, w_head, w_tail, wq, wk, wv, gq, gkv, *tab_args)


def _conv4(xp_s, base, n, w_ref, b_ref):
    y = b_ref[...] + w_ref[2:3, :] * xp_s[base:base + n, :]
    y = y + w_ref[0:1, :] * xp_s[base - 2:base - 2 + n, :]
    y = y + w_ref[1:2, :] * xp_s[base - 1:base - 1 + n, :]
    y = y + w_ref[3:4, :] * xp_s[base + 1:base + 1 + n, :]
    return y


def _scan8(a, b, reverse):
    row = lax.broadcasted_iota(jnp.int32, a.shape, 0)
    for s in (1, 2, 4):
        if reverse:
            keep = row < SUBLANES - s
            a_sh = pltpu.roll(a, SUBLANES - s, axis=0)
            b_sh = pltpu.roll(b, SUBLANES - s, axis=0)
        else:
            keep = row >= s
            a_sh = pltpu.roll(a, s, axis=0)
            b_sh = pltpu.roll(b, s, axis=0)
        b = jnp.where(keep, b + a * b_sh, b)
        a = jnp.where(keep, a * a_sh, a)
    return a, b


def _lru_segment(T):
    for seg in (36, 44, 28, 20, 12, 52, 60, 4, 8, 16, 32):
        if T % (SUBLANES * seg) == 0:
            return seg
    raise ValueError(f"no scan segment length for {T} rows")


def _lru_kernel(axc_ref, axl_ref, ayc_ref, ayl_ref, cw_ref, cb_ref, wg_ref, bg_ref, lam_ref,
                oc_ref, ol_ref, xp_s, xa_s, a_s, b_s, h0_s, h1_s, ac_s, bc_s, *, Lc, S, rows, seg):
    W = LRU_W
    T = Lc + S
    NS = W // LANES
    G = SUBLANES * seg
    P = SUBLANES
    zpad = jnp.zeros((P, W), F32)
    xp_s[0:P, :] = zpad
    xp_s[P:P + Lc, :] = axc_ref[...].astype(F32)
    xp_s[P + Lc:2 * P + Lc, :] = zpad
    xp_s[2 * P + Lc:2 * P + T, :] = axl_ref[...].astype(F32)
    xp_s[2 * P + T:3 * P + T, :] = zpad
    xa_s[0:Lc, :] = _conv4(xp_s, P, Lc, cw_ref, cb_ref)
    xa_s[Lc:T, :] = _conv4(xp_s, 2 * P + Lc, S, cw_ref, cb_ref)
    row8 = lax.broadcasted_iota(jnp.int32, (SUBLANES, LANES), 0)
    zero8 = jnp.zeros((SUBLANES, LANES), F32)

    for d in range(2):
        reverse = d == 1
        h_s = h1_s if reverse else h0_s
        neg = -lam_ref[d:d + 1, :]
        softplus = jnp.maximum(neg, 0.0) + jnp.log(1.0 + jnp.exp(-jnp.abs(neg)))
        c = (-0.5 * LRU_C * 1.4426950408889634) * softplus

        def gates(i, carry):
            r0 = pl.multiple_of(i * rows, rows)
            dst = pl.multiple_of(jnp.where(r0 < Lc, r0 + S, r0 - Lc), rows) if reverse else r0
            xa = xa_s[pl.ds(r0, rows), :]
            xb = xa.astype(BF)
            lo = 2 * d * W
            t_r = jnp.tanh(_dot(xb, wg_ref[:, lo:lo + W]) + bg_ref[:, lo:lo + W])
            t_i = jnp.tanh(_dot(xb, wg_ref[:, lo + W:lo + 2 * W]) + bg_ref[:, lo + W:lo + 2 * W])
            a = jnp.exp2(c * t_r + c)
            y = 1.0 - a * a
            root = jnp.where(y > 0.0, y * lax.rsqrt(y), 0.0)
            b = (root * (0.5 * xa)) * (t_i + 1.0)
            for k in range(NS):
                a_s[k, pl.ds(dst, rows), :] = a[:, k * LANES:(k + 1) * LANES]
                b_s[k, pl.ds(dst, rows), :] = b[:, k * LANES:(k + 1) * LANES]
            return carry

        lax.fori_loop(0, T // rows, gates, 0, unroll=3 if (T // rows) % 3 == 0 else 1)

        def group(j, h_in):
            base = ((T // G - 1 - j) if reverse else j) * G
            acc_a, acc_b = [None] * NS, [None] * NS
            for i in (range(seg - 1, -1, -1) if reverse else range(seg)):
                for k in range(NS):
                    a = a_s[k, pl.ds(base + i, SUBLANES, stride=seg), :]
                    b = b_s[k, pl.ds(base + i, SUBLANES, stride=seg), :]
                    if acc_a[k] is None:
                        acc_a[k], acc_b[k] = a, b
                    else:
                        acc_b[k] = a * acc_b[k] + b
                        acc_a[k] = a * acc_a[k]
                    ac_s[k, i] = acc_a[k]
                    bc_s[k, i] = acc_b[k]
            h_out, enter = [], []
            for k in range(NS):
                tot_a, tot_b = _scan8(acc_a[k], acc_b[k], reverse)
                after = tot_a * h_in[k] + tot_b
                if reverse:
                    enter.append(jnp.where(row8 == SUBLANES - 1, h_in[k], pltpu.roll(after, SUBLANES - 1, axis=0)))
                    h_out.append(jnp.broadcast_to(after[0:1, :], after.shape))
                else:
                    enter.append(jnp.where(row8 == 0, h_in[k], pltpu.roll(after, 1, axis=0)))
                    h_out.append(jnp.broadcast_to(after[SUBLANES - 1:SUBLANES, :], after.shape))
            for i in range(seg):
                for k in range(NS):
                    h_s[k, pl.ds(base + i, SUBLANES, stride=seg), :] = ac_s[k, i] * enter[k] + bc_s[k, i]
            return tuple(h_out)

        lax.fori_loop(0, T // G, group, (zero8,) * NS)

    def gelu(v):
        c = 0.7978845608028654
        hv = 0.5 * v
        return hv + hv * jnp.tanh(v * (c + (c * 0.044715) * (v * v)))

    for k in range(NS):
        sl = slice(k * LANES, (k + 1) * LANES)
        hc = h0_s[k, 0:Lc, :] + h1_s[k, S:T, :]
        hl = h0_s[k, Lc:T, :] + h1_s[k, 0:S, :]
        oc_ref[:, sl] = (hc * gelu(ayc_ref[:, sl].astype(F32))).astype(BF)
        ol_ref[:, sl] = (hl * gelu(ayl_ref[:, sl].astype(F32))).astype(BF)


def _lru(ax, ay, conv_w, conv_b, wg, bg, lam, lay):
    B, S, Lc = lay.B, lay.S, lay.Lc
    W = LRU_W
    T = Lc + S
    rows = 256
    while Lc % rows or S % rows:
        rows //= 2
    seg = _lru_segment(T)
    ctx = pl.BlockSpec((Lc, W), lambda b: (b, 0))
    lat = pl.BlockSpec((S, W), lambda b: (b + lay.lat_seq_block, 0))
    slabs = pltpu.VMEM((W // LANES, T, LANES), F32)
    part = pltpu.VMEM((W // LANES, seg, SUBLANES, LANES), F32)
    return pl.pallas_call(
        functools.partial(_lru_kernel, Lc=Lc, S=S, rows=rows, seg=seg),
        grid=(B,),
        in_specs=[ctx, lat, ctx, lat, _const((4, W)), _const((1, W)), _const(wg.shape), _const((1, 4 * W)), _const((2, W))],
        out_specs=[pl.BlockSpec((Lc, W), lambda b: (b, 0)), pl.BlockSpec((S, W), lambda b: (b, 0))],
        out_shape=[jax.ShapeDtypeStruct((lay.n_ctx, W), BF), jax.ShapeDtypeStruct((lay.n_lat, W), BF)],
        scratch_shapes=[pltpu.VMEM((T + 3 * SUBLANES, W), F32), pltpu.VMEM((T, W), F32), slabs, slabs, slabs, slabs, part, part],
        compiler_params=_params(1),
        name="rglru",
    )(ax, ax, ay, ay, conv_w, conv_b, wg, bg, lam)


def _split_heads_pair(x, ones_lane):
    xf = x.astype(F32)
    sw = pltpu.roll(xf, GQA_DH, axis=1)
    lane = lax.broadcasted_iota(jnp.int32, xf.shape, 1)
    low = lane < GQA_DH
    lo_fill = jnp.where(lane == LANES - 1, 1.0, 0.0) if ones_lane else jnp.zeros_like(xf)
    hi_fill = jnp.where(lane == 0, 1.0, 0.0) if ones_lane else jnp.zeros_like(xf)
    head0 = (jnp.where(low, xf, lo_fill).astype(BF), jnp.where(low, hi_fill, sw).astype(BF))
    head1 = (jnp.where(low, sw, lo_fill).astype(BF), jnp.where(low, hi_fill, xf).astype(BF))
    return head0, head1


def _gqa_kernel(sink_ref, *refs, S, Lc, ctx_queries):
    if ctx_queries:
        q_ref, k_ref, v_ref, kx_ref, vx_ref, qx_ref, o_ref, ox_ref, ks_s, vs_s, kxs_s, vxs_s = refs
    else:
        q_ref, k_ref, v_ref, kx_ref, vx_ref, o_ref, ks_s, vs_s, kxs_s, vxs_s = refs
    blk = ATT_BLOCK
    nb = S // blk
    win = min(3 * blk, S)
    for src, dst, is_value in ((k_ref, ks_s, False), (v_ref, vs_s, True), (kx_ref, kxs_s, False), (vx_ref, vxs_s, True)):
        heads = _split_heads_pair(src[...], is_value)
        for j in range(GQA_HKV):
            for half in range(2):
                dst[2 * j + half] = heads[j][half]

    row = lax.broadcasted_iota(jnp.int32, (2 * blk, win), 0)
    col = lax.broadcasted_iota(jnp.int32, (2 * blk, win), 1)
    rel = (row % blk) - col
    top = row[:, 0:1] < blk
    low = lax.broadcasted_iota(jnp.int32, (1, LANES), 1) < GQA_DH

    def issue(q, kstart):
        scores = []
        for j in range(GQA_HKV):
            q2 = jnp.concatenate([q[:, 2 * j * LANES:(2 * j + 1) * LANES], q[:, (2 * j + 1) * LANES:(2 * j + 2) * LANES]], axis=0)
            for half in range(2):
                i = 2 * j + half
                s_w = None if kstart is None else _dot_nt(q2, ks_s[i, pl.ds(kstart, win), :])
                scores.append((_dot_nt(q2, kxs_s[i]), s_w))
        return scores

    def finish(scores, kstart, valid):
        outs = []
        for j in range(GQA_HKV):
            acc = jnp.zeros((2 * blk, LANES), F32)
            for half in range(2):
                i = 2 * j + half
                sink = jnp.where(top, sink_ref[4 * j + half] * LOG2E, sink_ref[4 * j + 2 + half] * LOG2E)
                s_c, s_w = scores[i]
                m = jnp.maximum(jnp.max(s_c, axis=-1, keepdims=True), sink)
                if kstart is not None:
                    s_w = jnp.where(valid, s_w, MASK_VALUE)
                    m = jnp.maximum(m, jnp.max(s_w, axis=-1, keepdims=True))
                t = _dot(jnp.exp2(s_c - m).astype(BF), vxs_s[i])
                if kstart is not None:
                    t = t + _dot(jnp.exp2(s_w - m).astype(BF), vs_s[i, pl.ds(kstart, win), :])
                ones_lane = LANES - 1 if half == 0 else 0
                den = t[:, ones_lane:ones_lane + 1] + jnp.exp2(sink - m)
                acc = acc + jnp.where(low if half == 0 else jnp.logical_not(low), t * (1.0 / den), 0.0)
            outs += [acc[0:blk].astype(BF), acc[blk:2 * blk].astype(BF)]
        return jnp.concatenate(outs, axis=1)

    group = 2 if nb % 2 == 0 else 1

    def blocks(g, carry):
        pending = []
        for n in [g * group + u for u in range(group)]:
            q0 = pl.multiple_of(n * blk, blk)
            kstart = pl.multiple_of(jnp.clip((n - 1) * blk, 0, S - win), blk)
            pending.append((q0, kstart, issue(q_ref[pl.ds(q0, blk), :], kstart)))
        for q0, kstart, scores in pending:
            dist = rel + (q0 - kstart)
            valid = (dist <= WINDOW) & (dist >= -WINDOW)
            o_ref[pl.ds(q0, blk), :] = finish(scores, kstart, valid)
        return carry

    lax.fori_loop(0, nb // group, blocks, 0)
    if ctx_queries:
        for n in range(Lc // blk):
            ox_ref[n * blk:(n + 1) * blk, :] = finish(issue(qx_ref[n * blk:(n + 1) * blk, :], None), None, None)


def _gqa(q, k, v, sink, lay, ctx_queries):
    B, S, Lc = lay.B, lay.S, lay.Lc
    W = GQA_HQ * GQA_DH
    lat = lambda w: pl.BlockSpec((S, w), lambda b: (lay.lat_seq_block + b, 0))
    ctx = lambda w: pl.BlockSpec((Lc, w), lambda b: (b, 0))
    in_specs = [pl.BlockSpec(memory_space=pltpu.SMEM), lat(W), lat(LANES), lat(LANES), ctx(LANES), ctx(LANES)]
    out_specs = [pl.BlockSpec((S, W), lambda b: (b, 0))]
    out_shape = [jax.ShapeDtypeStruct((lay.n_lat, W), BF)]
    args = [sink, q, k, v, k, v]
    if ctx_queries:
        in_specs.append(ctx(W))
        out_specs.append(ctx(W))
        out_shape.append(jax.ShapeDtypeStruct((lay.n_ctx, W), BF))
        args.append(q)
    return pl.pallas_call(
        functools.partial(_gqa_kernel, S=S, Lc=Lc, ctx_queries=ctx_queries),
        grid=(B,),
        in_specs=in_specs,
        out_specs=out_specs,
        out_shape=out_shape,
        scratch_shapes=[pltpu.VMEM((4, S, LANES), BF), pltpu.VMEM((4, S, LANES), BF),
                        pltpu.VMEM((4, Lc, LANES), BF), pltpu.VMEM((4, Lc, LANES), BF)],
        compiler_params=_params(1),
        name="gqa_window",
    )(*args)


def _mla_heads(q_ref, kv_refs, o_ref):
    lane = lax.broadcasted_iota(jnp.int32, (1, LANES), 1)
    low = lane < MLA_DV

    def head_scores(h):
        hsl = slice(h * LANES, (h + 1) * LANES)
        qh = q_ref[:, hsl]
        return [_dot_nt(qh, k_ref[:, hsl]) for k_ref, _ in kv_refs]

    ahead = 2
    upcoming = [head_scores(h) for h in range(ahead)]
    for pair in range(MLA_H // 2):
        vsl = slice(pair * LANES, (pair + 1) * LANES)
        acc = None
        for half in range(2):
            h = 2 * pair + half
            scores = upcoming.pop(0)
            if h + ahead < MLA_H:
                upcoming.append(head_scores(h + ahead))
            m = functools.reduce(jnp.maximum, [jnp.max(s, axis=-1, keepdims=True) for s in scores])
            keep = low if half == 0 else jnp.logical_not(low)
            ones_lane = LANES - 1 if half == 0 else 0
            t = None
            for s, (_, v_ref) in zip(scores, kv_refs):
                vh = jnp.where(keep, v_ref[:, vsl], jnp.zeros((), BF))
                vh = jnp.where(lane == ones_lane, jnp.ones((), BF), vh)
                u = _dot(jnp.exp2(s - m).astype(BF), vh)
                t = u if t is None else t + u
            o = jnp.where(keep, t * (1.0 / t[:, ones_lane:ones_lane + 1]), 0.0)
            acc = o if acc is None else acc + o
        o_ref[:, vsl] = acc.astype(BF)


def _mla_lat_kernel(q_ref, kc_ref, kl_ref, vc_ref, vl_ref, o_ref):
    _mla_heads(q_ref, ((kc_ref, vc_ref), (kl_ref, vl_ref)), o_ref)


def _mla_ctx_kernel(q_ref, kc_ref, vc_ref, o_ref):
    _mla_heads(q_ref, ((kc_ref, vc_ref),), o_ref)


def _mla_latent(q, k, v, lay):
    B, S, Lc = lay.B, lay.S, lay.Lc
    tq = min(512, S)
    nq = S // tq
    lat_q0 = lay.n_ctx // tq
    kw, vw = MLA_H * LANES, MLA_H * MLA_DV
    return pl.pallas_call(
        _mla_lat_kernel,
        grid=(B, nq),
        in_specs=[pl.BlockSpec((tq, kw), lambda b, i: (lat_q0 + b * nq + i, 0)),
                  pl.BlockSpec((Lc, kw), lambda b, i: (b, 0)),
                  pl.BlockSpec((S, kw), lambda b, i: (lay.lat_seq_block + b, 0)),
                  pl.BlockSpec((Lc, vw), lambda b, i: (b, 0)),
                  pl.BlockSpec((S, vw), lambda b, i: (lay.lat_seq_block + b, 0))],
        out_specs=pl.BlockSpec((tq, vw), lambda b, i: (b * nq + i, 0)),
        out_shape=jax.ShapeDtypeStruct((lay.n_lat, vw), BF),
        compiler_params=_params(2),
        name="mla_latent",
    )(q, k, k, v, v)


def _mla_context(q, k, v, lay):
    B, Lc = lay.B, lay.Lc
    kw, vw = MLA_H * LANES, MLA_H * MLA_DV
    blk = lambda w: pl.BlockSpec((Lc, w), lambda b: (b, 0))
    return pl.pallas_call(
        _mla_ctx_kernel,
        grid=(B,),
        in_specs=[blk(kw), blk(kw), blk(vw)],
        out_specs=blk(vw),
        out_shape=jax.ShapeDtypeStruct((lay.n_ctx, vw), BF),
        compiler_params=_params(1),
        name="mla_context",
    )(q, k, v)


def _ret_chunk(Lc, S):
    return 256 if Lc % 256 == 0 and S % 256 == 0 else RET_CHUNK


def _ret_kernel(dec_ref, qc_ref, ql_ref, kc_ref, kl_ref, vc_ref, vl_ref, gc_ref, gl_ref, gn_ref,
                oc_ref, ol_ref, kv_s, r_s, *, Lc, S, ch):
    nc, nl = Lc // ch, S // ch
    n = nc + nl
    lg_f = jnp.log(_sigmoid(dec_ref[0, 0:1, :]))
    lg_b = jnp.log(_sigmoid(dec_ref[0, 1:2, :]))
    pos_i = lax.broadcasted_iota(jnp.int32, (ch, ch), 0).astype(F32)
    pos_j = lax.broadcasted_iota(jnp.int32, (ch, ch), 1).astype(F32)
    diff = pos_i - pos_j
    fwd = diff >= 0.0
    mask = jnp.where(fwd, jnp.exp(jnp.where(fwd, diff, 0.0) * lg_f), jnp.exp(jnp.where(fwd, 0.0, -diff) * lg_b))
    pos = pos_i[:, 0:1]
    lf, lb = lg_f[:, :LANES], lg_b[:, :LANES]
    zeta = (jnp.exp((ch - 1.0 - pos) * lf), jnp.exp(pos * lb))
    xi = (jnp.exp((pos + 1.0) * lf), jnp.exp((ch - pos) * lb))
    decay = (jnp.exp(ch * lf), jnp.exp(ch * lb))

    def rows(c_ref, l_ref, c):
        if c < nc:
            return c_ref[c * ch:(c + 1) * ch, :]
        return l_ref[(c - nc) * ch:(c - nc + 1) * ch, :]

    for c in range(n):
        k = rows(kc_ref, kl_ref, c).astype(F32)
        kz = jnp.concatenate([(k * zeta[0]).astype(BF), (k * zeta[1]).astype(BF)], axis=1)
        kv_s[c] = _dot_tn(kz, rows(vc_ref, vl_ref, c))

    orders = (list(range(n)), list(range(nc - 1, -1, -1)) + list(range(n - 1, nc - 1, -1)))
    for d, order in enumerate(orders):
        R = jnp.zeros((RET_DK, RET_DV), F32)
        for c in order:
            r_s[c, d * RET_DK:(d + 1) * RET_DK, :] = R.astype(BF)
            R = decay[d] * R + kv_s[c, d * RET_DK:(d + 1) * RET_DK, :]

    for c in range(n):
        q = rows(qc_ref, ql_ref, c)
        s = _dot_nt(q, rows(kc_ref, kl_ref, c)) * mask
        qf = q.astype(F32)
        qx = jnp.concatenate([(qf * xi[0]).astype(BF), (qf * xi[1]).astype(BF)], axis=1)
        y = _dot(s.astype(BF), rows(vc_ref, vl_ref, c)) + _dot(qx, r_s[c])
        mu = jnp.mean(y, axis=-1, keepdims=True)
        yc = y - mu
        var = jnp.mean(yc * yc, axis=-1, keepdims=True)
        yn = yc * lax.rsqrt(var + LN_EPS) * gn_ref[...]
        g = rows(gc_ref, gl_ref, c).astype(F32)
        out = (g * _sigmoid(g) * yn).astype(BF)
        if c < nc:
            oc_ref[c * ch:(c + 1) * ch, :] = out
        else:
            ol_ref[(c - nc) * ch:(c - nc + 1) * ch, :] = out


def _retention(dq, dk, dv, dg, decay_lanes, gn, lay):
    B, S, Lc = lay.B, lay.S, lay.Lc
    ch = _ret_chunk(Lc, S)
    n = (Lc + S) // ch
    ctx = pl.BlockSpec((Lc, LANES), lambda b, h: (b, h))
    lat = pl.BlockSpec((S, LANES), lambda b, h: (lay.lat_seq_block + b, h))
    return pl.pallas_call(
        functools.partial(_ret_kernel, Lc=Lc, S=S, ch=ch),
        grid=(B, RET_H),
        in_specs=[pl.BlockSpec((1, 2, ch), lambda b, h: (h, 0, 0)),
                  ctx, lat, ctx, lat, ctx, lat, ctx, lat, pl.BlockSpec((1, LANES), lambda b, h: (0, h))],
        out_specs=[pl.BlockSpec((Lc, LANES), lambda b, h: (b, h)), pl.BlockSpec((S, LANES), lambda b, h: (b, h))],
        out_shape=[jax.ShapeDtypeStruct((lay.n_ctx, RET_H * RET_DV), BF), jax.ShapeDtypeStruct((lay.n_lat, RET_H * RET_DV), BF)],
        scratch_shapes=[pltpu.VMEM((n, 2 * RET_DK, RET_DV), F32), pltpu.VMEM((n, 2 * RET_DK, RET_DV), BF)],
        compiler_params=_params(2),
        name="retention",
    )(decay_lanes, dq, dq, dk, dk, dv, dv, dg, dg, gn)


def _merge_kernel(*refs, n_br, ct):
    x_ref, mod_ref = refs[:2]
    br_refs = refs[2:2 + N_BRANCH * n_br]
    wg_ref, wb_ref, wo_ref, g_ref, beta_ref, o_ref = refs[2 + N_BRANCH * n_br:]
    x = x_ref[...]
    h = (x * (1.0 + mod_ref[0, 4:5, :]) + mod_ref[0, 3:4, :]).astype(BF)
    acc = None
    pre = lambda k: _dot(h, wg_ref[:, k * D_MODEL:(k + 1) * D_MODEL])
    z_next = pre(0)
    for k in range(N_BRANCH):
        z_gate = z_next
        if k + 1 < N_BRANCH:
            z_next = pre(k + 1)
        br = _rows(br_refs[k * n_br:(k + 1) * n_br], pl.program_id(0), ct)
        t = (0.5 * jnp.tanh(z_gate) + 0.5) * _dot(br, wb_ref[k])
        acc = t if acc is None else acc + t
    y = _dot(acc.astype(BF), wo_ref[...])
    z = DN_ALPHA * x + mod_ref[0, 5:6, :] * y
    o_ref[...] = _layer_norm(z, g_ref[...], beta_ref[...])


def _merge(x, mod, branches, w_gate, wb_all, wo_all, l, g, beta, lay, g_off):
    tm = lay.tm
    n = lay.tiles - g_off
    br_specs, br_args = [], []
    for br in branches:
        specs, args = _row_specs(br, BRANCH_W, lay, g_off)
        br_specs += specs
        br_args += args
    return pl.pallas_call(
        functools.partial(_merge_kernel, n_br=len(br_args) // N_BRANCH, ct=lay.ctx_tiles),
        grid=(n,),
        in_specs=[pl.BlockSpec((tm, D_MODEL), lambda i: (i + g_off, 0)),
                  pl.BlockSpec((1, 9, D_MODEL), lambda i: (lay.mod_index(i + g_off), 0, 0))] + br_specs + [
                  pl.BlockSpec((None,) + w_gate.shape[1:], lambda i: (l, 0, 0), pipeline_mode=pl.Buffered(1)),
                  pl.BlockSpec((None, N_BRANCH, BRANCH_W, D_MODEL), lambda i: (l, 0, 0, 0), pipeline_mode=pl.Buffered(1)),
                  pl.BlockSpec((None, D_MODEL, D_MODEL), lambda i: (l, 0, 0), pipeline_mode=pl.Buffered(1)),
                  _const((1, D_MODEL)), _const((1, D_MODEL))],
        out_specs=pl.BlockSpec((tm, D_MODEL), lambda i: (i, 0)),
        out_shape=jax.ShapeDtypeStruct((n * tm, D_MODEL), F32),
        compiler_params=_params(1),
        name="gated_merge",
    )(x, mod, *br_args, w_gate, wb_all, wo_all, g, beta)


def _rope_tables(S, head_dim, lead, reps, tail, tm):
    nf = head_dim // 4
    inv = ROPE_BASE ** (-np.arange(nf, dtype=np.float64) / nf)
    n_rows = S // GRID_W
    ar = np.arange(n_rows, dtype=np.float64)[:, None] * inv[None, :]
    ac = np.arange(GRID_W, dtype=np.float64)[:, None] * inv[None, :]

    def lanes(by_row, first, second, fill):
        n = first.shape[0]
        neutral = np.full((n, nf), fill)
        head = np.concatenate([first, second, neutral, neutral] if by_row else [neutral, neutral, first, second], -1)
        pad = lambda w: np.full((n, w), fill)
        return np.tile(np.concatenate([pad(lead), head, pad(tail)], -1), (1, reps))

    row_c = lanes(True, np.cos(ar), np.cos(ar), 1.0)
    row_s = lanes(True, -np.sin(ar), np.sin(ar), 0.0)
    col_c = lanes(False, np.cos(ac), np.cos(ac), 1.0)
    col_s = lanes(False, -np.sin(ac), np.sin(ac), 0.0)
    W = row_c.shape[1]
    per_tile = tm // GRID_W

    def by_tile(tab, fill):
        t = tab.reshape(S // tm, per_tile, W)
        t = np.concatenate([t, np.full((S // tm, SUBLANES - per_tile, W), fill)], 1).reshape(-1, W)
        return np.concatenate([t, np.full((SUBLANES, W), fill)], 0).astype(np.float32)

    ident = lambda fill: np.full((GRID_W, W), fill)
    return (by_tile(row_c, 1.0), by_tile(row_s, 0.0),
            np.stack([col_c, ident(1.0)]).astype(np.float32), np.stack([col_s, ident(0.0)]).astype(np.float32))


def _mixer_weights(w_in, lru_w_a, lru_b_a, lru_w_x, lru_b_x, mla_w_uq, mla_w_ukv):
    wb = w_in.astype(BF)
    cut = sum(IN_SIZES[:8])
    gate0 = sum(IN_SIZES[:12])
    assert cut + LANES - MLA_DR == _W_HEAD
    w_proj = (wb, wb[:, :, cut:gate0])
    w_gate = wb[:, :, gate0:] * jnp.asarray(0.5, BF)
    blocks = jnp.stack([lru_w_a[:, 0], lru_w_x[:, 0], lru_w_a[:, 1], lru_w_x[:, 1]], axis=1)
    eye = jnp.eye(LRU_BLOCKS, dtype=F32)
    dense = (0.5 * eye)[None, None, :, None, :, None] * blocks[:, :, :, :, None, :]
    wg = dense.reshape(DEPTH, 4, LRU_W, LRU_W).transpose(0, 2, 1, 3).reshape(DEPTH, LRU_W, 4 * LRU_W).astype(BF)
    bg = 0.5 * jnp.stack([lru_b_a[:, 0], lru_b_x[:, 0], lru_b_a[:, 1], lru_b_x[:, 1]], axis=1).reshape(DEPTH, 1, 4 * LRU_W)
    uq = mla_w_uq.reshape(DEPTH, MLA_RQ, MLA_H, MLA_DN + MLA_DR)
    wq = jnp.pad(uq, ((0, 0), (0, 0), (0, 0), (0, LANES - MLA_DN - MLA_DR))).reshape(DEPTH, MLA_RQ, MLA_H * LANES).astype(BF)
    ukv = mla_w_ukv.reshape(DEPTH, MLA_RKV, MLA_H, MLA_DN + MLA_DV)
    wk = jnp.pad(ukv[..., :MLA_DN], ((0, 0), (0, 0), (0, 0), (0, LANES - MLA_DN))).reshape(DEPTH, MLA_RKV, MLA_H * LANES).astype(BF)
    wv = ukv[..., MLA_DN:].reshape(DEPTH, MLA_RKV, MLA_H * MLA_DV).astype(BF)
    return w_proj, w_gate, wg, bg, wq, wk, wv


def kernel(x, c, ctx, c_ctx, w_mod, b_mod, ln_g, ln_b, ffn_w_in, ffn_w_out, w_in, lru_conv_w, lru_conv_b, lru_w_a, lru_b_a, lru_w_x, lru_b_x, lru_lambda, gqa_sink, mla_q_norm, mla_kv_norm, mla_w_uq, mla_w_ukv, ret_decay, ret_gn_g, w_branch, w_out):
    B, S, D = x.shape
    Lc = ctx.shape[1]
    lay = _Layout(B, S, Lc)
    tm = lay.tm

    n_cond = -(-(B + 1) // SUBLANES) * SUBLANES
    cond = jnp.concatenate([c, c_ctx[None, :], jnp.zeros((n_cond - B - 1, D), F32)], axis=0)
    mod_all = _modulation(cond, w_mod, b_mod)[:, :B + 1].reshape(DEPTH, B + 1, 9, D)

    tabs = [_rope_tables(S, head_dim, lead, reps, tail, tm) for head_dim, lead, reps, tail in
            ((GQA_DH, 0, GQA_HQ, 0), (GQA_DH, 0, GQA_HKV, 0), (MLA_DR, 0, 1, LANES - MLA_DR), (RET_DK, 0, RET_H, 0),
             (MLA_DR, MLA_DN, MLA_H, LANES - MLA_DN - MLA_DR))]

    tok = (ctx.reshape(B * Lc, D), x.reshape(B * S, D))
    row = lambda v: v[None, :]
    ct = lay.ctx_tiles
    w1_all, w2_all = ffn_w_in.astype(BF), ffn_w_out.astype(BF)
    wb_all, wo_all = w_branch.astype(BF), w_out.astype(BF)
    w_proj, w_gate, wg, bg, wq, wk, wv = _mixer_weights(w_in, lru_w_a, lru_b_a, lru_w_x, lru_b_x, mla_w_uq, mla_w_ukv)

    for l in range(DEPTH):
        last = l == DEPTH - 1
        mod = mod_all[l]
        dec = jnp.broadcast_to(ret_decay[l].T[:, :, None], (RET_H, 2, _ret_chunk(Lc, S)))

        tok = _half_ffn(tok, mod, w1_all, w2_all, l, 0, row(ln_g[l, 0]), row(ln_b[l, 0]), lay, 0, 0)

        ax, ay, bq, bk, bv, mq, mk, mv, dq, dk, dv, dg = _mixer_proj(
            tok, mod, w_proj, l, wq[l], wk[l], wv[l], row(mla_q_norm[l]), row(mla_kv_norm[l]), tabs, lay)
        a_c, a_l = _lru(ax, ay, lru_conv_w[l], row(lru_conv_b[l]), wg[l], bg[l], lru_lambda[l], lay)
        b_out = _gqa(bq, bk, bv, gqa_sink[l], lay, ctx_queries=not last)
        c_l = _mla_latent(mq, mk, mv, lay)
        d_c, d_l = _retention(dq, dk, dv, dg, dec, row(ret_gn_g[l]), lay)

        if last:
            branches = (a_l, b_out[0], c_l, d_l)
            g_off = ct
        else:
            branches = ((a_c, a_l), (b_out[1], b_out[0]), (_mla_context(mq, mk, mv, lay), c_l), (d_c, d_l))
            g_off = 0
        tok = _merge(tok, mod, branches, w_gate, wb_all, wo_all, l, row(ln_g[l, 1]), row(ln_b[l, 1]), lay, g_off)
        tok = _half_ffn(tok, mod, w1_all, w2_all, l, 1, row(ln_g[l, 2]), row(ln_b[l, 2]), lay, 6, g_off)

    return tok.reshape(B, S, D)
```

```python
import functools

import jax
import jax.numpy as jnp
import numpy as np
from jax import lax
from jax.experimental import pallas as pl
from jax.experimental.pallas import tpu as pltpu

D_MODEL = 1024
DEPTH = 2
GRID_W = 64
N_BRANCH = 4
BRANCH_W = 512
LRU_W = 512
LRU_BLOCKS = 8
LRU_BW = LRU_W // LRU_BLOCKS
LRU_C = 8.0
GQA_HQ = 8
GQA_HKV = 2
GQA_DH = 64
WINDOW = 128
ATT_BLOCK = 128
MLA_H = 8
MLA_RQ = 384
MLA_RKV = 256
MLA_DN = 64
MLA_DR = 32
MLA_DV = 64
RET_H = 4
RET_DK = 128
RET_DV = 128
RET_CHUNK = 128
FF = 2816
ROPE_BASE = 10000.0
LN_EPS = 1e-5
MASK_VALUE = -1e30
DN_ALPHA = (2 * DEPTH) ** 0.25
IN_SIZES = (LRU_W, LRU_W, GQA_HQ * GQA_DH, GQA_HKV * GQA_DH, GQA_HKV * GQA_DH, MLA_RQ, MLA_RKV, MLA_DR,
            RET_H * RET_DK, RET_H * RET_DK, RET_H * RET_DV, RET_H * RET_DV, N_BRANCH * D_MODEL)

LANES = 128
SUBLANES = 8
VMEM_LIMIT = 56 * 1024 * 1024
MXU_DIM = 256
FFN_CHUNKS = (6 * MXU_DIM, 5 * MXU_DIM)
LOG2E = 1.4426950408889634
BF = jnp.bfloat16
F32 = jnp.float32


def _params(n_axes):
    return pltpu.CompilerParams(dimension_semantics=("arbitrary",) * n_axes, vmem_limit_bytes=VMEM_LIMIT)


def _resident(shape):
    nd = len(shape)
    return pl.BlockSpec(shape, lambda *_: (0,) * nd, pipeline_mode=pl.Buffered(1))


def _const(shape):
    nd = len(shape)
    return pl.BlockSpec(shape, lambda *_: (0,) * nd)


def _layer_norm(z, g, b):
    mu = jnp.mean(z, axis=-1, keepdims=True)
    zc = z - mu
    var = jnp.mean(zc * zc, axis=-1, keepdims=True)
    return zc * lax.rsqrt(var + LN_EPS) * g + b


def _sigmoid(x):
    return 1.0 / (1.0 + jnp.exp(-x))


def _dot(a, b):
    return jnp.dot(a, b, preferred_element_type=F32)


def _dot_nt(a, b):
    return lax.dot_general(a, b, (((1,), (1,)), ((), ())), preferred_element_type=F32)


def _dot_tn(a, b):
    return lax.dot_general(a, b, (((0,), (0,)), ((), ())), preferred_element_type=F32)


def _rope(y, c, s, nf):
    w = y.shape[-1]
    lane = lax.broadcasted_iota(jnp.int32, y.shape, 1)
    partner = jnp.where((lane % (2 * nf)) < nf, pltpu.roll(y, w - nf, axis=1), pltpu.roll(y, nf, axis=1))
    return y * c + partner * s


class _Layout:
    def __init__(self, B, S, Lc):
        self.B, self.S, self.Lc = B, S, Lc
        self.n_ctx, self.n_lat = B * Lc, B * S
        self.n_tok = self.n_ctx + self.n_lat
        tm = 512
        while S % tm or self.n_ctx % tm:
            tm //= 2
        self.tm = tm
        self.ctx_tiles = self.n_ctx // tm
        self.tiles_per_seq = S // tm
        self.tiles = self.ctx_tiles + B * self.tiles_per_seq
        assert self.n_ctx % S == 0, "latent rows must start on a whole-sequence block"
        self.lat_seq_block = self.n_ctx // S
        self.lat_att_block = self.n_ctx // ATT_BLOCK

    def mod_index(self, g):
        return jnp.where(g < self.ctx_tiles, self.B, (g - self.ctx_tiles) // self.tiles_per_seq)

    def rope_index(self, g):
        return jnp.where(g < self.ctx_tiles, self.tiles_per_seq, (g - self.ctx_tiles) % self.tiles_per_seq)


def _mod_kernel(s_ref, w_ref, b_ref, o_ref):
    s = s_ref[...]
    s = s * _sigmoid(s)
    o_ref[0] = _dot(s.astype(BF), w_ref[0].astype(BF)) + b_ref[0]


def _modulation(cond, w_mod, b_mod):
    R = cond.shape[0]
    tn = 1024
    n9 = 9 * D_MODEL
    return pl.pallas_call(
        _mod_kernel,
        grid=(DEPTH, n9 // tn),
        in_specs=[pl.BlockSpec((R, D_MODEL), lambda l, j: (0, 0)),
                  pl.BlockSpec((1, D_MODEL, tn), lambda l, j: (l, 0, j)),
                  pl.BlockSpec((1, 1, tn), lambda l, j: (l, 0, j))],
        out_specs=pl.BlockSpec((1, R, tn), lambda l, j: (l, 0, j)),
        out_shape=jax.ShapeDtypeStruct((DEPTH, R, n9), F32),
        compiler_params=_params(2),
        name="modulation",
    )(cond, w_mod, b_mod.reshape(DEPTH, 1, n9))


def _row_specs(src, width, lay, g_off, tile=lambda i: i):
    tm, ct = lay.tm, lay.ctx_tiles
    if isinstance(src, tuple):
        assert g_off == 0
        return [pl.BlockSpec((tm, width), lambda i: (jnp.minimum(tile(i), ct - 1), 0)),
                pl.BlockSpec((tm, width), lambda i: (jnp.maximum(tile(i) - ct, 0), 0))], list(src)
    assert src.shape[0] == lay.n_tok or (src.shape[0] == lay.n_lat and g_off == ct)
    off = g_off if src.shape[0] == lay.n_tok else 0
    return [pl.BlockSpec((tm, width), lambda i: (tile(i) + off, 0))], [src]


def _rows(refs, t, ct):
    if len(refs) == 2:
        return jnp.where(t < ct, refs[0][...], refs[1][...])
    return refs[0][...]


def _ffn_kernel(*refs, k0, n_x, ct):
    mod_ref, w1_ref, w2_ref, g_ref, b_ref, o_ref = refs[n_x:]
    x = _rows(refs[:n_x], pl.program_id(0), ct)
    shift = mod_ref[0, k0:k0 + 1, :]
    scale = mod_ref[0, k0 + 1:k0 + 2, :]
    gate = mod_ref[0, k0 + 2:k0 + 3, :]
    xm = (x * (1.0 + scale) + shift).astype(BF)
    assert sum(FFN_CHUNKS) == FF
    lo = 0
    up = []
    for width in FFN_CHUNKS:
        up.append((lo, width, _dot(xm, w1_ref[:, lo:lo + width]), _dot(xm, w1_ref[:, FF + lo:FF + lo + width])))
        lo += width
    hidden = [(lo, width, (a * _sigmoid(a) * b).astype(BF)) for lo, width, a, b in up]
    half = x.shape[0] // 2
    down = []
    for r in range(2):
        rows = slice(r * half, (r + 1) * half)
        down.append(sum(_dot(h[rows, :], w2_ref[lo:lo + width, :]) for lo, width, h in hidden))
    for r in range(2):
        rows = slice(r * half, (r + 1) * half)
        z = DN_ALPHA * x[rows, :] + (0.5 * gate) * down[r]
        o_ref[rows, :] = _layer_norm(z, g_ref[...], b_ref[...])


def _half_ffn(x, mod, w1_all, w2_all, l, j, g, b, lay, k0, g_off):
    tm = lay.tm
    n = lay.tiles - g_off
    x_specs, x_args = _row_specs(x, D_MODEL, lay, g_off)
    pick = lambda *_: (l, j, 0, 0)
    return pl.pallas_call(
        functools.partial(_ffn_kernel, k0=k0, n_x=len(x_args), ct=lay.ctx_tiles),
        grid=(n,),
        in_specs=x_specs + [
            pl.BlockSpec((1, 9, D_MODEL), lambda i: (lay.mod_index(i + g_off), 0, 0)),
            pl.BlockSpec((None, None, D_MODEL, 2 * FF), pick, pipeline_mode=pl.Buffered(1)),
            pl.BlockSpec((None, None, FF, D_MODEL), pick, pipeline_mode=pl.Buffered(1)),
            _const((1, D_MODEL)), _const((1, D_MODEL))],
        out_specs=pl.BlockSpec((tm, D_MODEL), lambda i: (i, 0)),
        out_shape=jax.ShapeDtypeStruct((n * tm, D_MODEL), F32),
        compiler_params=_params(1),
        name="half_ffn",
    )(*x_args, mod, w1_all, w2_all, g, b)


_W_LRU, _W_GQ, _W_GKV, _W_RET = LRU_W, GQA_HQ * GQA_DH, GQA_HKV * GQA_DH, RET_H * RET_DK
_OFF_AX, _OFF_AY, _OFF_BQ, _OFF_BKV = 0, 512, 1024, 1536
_OFF_MLA, _W_MLA = 1792, MLA_RQ + MLA_RKV + LANES
_W_HEAD = _OFF_MLA + _W_MLA
_OFF_DQ, _OFF_DK, _OFF_DV, _OFF_DG = 0, 512, 1024, 1536
_PROJ_OUT_WIDTHS = (_W_LRU, _W_LRU, _W_GQ, _W_GKV, _W_GKV, MLA_H * LANES, MLA_H * LANES, MLA_H * MLA_DV,
                    _W_RET, _W_RET, _W_RET, _W_RET)
_N_ROPE_TABLES = 5


def _rope_tile(y, tab, nf):
    rc_ref, rs_ref, cc_ref, cs_ref = tab
    parts = []
    for g in range(y.shape[0] // GRID_W):
        c = rc_ref[g:g + 1, :] * cc_ref[...]
        s = rs_ref[g:g + 1, :] + cs_ref[...]
        parts.append(_rope(y[g * GRID_W:(g + 1) * GRID_W, :], c, s, nf))
    return jnp.concatenate(parts, axis=0)


def _rms(x, g):
    return x * lax.rsqrt(jnp.mean(x * x, axis=-1, keepdims=True) + LN_EPS) * g


def _proj_kernel(x_ref, mod_ref, w_ref, wt_ref, wq_ref, wk_ref, wv_ref, gq_ref, gkv_ref, *refs):
    tabs = [refs[4 * i:4 * i + 4] for i in range(_N_ROPE_TABLES)]
    t_gq, t_gk, t_kr, t_ret, t_mq = tabs
    ax_o, ay_o, bq_o, bk_o, bv_o, mq_o, mk_o, mv_o, dq_o, dk_o, dv_o, dg_o = refs[4 * _N_ROPE_TABLES:]
    x = x_ref[...]
    h = (x * (1.0 + mod_ref[0, 4:5, :]) + mod_ref[0, 3:4, :]).astype(BF)
    proj = lambda lo, width: _dot(h, w_ref[:, lo:lo + width])
    tail = lambda lo, width: _dot(h, wt_ref[:, lo:lo + width])

    c = proj(_OFF_MLA, _W_MLA)
    y_bq = proj(_OFF_BQ, _W_GQ)
    yq = _rms(c[:, :MLA_RQ], gq_ref[...]).astype(BF)
    ykv = _rms(c[:, MLA_RQ:MLA_RQ + MLA_RKV], gkv_ref[...]).astype(BF)
    y_mq = _dot(yq, wq_ref[...])
    bq_o[...] = (_rope_tile(y_bq, t_gq, GQA_DH // 4) * (GQA_DH ** -0.5 * LOG2E)).astype(BF)
    y_mk = _dot(ykv, wk_ref[...])
    mq_o[...] = (_rope_tile(y_mq, t_mq, MLA_DR // 4) * ((MLA_DN + MLA_DR) ** -0.5 * LOG2E)).astype(BF)
    y_mv = _dot(ykv, wv_ref[...])
    kr = c[:, MLA_RQ + MLA_RKV:]
    kr = jnp.where(lax.broadcasted_iota(jnp.int32, kr.shape, 1) < MLA_DR, kr, 0.0)
    kr = _rope_tile(kr, t_kr, MLA_DR // 4)
    kr_all = pltpu.roll(jnp.concatenate([kr] * MLA_H, axis=1), MLA_DN, axis=1)
    mk_o[...] = (y_mk + kr_all).astype(BF)
    y_dq = tail(_OFF_DQ, _W_RET)
    mv_o[...] = y_mv.astype(BF)
    y_dk = tail(_OFF_DK, _W_RET)
    dq_o[...] = _rope_tile(y_dq, t_ret, RET_DK // 4).astype(BF)
    kv = proj(_OFF_BKV, 2 * _W_GKV)
    dk_o[...] = (_rope_tile(y_dk, t_ret, RET_DK // 4) * RET_DK ** -0.5).astype(BF)
    y_ax = proj(_OFF_AX, _W_LRU)
    bk_o[...] = _rope_tile(kv[:, :_W_GKV], t_gk, GQA_DH // 4).astype(BF)
    bv_o[...] = kv[:, _W_GKV:].astype(BF)
    y_ay = proj(_OFF_AY, _W_LRU)
    ax_o[...] = y_ax.astype(BF)
    y_dv = tail(_OFF_DV, _W_RET)
    ay_o[...] = y_ay.astype(BF)
    y_dg = tail(_OFF_DG, _W_RET)
    dv_o[...] = y_dv.astype(BF)
    dg_o[...] = y_dg.astype(BF)


def _rope_specs(tab, lay):
    w = tab[0].shape[1]
    row = pl.BlockSpec((SUBLANES, w), lambda i: (lay.rope_index(i), 0))
    col = pl.BlockSpec((None, GRID_W, w), lambda i: (jnp.where(i < lay.ctx_tiles, 1, 0), 0, 0))
    return [row, row, col, col]


def _mixer_proj(x, mod, w_all, l, wq, wk, wv, gq, gkv, tabs, lay):
    tm = lay.tm
    row = lambda i: (i, 0)
    w_head, w_tail = w_all
    assert len(tabs) == _N_ROPE_TABLES
    tab_specs, tab_args = [], []
    for tab in tabs:
        tab_specs += _rope_specs(tab, lay)
        tab_args += list(tab)
    return pl.pallas_call(
        _proj_kernel,
        grid=(lay.tiles,),
        in_specs=[pl.BlockSpec((tm, D_MODEL), row),
                  pl.BlockSpec((1, 9, D_MODEL), lambda i: (lay.mod_index(i), 0, 0)),
                  pl.BlockSpec((None, D_MODEL, _W_HEAD), lambda i: (l, 0, 0), pipeline_mode=pl.Buffered(1)),
                  pl.BlockSpec((None,) + w_tail.shape[1:], lambda i: (l, 0, 0), pipeline_mode=pl.Buffered(1)),
                  _const(wq.shape), _const(wk.shape), _const(wv.shape), _const(gq.shape), _const(gkv.shape)] + tab_specs,
        out_specs=[pl.BlockSpec((tm, w), row) for w in _PROJ_OUT_WIDTHS],
        out_shape=[jax.ShapeDtypeStruct((lay.n_tok, w), BF) for w in _PROJ_OUT_WIDTHS],
        compiler_params=_params(1),
        name="mixer_proj",
    )(x, mod, w_head, w_tail, wq, wk, wv, gq, gkv, *tab_args)


def _conv4(xp_s, base, n, w_ref, b_ref):
    y = b_ref[...] + w_ref[2:3, :] * xp_s[base:base + n, :]
    y = y + w_ref[0:1, :] * xp_s[base - 2:base - 2 + n, :]
    y = y + w_ref[1:2, :] * xp_s[base - 1:base - 1 + n, :]
    y = y + w_ref[3:4, :] * xp_s[base + 1:base + 1 + n, :]
    return y


def _scan8(a, b, reverse):
    row = lax.broadcasted_iota(jnp.int32, a.shape, 0)
    for s in (1, 2, 4):
        if reverse:
            keep = row < SUBLANES - s
            a_sh = pltpu.roll(a, SUBLANES - s, axis=0)
            b_sh = pltpu.roll(b, SUBLANES - s, axis=0)
        else:
            keep = row >= s
            a_sh = pltpu.roll(a, s, axis=0)
            b_sh = pltpu.roll(b, s, axis=0)
        b = jnp.where(keep, b + a * b_sh, b)
        a = jnp.where(keep, a * a_sh, a)
    return a, b


def _lru_segment(T):
    for seg in (36, 44, 28, 20, 12, 52, 60, 4, 8, 16, 32):
        if T % (SUBLANES * seg) == 0:
            return seg
    raise ValueError(f"no scan segment length for {T} rows")


def _lru_kernel(axc_ref, axl_ref, ayc_ref, ayl_ref, cw_ref, cb_ref, wg_ref, bg_ref, lam_ref,
                oc_ref, ol_ref, xp_s, xa_s, a_s, b_s, h0_s, h1_s, ac_s, bc_s, *, Lc, S, rows, seg):
    W = LRU_W
    T = Lc + S
    NS = W // LANES
    G = SUBLANES * seg
    P = SUBLANES
    zpad = jnp.zeros((P, W), F32)
    xp_s[0:P, :] = zpad
    xp_s[P:P + Lc, :] = axc_ref[...].astype(F32)
    xp_s[P + Lc:2 * P + Lc, :] = zpad
    xp_s[2 * P + Lc:2 * P + T, :] = axl_ref[...].astype(F32)
    xp_s[2 * P + T:3 * P + T, :] = zpad
    xa_s[0:Lc, :] = _conv4(xp_s, P, Lc, cw_ref, cb_ref)
    xa_s[Lc:T, :] = _conv4(xp_s, 2 * P + Lc, S, cw_ref, cb_ref)
    row8 = lax.broadcasted_iota(jnp.int32, (SUBLANES, LANES), 0)
    zero8 = jnp.zeros((SUBLANES, LANES), F32)

    for d in range(2):
        reverse = d == 1
        h_s = h1_s if reverse else h0_s
        neg = -lam_ref[d:d + 1, :]
        softplus = jnp.maximum(neg, 0.0) + jnp.log(1.0 + jnp.exp(-jnp.abs(neg)))
        c = (-0.5 * LRU_C * 1.4426950408889634) * softplus

        def gates(i, carry):
            r0 = pl.multiple_of(i * rows, rows)
            dst = pl.multiple_of(jnp.where(r0 < Lc, r0 + S, r0 - Lc), rows) if reverse else r0
            xa = xa_s[pl.ds(r0, rows), :]
            xb = xa.astype(BF)
            lo = 2 * d * W
            t_r = jnp.tanh(_dot(xb, wg_ref[:, lo:lo + W]) + bg_ref[:, lo:lo + W])
            t_i = jnp.tanh(_dot(xb, wg_ref[:, lo + W:lo + 2 * W]) + bg_ref[:, lo + W:lo + 2 * W])
            a = jnp.exp2(c * t_r + c)
            y = 1.0 - a * a
            root = jnp.where(y > 0.0, y * lax.rsqrt(y), 0.0)
            b = (root * (0.5 * xa)) * (t_i + 1.0)
            for k in range(NS):
                a_s[k, pl.ds(dst, rows), :] = a[:, k * LANES:(k + 1) * LANES]
                b_s[k, pl.ds(dst, rows), :] = b[:, k * LANES:(k + 1) * LANES]
            return carry

        lax.fori_loop(0, T // rows, gates, 0, unroll=3 if (T // rows) % 3 == 0 else 1)

        def group(j, h_in):
            base = ((T // G - 1 - j) if reverse else j) * G
            acc_a, acc_b = [None] * NS, [None] * NS
            for i in (range(seg - 1, -1, -1) if reverse else range(seg)):
                for k in range(NS):
                    a = a_s[k, pl.ds(base + i, SUBLANES, stride=seg), :]
                    b = b_s[k, pl.ds(base + i, SUBLANES, stride=seg), :]
                    if acc_a[k] is None:
                        acc_a[k], acc_b[k] = a, b
                    else:
                        acc_b[k] = a * acc_b[k] + b
                        acc_a[k] = a * acc_a[k]
                    ac_s[k, i] = acc_a[k]
                    bc_s[k, i] = acc_b[k]
            h_out, enter = [], []
            for k in range(NS):
                tot_a, tot_b = _scan8(acc_a[k], acc_b[k], reverse)
                after = tot_a * h_in[k] + tot_b
                if reverse:
                    enter.append(jnp.where(row8 == SUBLANES - 1, h_in[k], pltpu.roll(after, SUBLANES - 1, axis=0)))
                    h_out.append(jnp.broadcast_to(after[0:1, :], after.shape))
                else:
                    enter.append(jnp.where(row8 == 0, h_in[k], pltpu.roll(after, 1, axis=0)))
                    h_out.append(jnp.broadcast_to(after[SUBLANES - 1:SUBLANES, :], after.shape))
            for i in range(seg):
                for k in range(NS):
                    h_s[k, pl.ds(base + i, SUBLANES, stride=seg), :] = ac_s[k, i] * enter[k] + bc_s[k, i]
            return tuple(h_out)

        lax.fori_loop(0, T // G, group, (zero8,) * NS)

    def gelu(v):
        c = 0.7978845608028654
        hv = 0.5 * v
        return hv + hv * jnp.tanh(v * (c + (c * 0.044715) * (v * v)))

    for k in range(NS):
        sl = slice(k * LANES, (k + 1) * LANES)
        hc = h0_s[k, 0:Lc, :] + h1_s[k, S:T, :]
        hl = h0_s[k, Lc:T, :] + h1_s[k, 0:S, :]
        oc_ref[:, sl] = (hc * gelu(ayc_ref[:, sl].astype(F32))).astype(BF)
        ol_ref[:, sl] = (hl * gelu(ayl_ref[:, sl].astype(F32))).astype(BF)


def _lru(ax, ay, conv_w, conv_b, wg, bg, lam, lay):
    B, S, Lc = lay.B, lay.S, lay.Lc
    W = LRU_W
    T = Lc + S
    rows = 256
    while Lc % rows or S % rows:
        rows //= 2
    seg = _lru_segment(T)
    ctx = pl.BlockSpec((Lc, W), lambda b: (b, 0))
    lat = pl.BlockSpec((S, W), lambda b: (b + lay.lat_seq_block, 0))
    slabs = pltpu.VMEM((W // LANES, T, LANES), F32)
    part = pltpu.VMEM((W // LANES, seg, SUBLANES, LANES), F32)
    return pl.pallas_call(
        functools.partial(_lru_kernel, Lc=Lc, S=S, rows=rows, seg=seg),
        grid=(B,),
        in_specs=[ctx, lat, ctx, lat, _const((4, W)), _const((1, W)), _const(wg.shape), _const((1, 4 * W)), _const((2, W))],
        out_specs=[pl.BlockSpec((Lc, W), lambda b: (b, 0)), pl.BlockSpec((S, W), lambda b: (b, 0))],
        out_shape=[jax.ShapeDtypeStruct((lay.n_ctx, W), BF), jax.ShapeDtypeStruct((lay.n_lat, W), BF)],
        scratch_shapes=[pltpu.VMEM((T + 3 * SUBLANES, W), F32), pltpu.VMEM((T, W), F32), slabs, slabs, slabs, slabs, part, part],
        compiler_params=_params(1),
        name="rglru",
    )(ax, ax, ay, ay, conv_w, conv_b, wg, bg, lam)


def _split_heads_pair(x, ones_lane):
    xf = x.astype(F32)
    sw = pltpu.roll(xf, GQA_DH, axis=1)
    lane = lax.broadcasted_iota(jnp.int32, xf.shape, 1)
    low = lane < GQA_DH
    lo_fill = jnp.where(lane == LANES - 1, 1.0, 0.0) if ones_lane else jnp.zeros_like(xf)
    hi_fill = jnp.where(lane == 0, 1.0, 0.0) if ones_lane else jnp.zeros_like(xf)
    head0 = (jnp.where(low, xf, lo_fill).astype(BF), jnp.where(low, hi_fill, sw).astype(BF))
    head1 = (jnp.where(low, sw, lo_fill).astype(BF), jnp.where(low, hi_fill, xf).astype(BF))
    return head0, head1


def _gqa_kernel(sink_ref, *refs, S, Lc, ctx_queries):
    if ctx_queries:
        q_ref, k_ref, v_ref, kx_ref, vx_ref, qx_ref, o_ref, ox_ref, ks_s, vs_s, kxs_s, vxs_s = refs
    else:
        q_ref, k_ref, v_ref, kx_ref, vx_ref, o_ref, ks_s, vs_s, kxs_s, vxs_s = refs
    blk = ATT_BLOCK
    nb = S // blk
    win = min(3 * blk, S)
    for src, dst, is_value in ((k_ref, ks_s, False), (v_ref, vs_s, True), (kx_ref, kxs_s, False), (vx_ref, vxs_s, True)):
        heads = _split_heads_pair(src[...], is_value)
        for j in range(GQA_HKV):
            for half in range(2):
                dst[2 * j + half] = heads[j][half]

    row = lax.broadcasted_iota(jnp.int32, (2 * blk, win), 0)
    col = lax.broadcasted_iota(jnp.int32, (2 * blk, win), 1)
    rel = (row % blk) - col
    top = row[:, 0:1] < blk
    low = lax.broadcasted_iota(jnp.int32, (1, LANES), 1) < GQA_DH

    def issue(q, kstart):
        scores = []
        for j in range(GQA_HKV):
            q2 = jnp.concatenate([q[:, 2 * j * LANES:(2 * j + 1) * LANES], q[:, (2 * j + 1) * LANES:(2 * j + 2) * LANES]], axis=0)
            for half in range(2):
                i = 2 * j + half
                s_w = None if kstart is None else _dot_nt(q2, ks_s[i, pl.ds(kstart, win), :])
                scores.append((_dot_nt(q2, kxs_s[i]), s_w))
        return scores

    def finish(scores, kstart, valid):
        outs = []
        for j in range(GQA_HKV):
            acc = jnp.zeros((2 * blk, LANES), F32)
            for half in range(2):
                i = 2 * j + half
                sink = jnp.where(top, sink_ref[4 * j + half] * LOG2E, sink_ref[4 * j + 2 + half] * LOG2E)
                s_c, s_w = scores[i]
                m = jnp.maximum(jnp.max(s_c, axis=-1, keepdims=True), sink)
                if kstart is not None:
                    s_w = jnp.where(valid, s_w, MASK_VALUE)
                    m = jnp.maximum(m, jnp.max(s_w, axis=-1, keepdims=True))
                t = _dot(jnp.exp2(s_c - m).astype(BF), vxs_s[i])
                if kstart is not None:
                    t = t + _dot(jnp.exp2(s_w - m).astype(BF), vs_s[i, pl.ds(kstart, win), :])
                ones_lane = LANES - 1 if half == 0 else 0
                den = t[:, ones_lane:ones_lane + 1] + jnp.exp2(sink - m)
                acc = acc + jnp.where(low if half == 0 else jnp.logical_not(low), t * (1.0 / den), 0.0)
            outs += [acc[0:blk].astype(BF), acc[blk:2 * blk].astype(BF)]
        return jnp.concatenate(outs, axis=1)

    group = 2 if nb % 2 == 0 else 1

    def blocks(g, carry):
        pending = []
        for n in [g * group + u for u in range(group)]:
            q0 = pl.multiple_of(n * blk, blk)
            kstart = pl.multiple_of(jnp.clip((n - 1) * blk, 0, S - win), blk)
            pending.append((q0, kstart, issue(q_ref[pl.ds(q0, blk), :], kstart)))
        for q0, kstart, scores in pending:
            dist = rel + (q0 - kstart)
            valid = (dist <= WINDOW) & (dist >= -WINDOW)
            o_ref[pl.ds(q0, blk), :] = finish(scores, kstart, valid)
        return carry

    lax.fori_loop(0, nb // group, blocks, 0)
    if ctx_queries:
        for n in range(Lc // blk):
            ox_ref[n * blk:(n + 1) * blk, :] = finish(issue(qx_ref[n * blk:(n + 1) * blk, :], None), None, None)


def _gqa(q, k, v, sink, lay, ctx_queries):
    B, S, Lc = lay.B, lay.S, lay.Lc
    W = GQA_HQ * GQA_DH
    lat = lambda w: pl.BlockSpec((S, w), lambda b: (lay.lat_seq_block + b, 0))
    ctx = lambda w: pl.BlockSpec((Lc, w), lambda b: (b, 0))
    in_specs = [pl.BlockSpec(memory_space=pltpu.SMEM), lat(W), lat(LANES), lat(LANES), ctx(LANES), ctx(LANES)]
    out_specs = [pl.BlockSpec((S, W), lambda b: (b, 0))]
    out_shape = [jax.ShapeDtypeStruct((lay.n_lat, W), BF)]
    args = [sink, q, k, v, k, v]
    if ctx_queries:
        in_specs.append(ctx(W))
        out_specs.append(ctx(W))
        out_shape.append(jax.ShapeDtypeStruct((lay.n_ctx, W), BF))
        args.append(q)
    return pl.pallas_call(
        functools.partial(_gqa_kernel, S=S, Lc=Lc, ctx_queries=ctx_queries),
        grid=(B,),
        in_specs=in_specs,
        out_specs=out_specs,
        out_shape=out_shape,
        scratch_shapes=[pltpu.VMEM((4, S, LANES), BF), pltpu.VMEM((4, S, LANES), BF),
                        pltpu.VMEM((4, Lc, LANES), BF), pltpu.VMEM((4, Lc, LANES), BF)],
        compiler_params=_params(1),
        name="gqa_window",
    )(*args)


def _mla_heads(q_ref, kv_refs, o_ref):
    lane = lax.broadcasted_iota(jnp.int32, (1, LANES), 1)
    low = lane < MLA_DV

    def head_scores(h):
        hsl = slice(h * LANES, (h + 1) * LANES)
        qh = q_ref[:, hsl]
        return [_dot_nt(qh, k_ref[:, hsl]) for k_ref, _ in kv_refs]

    ahead = 2
    upcoming = [head_scores(h) for h in range(ahead)]
    for pair in range(MLA_H // 2):
        vsl = slice(pair * LANES, (pair + 1) * LANES)
        acc = None
        for half in range(2):
            h = 2 * pair + half
            scores = upcoming.pop(0)
            if h + ahead < MLA_H:
                upcoming.append(head_scores(h + ahead))
            m = functools.reduce(jnp.maximum, [jnp.max(s, axis=-1, keepdims=True) for s in scores])
            keep = low if half == 0 else jnp.logical_not(low)
            ones_lane = LANES - 1 if half == 0 else 0
            t = None
            for s, (_, v_ref) in zip(scores, kv_refs):
                vh = jnp.where(keep, v_ref[:, vsl], jnp.zeros((), BF))
                vh = jnp.where(lane == ones_lane, jnp.ones((), BF), vh)
                u = _dot(jnp.exp2(s - m).astype(BF), vh)
                t = u if t is None else t + u
            o = jnp.where(keep, t * (1.0 / t[:, ones_lane:ones_lane + 1]), 0.0)
            acc = o if acc is None else acc + o
        o_ref[:, vsl] = acc.astype(BF)


def _mla_lat_kernel(q_ref, kc_ref, kl_ref, vc_ref, vl_ref, o_ref):
    _mla_heads(q_ref, ((kc_ref, vc_ref), (kl_ref, vl_ref)), o_ref)


def _mla_ctx_kernel(q_ref, kc_ref, vc_ref, o_ref):
    _mla_heads(q_ref, ((kc_ref, vc_ref),), o_ref)


def _mla_latent(q, k, v, lay):
    B, S, Lc = lay.B, lay.S, lay.Lc
    tq = min(512, S)
    nq = S // tq
    lat_q0 = lay.n_ctx // tq
    kw, vw = MLA_H * LANES, MLA_H * MLA_DV
    return pl.pallas_call(
        _mla_lat_kernel,
        grid=(B, nq),
        in_specs=[pl.BlockSpec((tq, kw), lambda b, i: (lat_q0 + b * nq + i, 0)),
                  pl.BlockSpec((Lc, kw), lambda b, i: (b, 0)),
                  pl.BlockSpec((S, kw), lambda b, i: (lay.lat_seq_block + b, 0)),
                  pl.BlockSpec((Lc, vw), lambda b, i: (b, 0)),
                  pl.BlockSpec((S, vw), lambda b, i: (lay.lat_seq_block + b, 0))],
        out_specs=pl.BlockSpec((tq, vw), lambda b, i: (b * nq + i, 0)),
        out_shape=jax.ShapeDtypeStruct((lay.n_lat, vw), BF),
        compiler_params=_params(2),
        name="mla_latent",
    )(q, k, k, v, v)


def _mla_context(q, k, v, lay):
    B, Lc = lay.B, lay.Lc
    kw, vw = MLA_H * LANES, MLA_H * MLA_DV
    blk = lambda w: pl.BlockSpec((Lc, w), lambda b: (b, 0))
    return pl.pallas_call(
        _mla_ctx_kernel,
        grid=(B,),
        in_specs=[blk(kw), blk(kw), blk(vw)],
        out_specs=blk(vw),
        out_shape=jax.ShapeDtypeStruct((lay.n_ctx, vw), BF),
        compiler_params=_params(1),
        name="mla_context",
    )(q, k, v)


def _ret_chunk(Lc, S):
    return 256 if Lc % 256 == 0 and S % 256 == 0 else RET_CHUNK


def _ret_kernel(dec_ref, qc_ref, ql_ref, kc_ref, kl_ref, vc_ref, vl_ref, gc_ref, gl_ref, gn_ref,
                oc_ref, ol_ref, kv_s, r_s, *, Lc, S, ch):
    nc, nl = Lc // ch, S // ch
    n = nc + nl
    lg_f = jnp.log(_sigmoid(dec_ref[0, 0:1, :]))
    lg_b = jnp.log(_sigmoid(dec_ref[0, 1:2, :]))
    pos_i = lax.broadcasted_iota(jnp.int32, (ch, ch), 0).astype(F32)
    pos_j = lax.broadcasted_iota(jnp.int32, (ch, ch), 1).astype(F32)
    diff = pos_i - pos_j
    fwd = diff >= 0.0
    mask = jnp.where(fwd, jnp.exp(jnp.where(fwd, diff, 0.0) * lg_f), jnp.exp(jnp.where(fwd, 0.0, -diff) * lg_b))
    pos = pos_i[:, 0:1]
    lf, lb = lg_f[:, :LANES], lg_b[:, :LANES]
    zeta = (jnp.exp((ch - 1.0 - pos) * lf), jnp.exp(pos * lb))
    xi = (jnp.exp((pos + 1.0) * lf), jnp.exp((ch - pos) * lb))
    decay = (jnp.exp(ch * lf), jnp.exp(ch * lb))

    def rows(c_ref, l_ref, c):
        if c < nc:
            return c_ref[c * ch:(c + 1) * ch, :]
        return l_ref[(c - nc) * ch:(c - nc + 1) * ch, :]

    for c in range(n):
        k = rows(kc_ref, kl_ref, c).astype(F32)
        kz = jnp.concatenate([(k * zeta[0]).astype(BF), (k * zeta[1]).astype(BF)], axis=1)
        kv_s[c] = _dot_tn(kz, rows(vc_ref, vl_ref, c))

    orders = (list(range(n)), list(range(nc - 1, -1, -1)) + list(range(n - 1, nc - 1, -1)))
    for d, order in enumerate(orders):
        R = jnp.zeros((RET_DK, RET_DV), F32)
        for c in order:
            r_s[c, d * RET_DK:(d + 1) * RET_DK, :] = R.astype(BF)
            R = decay[d] * R + kv_s[c, d * RET_DK:(d + 1) * RET_DK, :]

    qk = lambda c: _dot_nt(rows(qc_ref, ql_ref, c), rows(kc_ref, kl_ref, c))
    qk_next = qk(0)
    for c in range(n):
        q = rows(qc_ref, ql_ref, c)
        s = qk_next * mask
        if c + 1 < n:
            qk_next = qk(c + 1)
        qf = q.astype(F32)
        qx = jnp.concatenate([(qf * xi[0]).astype(BF), (qf * xi[1]).astype(BF)], axis=1)
        y = _dot(s.astype(BF), rows(vc_ref, vl_ref, c)) + _dot(qx, r_s[c])
        mu = jnp.mean(y, axis=-1, keepdims=True)
        yc = y - mu
        var = jnp.mean(yc * yc, axis=-1, keepdims=True)
        yn = yc * lax.rsqrt(var + LN_EPS) * gn_ref[...]
        hg = 0.5 * rows(gc_ref, gl_ref, c).astype(F32)
        out = ((hg + hg * jnp.tanh(hg)) * yn).astype(BF)
        if c < nc:
            oc_ref[c * ch:(c + 1) * ch, :] = out
        else:
            ol_ref[(c - nc) * ch:(c - nc + 1) * ch, :] = out


def _retention(dq, dk, dv, dg, decay_lanes, gn, lay):
    B, S, Lc = lay.B, lay.S, lay.Lc
    ch = _ret_chunk(Lc, S)
    n = (Lc + S) // ch
    ctx = pl.BlockSpec((Lc, LANES), lambda b, h: (b, h))
    lat = pl.BlockSpec((S, LANES), lambda b, h: (lay.lat_seq_block + b, h))
    return pl.pallas_call(
        functools.partial(_ret_kernel, Lc=Lc, S=S, ch=ch),
        grid=(B, RET_H),
        in_specs=[pl.BlockSpec((1, 2, ch), lambda b, h: (h, 0, 0)),
                  ctx, lat, ctx, lat, ctx, lat, ctx, lat, pl.BlockSpec((1, LANES), lambda b, h: (0, h))],
        out_specs=[pl.BlockSpec((Lc, LANES), lambda b, h: (b, h)), pl.BlockSpec((S, LANES), lambda b, h: (b, h))],
        out_shape=[jax.ShapeDtypeStruct((lay.n_ctx, RET_H * RET_DV), BF), jax.ShapeDtypeStruct((lay.n_lat, RET_H * RET_DV), BF)],
        scratch_shapes=[pltpu.VMEM((n, 2 * RET_DK, RET_DV), F32), pltpu.VMEM((n, 2 * RET_DK, RET_DV), BF)],
        compiler_params=_params(2),
        name="retention",
    )(decay_lanes, dq, dq, dk, dk, dv, dv, dg, dg, gn)


def _merge_kernel(*refs, n_br, ct):
    x_ref, mod_ref = refs[:2]
    br_refs = refs[2:2 + N_BRANCH * n_br]
    wg_ref, wb_ref, wo_ref, g_ref, beta_ref, o_ref = refs[2 + N_BRANCH * n_br:]
    x = x_ref[...]
    h = (x * (1.0 + mod_ref[0, 4:5, :]) + mod_ref[0, 3:4, :]).astype(BF)
    acc = None
    pre = lambda k: _dot(h, wg_ref[:, k * D_MODEL:(k + 1) * D_MODEL])
    z_next = pre(0)
    for k in range(N_BRANCH):
        z_gate = z_next
        if k + 1 < N_BRANCH:
            z_next = pre(k + 1)
        br = _rows(br_refs[k * n_br:(k + 1) * n_br], pl.program_id(0), ct)
        t = (0.5 * jnp.tanh(z_gate) + 0.5) * _dot(br, wb_ref[k])
        acc = t if acc is None else acc + t
    merged = acc.astype(BF)
    half = x.shape[0] // 2
    ys = [_dot(merged[r * half:(r + 1) * half, :], wo_ref[...]) for r in range(2)]
    for r in range(2):
        rows = slice(r * half, (r + 1) * half)
        z = DN_ALPHA * x[rows, :] + mod_ref[0, 5:6, :] * ys[r]
        o_ref[rows, :] = _layer_norm(z, g_ref[...], beta_ref[...])


def _merge(x, mod, branches, w_gate, wb_all, wo_all, l, g, beta, lay, g_off):
    tm = lay.tm
    n = lay.tiles - g_off
    br_specs, br_args = [], []
    for br in branches:
        specs, args = _row_specs(br, BRANCH_W, lay, g_off)
        br_specs += specs
        br_args += args
    return pl.pallas_call(
        functools.partial(_merge_kernel, n_br=len(br_args) // N_BRANCH, ct=lay.ctx_tiles),
        grid=(n,),
        in_specs=[pl.BlockSpec((tm, D_MODEL), lambda i: (i + g_off, 0)),
                  pl.BlockSpec((1, 9, D_MODEL), lambda i: (lay.mod_index(i + g_off), 0, 0))] + br_specs + [
                  pl.BlockSpec((None,) + w_gate.shape[1:], lambda i: (l, 0, 0), pipeline_mode=pl.Buffered(1)),
                  pl.BlockSpec((None, N_BRANCH, BRANCH_W, D_MODEL), lambda i: (l, 0, 0, 0), pipeline_mode=pl.Buffered(1)),
                  pl.BlockSpec((None, D_MODEL, D_MODEL), lambda i: (l, 0, 0), pipeline_mode=pl.Buffered(1)),
                  _const((1, D_MODEL)), _const((1, D_MODEL))],
        out_specs=pl.BlockSpec((tm, D_MODEL), lambda i: (i, 0)),
        out_shape=jax.ShapeDtypeStruct((n * tm, D_MODEL), F32),
        compiler_params=_params(1),
        name="gated_merge",
    )(x, mod, *br_args, w_gate, wb_all, wo_all, g, beta)


def _rope_tables(S, head_dim, lead, reps, tail, tm):
    nf = head_dim // 4
    inv = ROPE_BASE ** (-np.arange(nf, dtype=np.float64) / nf)
    n_rows = S // GRID_W
    ar = np.arange(n_rows, dtype=np.float64)[:, None] * inv[None, :]
    ac = np.arange(GRID_W, dtype=np.float64)[:, None] * inv[None, :]

    def lanes(by_row, first, second, fill):
        n = first.shape[0]
        neutral = np.full((n, nf), fill)
        head = np.concatenate([first, second, neutral, neutral] if by_row else [neutral, neutral, first, second], -1)
        pad = lambda w: np.full((n, w), fill)
        return np.tile(np.concatenate([pad(lead), head, pad(tail)], -1), (1, reps))

    row_c = lanes(True, np.cos(ar), np.cos(ar), 1.0)
    row_s = lanes(True, -np.sin(ar), np.sin(ar), 0.0)
    col_c = lanes(False, np.cos(ac), np.cos(ac), 1.0)
    col_s = lanes(False, -np.sin(ac), np.sin(ac), 0.0)
    W = row_c.shape[1]
    per_tile = tm // GRID_W

    def by_tile(tab, fill):
        t = tab.reshape(S // tm, per_tile, W)
        t = np.concatenate([t, np.full((S // tm, SUBLANES - per_tile, W), fill)], 1).reshape(-1, W)
        return np.concatenate([t, np.full((SUBLANES, W), fill)], 0).astype(np.float32)

    ident = lambda fill: np.full((GRID_W, W), fill)
    return (by_tile(row_c, 1.0), by_tile(row_s, 0.0),
            np.stack([col_c, ident(1.0)]).astype(np.float32), np.stack([col_s, ident(0.0)]).astype(np.float32))


def _mixer_weights(w_in, lru_w_a, lru_b_a, lru_w_x, lru_b_x, mla_w_uq, mla_w_ukv):
    wb = w_in.astype(BF)
    cut = sum(IN_SIZES[:8])
    gate0 = sum(IN_SIZES[:12])
    assert cut + LANES - MLA_DR == _W_HEAD
    w_proj = (wb, wb[:, :, cut:gate0])
    w_gate = wb[:, :, gate0:] * jnp.asarray(0.5, BF)
    blocks = jnp.stack([lru_w_a[:, 0], lru_w_x[:, 0], lru_w_a[:, 1], lru_w_x[:, 1]], axis=1)
    eye = jnp.eye(LRU_BLOCKS, dtype=F32)
    dense = (0.5 * eye)[None, None, :, None, :, None] * blocks[:, :, :, :, None, :]
    wg = dense.reshape(DEPTH, 4, LRU_W, LRU_W).transpose(0, 2, 1, 3).reshape(DEPTH, LRU_W, 4 * LRU_W).astype(BF)
    bg = 0.5 * jnp.stack([lru_b_a[:, 0], lru_b_x[:, 0], lru_b_a[:, 1], lru_b_x[:, 1]], axis=1).reshape(DEPTH, 1, 4 * LRU_W)
    uq = mla_w_uq.reshape(DEPTH, MLA_RQ, MLA_H, MLA_DN + MLA_DR)
    wq = jnp.pad(uq, ((0, 0), (0, 0), (0, 0), (0, LANES - MLA_DN - MLA_DR))).reshape(DEPTH, MLA_RQ, MLA_H * LANES).astype(BF)
    ukv = mla_w_ukv.reshape(DEPTH, MLA_RKV, MLA_H, MLA_DN + MLA_DV)
    wk = jnp.pad(ukv[..., :MLA_DN], ((0, 0), (0, 0), (0, 0), (0, LANES - MLA_DN))).reshape(DEPTH, MLA_RKV, MLA_H * LANES).astype(BF)
    wv = ukv[..., MLA_DN:].reshape(DEPTH, MLA_RKV, MLA_H * MLA_DV).astype(BF)
    return w_proj, w_gate, wg, bg, wq, wk, wv


def kernel(x, c, ctx, c_ctx, w_mod, b_mod, ln_g, ln_b, ffn_w_in, ffn_w_out, w_in, lru_conv_w, lru_conv_b, lru_w_a, lru_b_a, lru_w_x, lru_b_x, lru_lambda, gqa_sink, mla_q_norm, mla_kv_norm, mla_w_uq, mla_w_ukv, ret_decay, ret_gn_g, w_branch, w_out):
    B, S, D = x.shape
    Lc = ctx.shape[1]
    lay = _Layout(B, S, Lc)
    tm = lay.tm

    n_cond = -(-(B + 1) // SUBLANES) * SUBLANES
    cond = jnp.concatenate([c, c_ctx[None, :], jnp.zeros((n_cond - B - 1, D), F32)], axis=0)
    mod_all = _modulation(cond, w_mod, b_mod)[:, :B + 1].reshape(DEPTH, B + 1, 9, D)

    tabs = [_rope_tables(S, head_dim, lead, reps, tail, tm) for head_dim, lead, reps, tail in
            ((GQA_DH, 0, GQA_HQ, 0), (GQA_DH, 0, GQA_HKV, 0), (MLA_DR, 0, 1, LANES - MLA_DR), (RET_DK, 0, RET_H, 0),
             (MLA_DR, MLA_DN, MLA_H, LANES - MLA_DN - MLA_DR))]

    tok = (ctx.reshape(B * Lc, D), x.reshape(B * S, D))
    row = lambda v: v[None, :]
    ct = lay.ctx_tiles
    w1_all, w2_all = ffn_w_in.astype(BF), ffn_w_out.astype(BF)
    wb_all, wo_all = w_branch.astype(BF), w_out.astype(BF)
    w_proj, w_gate, wg, bg, wq, wk, wv = _mixer_weights(w_in, lru_w_a, lru_b_a, lru_w_x, lru_b_x, mla_w_uq, mla_w_ukv)

    for l in range(DEPTH):
        last = l == DEPTH - 1
        mod = mod_all[l]
        dec = jnp.broadcast_to(ret_decay[l].T[:, :, None], (RET_H, 2, _ret_chunk(Lc, S)))

        tok = _half_ffn(tok, mod, w1_all, w2_all, l, 0, row(ln_g[l, 0]), row(ln_b[l, 0]), lay, 0, 0)

        ax, ay, bq, bk, bv, mq, mk, mv, dq, dk, dv, dg = _mixer_proj(
            tok, mod, w_proj, l, wq[l], wk[l], wv[l], row(mla_q_norm[l]), row(mla_kv_norm[l]), tabs, lay)
        a_c, a_l = _lru(ax, ay, lru_conv_w[l], row(lru_conv_b[l]), wg[l], bg[l], lru_lambda[l], lay)
        b_out = _gqa(bq, bk, bv, gqa_sink[l], lay, ctx_queries=not last)
        c_l = _mla_latent(mq, mk, mv, lay)
        d_c, d_l = _retention(dq, dk, dv, dg, dec, row(ret_gn_g[l]), lay)

        if last:
            branches = (a_l, b_out[0], c_l, d_l)
            g_off = ct
        else:
            branches = ((a_c, a_l), (b_out[1], b_out[0]), (_mla_context(mq, mk, mv, lay), c_l), (d_c, d_l))
            g_off = 0
        tok = _merge(tok, mod, branches, w_gate, wb_all, wo_all, l, row(ln_g[l, 1]), row(ln_b[l, 1]), lay, g_off)
        tok = _half_ffn(tok, mod, w1_all, w2_all, l, 1, row(ln_g[l, 2]), row(ln_b[l, 2]), lay, 6, g_off)

    return tok.reshape(B, S, D)
```

```python
import functools

import jax
import jax.numpy as jnp
import numpy as np
from jax import lax
from jax.experimental import pallas as pl
from jax.experimental.pallas import tpu as pltpu

D_MODEL = 1024
DEPTH = 2
GRID_W = 64
N_BRANCH = 4
BRANCH_W = 512
LRU_W = 512
LRU_BLOCKS = 8
LRU_BW = LRU_W // LRU_BLOCKS
LRU_C = 8.0
GQA_HQ = 8
GQA_HKV = 2
GQA_DH = 64
WINDOW = 128
ATT_BLOCK = 128
MLA_H = 8
MLA_RQ = 384
MLA_RKV = 256
MLA_DN = 64
MLA_DR = 32
MLA_DV = 64
RET_H = 4
RET_DK = 128
RET_DV = 128
RET_CHUNK = 128
FF = 2816
ROPE_BASE = 10000.0
LN_EPS = 1e-5
MASK_VALUE = -1e30
DN_ALPHA = (2 * DEPTH) ** 0.25
IN_SIZES = (LRU_W, LRU_W, GQA_HQ * GQA_DH, GQA_HKV * GQA_DH, GQA_HKV * GQA_DH, MLA_RQ, MLA_RKV, MLA_DR,
            RET_H * RET_DK, RET_H * RET_DK, RET_H * RET_DV, RET_H * RET_DV, N_BRANCH * D_MODEL)

LANES = 128
SUBLANES = 8
VMEM_LIMIT = 56 * 1024 * 1024
MXU_DIM = 256
FFN_CHUNKS = (6 * MXU_DIM, 5 * MXU_DIM)
LOG2E = 1.4426950408889634
BF = jnp.bfloat16
F32 = jnp.float32


def _params(n_axes):
    return pltpu.CompilerParams(dimension_semantics=("arbitrary",) * n_axes, vmem_limit_bytes=VMEM_LIMIT)


def _resident(shape):
    nd = len(shape)
    return pl.BlockSpec(shape, lambda *_: (0,) * nd, pipeline_mode=pl.Buffered(1))


def _const(shape):
    nd = len(shape)
    return pl.BlockSpec(shape, lambda *_: (0,) * nd)


def _layer_norm(z, g, b):
    mu = jnp.mean(z, axis=-1, keepdims=True)
    zc = z - mu
    var = jnp.mean(zc * zc, axis=-1, keepdims=True)
    return zc * lax.rsqrt(var + LN_EPS) * g + b


def _sigmoid(x):
    return 1.0 / (1.0 + jnp.exp(-x))


def _dot(a, b):
    return jnp.dot(a, b, preferred_element_type=F32)


def _dot_nt(a, b):
    return lax.dot_general(a, b, (((1,), (1,)), ((), ())), preferred_element_type=F32)


def _dot_tn(a, b):
    return lax.dot_general(a, b, (((0,), (0,)), ((), ())), preferred_element_type=F32)


def _rope(y, c, s, nf):
    w = y.shape[-1]
    lane = lax.broadcasted_iota(jnp.int32, y.shape, 1)
    partner = jnp.where((lane % (2 * nf)) < nf, pltpu.roll(y, w - nf, axis=1), pltpu.roll(y, nf, axis=1))
    return y * c + partner * s


class _Layout:
    def __init__(self, B, S, Lc):
        self.B, self.S, self.Lc = B, S, Lc
        self.n_ctx, self.n_lat = B * Lc, B * S
        self.n_tok = self.n_ctx + self.n_lat
        tm = 512
        while S % tm or self.n_ctx % tm:
            tm //= 2
        self.tm = tm
        self.ctx_tiles = self.n_ctx // tm
        self.tiles_per_seq = S // tm
        self.tiles = self.ctx_tiles + B * self.tiles_per_seq
        assert self.n_ctx % S == 0, "latent rows must start on a whole-sequence block"
        self.lat_seq_block = self.n_ctx // S
        self.lat_att_block = self.n_ctx // ATT_BLOCK

    def mod_index(self, g):
        return jnp.where(g < self.ctx_tiles, self.B, (g - self.ctx_tiles) // self.tiles_per_seq)

    def rope_index(self, g):
        return jnp.where(g < self.ctx_tiles, self.tiles_per_seq, (g - self.ctx_tiles) % self.tiles_per_seq)


def _mod_kernel(s_ref, w_ref, b_ref, o_ref):
    s = s_ref[...]
    s = s * _sigmoid(s)
    o_ref[0] = _dot(s.astype(BF), w_ref[0].astype(BF)) + b_ref[0]


def _modulation(cond, w_mod, b_mod):
    R = cond.shape[0]
    tn = 1024
    n9 = 9 * D_MODEL
    return pl.pallas_call(
        _mod_kernel,
        grid=(DEPTH, n9 // tn),
        in_specs=[pl.BlockSpec((R, D_MODEL), lambda l, j: (0, 0)),
                  pl.BlockSpec((1, D_MODEL, tn), lambda l, j: (l, 0, j)),
                  pl.BlockSpec((1, 1, tn), lambda l, j: (l, 0, j))],
        out_specs=pl.BlockSpec((1, R, tn), lambda l, j: (l, 0, j)),
        out_shape=jax.ShapeDtypeStruct((DEPTH, R, n9), F32),
        compiler_params=_params(2),
        name="modulation",
    )(cond, w_mod, b_mod.reshape(DEPTH, 1, n9))


def _row_specs(src, width, lay, g_off, tile=lambda i: i):
    tm, ct = lay.tm, lay.ctx_tiles
    if isinstance(src, tuple):
        assert g_off == 0
        return [pl.BlockSpec((tm, width), lambda i: (jnp.minimum(tile(i), ct - 1), 0)),
                pl.BlockSpec((tm, width), lambda i: (jnp.maximum(tile(i) - ct, 0), 0))], list(src)
    assert src.shape[0] == lay.n_tok or (src.shape[0] == lay.n_lat and g_off == ct)
    off = g_off if src.shape[0] == lay.n_tok else 0
    return [pl.BlockSpec((tm, width), lambda i: (tile(i) + off, 0))], [src]


def _rows(refs, t, ct):
    if len(refs) == 2:
        return jnp.where(t < ct, refs[0][...], refs[1][...])
    return refs[0][...]


def _ffn_kernel(*refs, k0, n_x, ct):
    mod_ref, w1_ref, w2_ref, g_ref, b_ref, o_ref = refs[n_x:]
    x = _rows(refs[:n_x], pl.program_id(0), ct)
    shift = mod_ref[0, k0:k0 + 1, :]
    scale = mod_ref[0, k0 + 1:k0 + 2, :]
    gate = mod_ref[0, k0 + 2:k0 + 3, :]
    xm = (x * (1.0 + scale) + shift).astype(BF)
    assert sum(FFN_CHUNKS) == FF
    lo = 0
    up = []
    for width in FFN_CHUNKS:
        up.append((lo, width, _dot(xm, w1_ref[:, lo:lo + width]), _dot(xm, w1_ref[:, FF + lo:FF + lo + width])))
        lo += width
    hidden = [(lo, width, (a * _sigmoid(a) * b).astype(BF)) for lo, width, a, b in up]
    cut = x.shape[0] // 2
    parts = (slice(0, cut), slice(cut, x.shape[0]))
    down = [sum(_dot(h[rows, :], w2_ref[lo:lo + width, :]) for lo, width, h in hidden) for rows in parts]
    for rows, y in zip(parts, down):
        z = DN_ALPHA * x[rows, :] + (0.5 * gate) * y
        o_ref[rows, :] = _layer_norm(z, g_ref[...], b_ref[...])


def _half_ffn(x, mod, w1_all, w2_all, l, j, g, b, lay, k0, g_off):
    tm = lay.tm
    n = lay.tiles - g_off
    x_specs, x_args = _row_specs(x, D_MODEL, lay, g_off)
    pick = lambda *_: (l, j, 0, 0)
    return pl.pallas_call(
        functools.partial(_ffn_kernel, k0=k0, n_x=len(x_args), ct=lay.ctx_tiles),
        grid=(n,),
        in_specs=x_specs + [
            pl.BlockSpec((1, 9, D_MODEL), lambda i: (lay.mod_index(i + g_off), 0, 0)),
            pl.BlockSpec((None, None, D_MODEL, 2 * FF), pick, pipeline_mode=pl.Buffered(1)),
            pl.BlockSpec((None, None, FF, D_MODEL), pick, pipeline_mode=pl.Buffered(1)),
            _const((1, D_MODEL)), _const((1, D_MODEL))],
        out_specs=pl.BlockSpec((tm, D_MODEL), lambda i: (i, 0)),
        out_shape=jax.ShapeDtypeStruct((n * tm, D_MODEL), F32),
        compiler_params=_params(1),
        name="half_ffn",
    )(*x_args, mod, w1_all, w2_all, g, b)


_W_LRU, _W_GQ, _W_GKV, _W_RET = LRU_W, GQA_HQ * GQA_DH, GQA_HKV * GQA_DH, RET_H * RET_DK
_OFF_AX, _OFF_AY, _OFF_BQ, _OFF_BKV = 0, 512, 1024, 1536
_OFF_MLA, _W_MLA = 1792, MLA_RQ + MLA_RKV + LANES
_W_HEAD = _OFF_MLA + _W_MLA
_OFF_DQ, _OFF_DK, _OFF_DV, _OFF_DG = 0, 512, 1024, 1536
_PROJ_OUT_WIDTHS = (_W_LRU, _W_LRU, _W_GQ, _W_GKV, _W_GKV, MLA_H * LANES, MLA_H * LANES, MLA_H * MLA_DV,
                    _W_RET, _W_RET, _W_RET, _W_RET)
_N_ROPE_TABLES = 5


def _rope_tile(y, tab, nf):
    rc_ref, rs_ref, cc_ref, cs_ref = tab
    parts = []
    for g in range(y.shape[0] // GRID_W):
        c = rc_ref[g:g + 1, :] * cc_ref[...]
        s = rs_ref[g:g + 1, :] + cs_ref[...]
        parts.append(_rope(y[g * GRID_W:(g + 1) * GRID_W, :], c, s, nf))
    return jnp.concatenate(parts, axis=0)


def _rms(x, g):
    return x * lax.rsqrt(jnp.mean(x * x, axis=-1, keepdims=True) + LN_EPS) * g


def _proj_kernel(x_ref, mod_ref, w_ref, wt_ref, wq_ref, wk_ref, wv_ref, gq_ref, gkv_ref, *refs):
    tabs = [refs[4 * i:4 * i + 4] for i in range(_N_ROPE_TABLES)]
    t_gq, t_gk, t_kr, t_ret, t_mq = tabs
    ax_o, ay_o, bq_o, bk_o, bv_o, mq_o, mk_o, mv_o, dq_o, dk_o, dv_o, dg_o = refs[4 * _N_ROPE_TABLES:]
    x = x_ref[...]
    h = (x * (1.0 + mod_ref[0, 4:5, :]) + mod_ref[0, 3:4, :]).astype(BF)
    proj = lambda lo, width: _dot(h, w_ref[:, lo:lo + width])
    tail = lambda lo, width: _dot(h, wt_ref[:, lo:lo + width])

    c = proj(_OFF_MLA, _W_MLA)
    y_bq = proj(_OFF_BQ, _W_GQ)
    yq = _rms(c[:, :MLA_RQ], gq_ref[...]).astype(BF)
    ykv = _rms(c[:, MLA_RQ:MLA_RQ + MLA_RKV], gkv_ref[...]).astype(BF)
    y_mq = _dot(yq, wq_ref[...])
    bq_o[...] = (_rope_tile(y_bq, t_gq, GQA_DH // 4) * (GQA_DH ** -0.5 * LOG2E)).astype(BF)
    y_mk = _dot(ykv, wk_ref[...])
    mq_o[...] = (_rope_tile(y_mq, t_mq, MLA_DR // 4) * ((MLA_DN + MLA_DR) ** -0.5 * LOG2E)).astype(BF)
    y_mv = _dot(ykv, wv_ref[...])
    kr = c[:, MLA_RQ + MLA_RKV:]
    kr = jnp.where(lax.broadcasted_iota(jnp.int32, kr.shape, 1) < MLA_DR, kr, 0.0)
    kr = _rope_tile(kr, t_kr, MLA_DR // 4)
    kr_all = pltpu.roll(jnp.concatenate([kr] * MLA_H, axis=1), MLA_DN, axis=1)
    mk_o[...] = (y_mk + kr_all).astype(BF)
    y_dq = tail(_OFF_DQ, _W_RET)
    mv_o[...] = y_mv.astype(BF)
    y_dk = tail(_OFF_DK, _W_RET)
    dq_o[...] = _rope_tile(y_dq, t_ret, RET_DK // 4).astype(BF)
    kv = proj(_OFF_BKV, 2 * _W_GKV)
    dk_o[...] = (_rope_tile(y_dk, t_ret, RET_DK // 4) * RET_DK ** -0.5).astype(BF)
    y_ax = proj(_OFF_AX, _W_LRU)
    bk_o[...] = _rope_tile(kv[:, :_W_GKV], t_gk, GQA_DH // 4).astype(BF)
    bv_o[...] = kv[:, _W_GKV:].astype(BF)
    y_ay = proj(_OFF_AY, _W_LRU)
    ax_o[...] = y_ax.astype(BF)
    y_dv = tail(_OFF_DV, _W_RET)
    ay_o[...] = y_ay.astype(BF)
    y_dg = tail(_OFF_DG, _W_RET)
    dv_o[...] = y_dv.astype(BF)
    dg_o[...] = y_dg.astype(BF)


def _rope_specs(tab, lay):
    w = tab[0].shape[1]
    row = pl.BlockSpec((SUBLANES, w), lambda i: (lay.rope_index(i), 0))
    col = pl.BlockSpec((None, GRID_W, w), lambda i: (jnp.where(i < lay.ctx_tiles, 1, 0), 0, 0))
    return [row, row, col, col]


def _mixer_proj(x, mod, w_all, l, wq, wk, wv, gq, gkv, tabs, lay):
    tm = lay.tm
    row = lambda i: (i, 0)
    w_head, w_tail = w_all
    assert len(tabs) == _N_ROPE_TABLES
    tab_specs, tab_args = [], []
    for tab in tabs:
        tab_specs += _rope_specs(tab, lay)
        tab_args += list(tab)
    return pl.pallas_call(
        _proj_kernel,
        grid=(lay.tiles,),
        in_specs=[pl.BlockSpec((tm, D_MODEL), row),
                  pl.BlockSpec((1, 9, D_MODEL), lambda i: (lay.mod_index(i), 0, 0)),
                  pl.BlockSpec((None, D_MODEL, _W_HEAD), lambda i: (l, 0, 0), pipeline_mode=pl.Buffered(1)),
                  pl.BlockSpec((None,) + w_tail.shape[1:], lambda i: (l, 0, 0), pipeline_mode=pl.Buffered(1)),
                  _const(wq.shape), _const(wk.shape), _const(wv.shape), _const(gq.shape), _const(gkv.shape)] + tab_specs,
        out_specs=[pl.BlockSpec((tm, w), row) for w in _PROJ_OUT_WIDTHS],
        out_shape=[jax.ShapeDtypeStruct((lay.n_tok, w), BF) for w in _PROJ_OUT_WIDTHS],
        compiler_params=_params(1),
        name="mixer_proj",
    )(x, mod, w_head, w_tail, wq, wk, wv, gq, gkv, *tab_args)


def _conv4(xp_s, base, n, w_ref, b_ref):
    y = b_ref[...] + w_ref[2:3, :] * xp_s[base:base + n, :]
    y = y + w_ref[0:1, :] * xp_s[base - 2:base - 2 + n, :]
    y = y + w_ref[1:2, :] * xp_s[base - 1:base - 1 + n, :]
    y = y + w_ref[3:4, :] * xp_s[base + 1:base + 1 + n, :]
    return y


def _scan8(a, b, reverse):
    row = lax.broadcasted_iota(jnp.int32, a.shape, 0)
    for s in (1, 2, 4):
        if reverse:
            keep = row < SUBLANES - s
            a_sh = pltpu.roll(a, SUBLANES - s, axis=0)
            b_sh = pltpu.roll(b, SUBLANES - s, axis=0)
        else:
            keep = row >= s
            a_sh = pltpu.roll(a, s, axis=0)
            b_sh = pltpu.roll(b, s, axis=0)
        b = jnp.where(keep, b + a * b_sh, b)
        a = jnp.where(keep, a * a_sh, a)
    return a, b


def _lru_segment(T):
    for seg in (36, 44, 28, 20, 12, 52, 60, 4, 8, 16, 32):
        if T % (SUBLANES * seg) == 0:
            return seg
    raise ValueError(f"no scan segment length for {T} rows")


def _lru_kernel(axc_ref, axl_ref, ayc_ref, ayl_ref, cw_ref, cb_ref, wg_ref, bg_ref, lam_ref,
                oc_ref, ol_ref, xp_s, xa_s, a_s, b_s, h0_s, h1_s, ac_s, bc_s, *, Lc, S, rows, seg):
    W = LRU_W
    T = Lc + S
    NS = W // LANES
    G = SUBLANES * seg
    P = SUBLANES
    zpad = jnp.zeros((P, W), F32)
    xp_s[0:P, :] = zpad
    xp_s[P:P + Lc, :] = axc_ref[...].astype(F32)
    xp_s[P + Lc:2 * P + Lc, :] = zpad
    xp_s[2 * P + Lc:2 * P + T, :] = axl_ref[...].astype(F32)
    xp_s[2 * P + T:3 * P + T, :] = zpad
    xa_s[0:Lc, :] = _conv4(xp_s, P, Lc, cw_ref, cb_ref)
    xa_s[Lc:T, :] = _conv4(xp_s, 2 * P + Lc, S, cw_ref, cb_ref)
    row8 = lax.broadcasted_iota(jnp.int32, (SUBLANES, LANES), 0)
    zero8 = jnp.zeros((SUBLANES, LANES), F32)

    for d in range(2):
        reverse = d == 1
        h_s = h1_s if reverse else h0_s
        neg = -lam_ref[d:d + 1, :]
        softplus = jnp.maximum(neg, 0.0) + jnp.log(1.0 + jnp.exp(-jnp.abs(neg)))
        c = (-0.5 * LRU_C * 1.4426950408889634) * softplus

        def gates(i, carry):
            r0 = pl.multiple_of(i * rows, rows)
            dst = pl.multiple_of(jnp.where(r0 < Lc, r0 + S, r0 - Lc), rows) if reverse else r0
            xa = xa_s[pl.ds(r0, rows), :]
            xb = xa.astype(BF)
            lo = 2 * d * W
            t_r = jnp.tanh(_dot(xb, wg_ref[:, lo:lo + W]) + bg_ref[:, lo:lo + W])
            t_i = jnp.tanh(_dot(xb, wg_ref[:, lo + W:lo + 2 * W]) + bg_ref[:, lo + W:lo + 2 * W])
            a = jnp.exp2(c * t_r + c)
            y = 1.0 - a * a
            root = jnp.where(y > 0.0, y * lax.rsqrt(y), 0.0)
            b = (root * (0.5 * xa)) * (t_i + 1.0)
            for k in range(NS):
                a_s[k, pl.ds(dst, rows), :] = a[:, k * LANES:(k + 1) * LANES]
                b_s[k, pl.ds(dst, rows), :] = b[:, k * LANES:(k + 1) * LANES]
            return carry

        lax.fori_loop(0, T // rows, gates, 0, unroll=3 if (T // rows) % 3 == 0 else 1)

        def group(j, h_in):
            base = ((T // G - 1 - j) if reverse else j) * G
            acc_a, acc_b = [None] * NS, [None] * NS
            for i in (range(seg - 1, -1, -1) if reverse else range(seg)):
                for k in range(NS):
                    a = a_s[k, pl.ds(base + i, SUBLANES, stride=seg), :]
                    b = b_s[k, pl.ds(base + i, SUBLANES, stride=seg), :]
                    if acc_a[k] is None:
                        acc_a[k], acc_b[k] = a, b
                    else:
                        acc_b[k] = a * acc_b[k] + b
                        acc_a[k] = a * acc_a[k]
                    ac_s[k, i] = acc_a[k]
                    bc_s[k, i] = acc_b[k]
            h_out, enter = [], []
            for k in range(NS):
                tot_a, tot_b = _scan8(acc_a[k], acc_b[k], reverse)
                after = tot_a * h_in[k] + tot_b
                if reverse:
                    enter.append(jnp.where(row8 == SUBLANES - 1, h_in[k], pltpu.roll(after, SUBLANES - 1, axis=0)))
                    h_out.append(jnp.broadcast_to(after[0:1, :], after.shape))
                else:
                    enter.append(jnp.where(row8 == 0, h_in[k], pltpu.roll(after, 1, axis=0)))
                    h_out.append(jnp.broadcast_to(after[SUBLANES - 1:SUBLANES, :], after.shape))
            for i in range(seg):
                for k in range(NS):
                    h_s[k, pl.ds(base + i, SUBLANES, stride=seg), :] = ac_s[k, i] * enter[k] + bc_s[k, i]
            return tuple(h_out)

        lax.fori_loop(0, T // G, group, (zero8,) * NS)

    def gelu(v):
        c = 0.7978845608028654
        hv = 0.5 * v
        return hv + hv * jnp.tanh(v * (c + (c * 0.044715) * (v * v)))

    for k in range(NS):
        sl = slice(k * LANES, (k + 1) * LANES)
        hc = h0_s[k, 0:Lc, :] + h1_s[k, S:T, :]
        hl = h0_s[k, Lc:T, :] + h1_s[k, 0:S, :]
        oc_ref[:, sl] = (hc * gelu(ayc_ref[:, sl].astype(F32))).astype(BF)
        ol_ref[:, sl] = (hl * gelu(ayl_ref[:, sl].astype(F32))).astype(BF)


def _lru(ax, ay, conv_w, conv_b, wg, bg, lam, lay):
    B, S, Lc = lay.B, lay.S, lay.Lc
    W = LRU_W
    T = Lc + S
    rows = 256
    while Lc % rows or S % rows:
        rows //= 2
    seg = _lru_segment(T)
    ctx = pl.BlockSpec((Lc, W), lambda b: (b, 0))
    lat = pl.BlockSpec((S, W), lambda b: (b + lay.lat_seq_block, 0))
    slabs = pltpu.VMEM((W // LANES, T, LANES), F32)
    part = pltpu.VMEM((W // LANES, seg, SUBLANES, LANES), F32)
    return pl.pallas_call(
        functools.partial(_lru_kernel, Lc=Lc, S=S, rows=rows, seg=seg),
        grid=(B,),
        in_specs=[ctx, lat, ctx, lat, _const((4, W)), _const((1, W)), _const(wg.shape), _const((1, 4 * W)), _const((2, W))],
        out_specs=[pl.BlockSpec((Lc, W), lambda b: (b, 0)), pl.BlockSpec((S, W), lambda b: (b, 0))],
        out_shape=[jax.ShapeDtypeStruct((lay.n_ctx, W), BF), jax.ShapeDtypeStruct((lay.n_lat, W), BF)],
        scratch_shapes=[pltpu.VMEM((T + 3 * SUBLANES, W), F32), pltpu.VMEM((T, W), F32), slabs, slabs, slabs, slabs, part, part],
        compiler_params=_params(1),
        name="rglru",
    )(ax, ax, ay, ay, conv_w, conv_b, wg, bg, lam)


def _split_heads_pair(x, ones_lane):
    xf = x.astype(F32)
    sw = pltpu.roll(xf, GQA_DH, axis=1)
    lane = lax.broadcasted_iota(jnp.int32, xf.shape, 1)
    low = lane < GQA_DH
    lo_fill = jnp.where(lane == LANES - 1, 1.0, 0.0) if ones_lane else jnp.zeros_like(xf)
    hi_fill = jnp.where(lane == 0, 1.0, 0.0) if ones_lane else jnp.zeros_like(xf)
    head0 = (jnp.where(low, xf, lo_fill).astype(BF), jnp.where(low, hi_fill, sw).astype(BF))
    head1 = (jnp.where(low, sw, lo_fill).astype(BF), jnp.where(low, hi_fill, xf).astype(BF))
    return head0, head1


def _gqa_kernel(sink_ref, *refs, S, Lc, ctx_queries):
    if ctx_queries:
        q_ref, k_ref, v_ref, kx_ref, vx_ref, qx_ref, o_ref, ox_ref, ks_s, vs_s, kxs_s, vxs_s = refs
    else:
        q_ref, k_ref, v_ref, kx_ref, vx_ref, o_ref, ks_s, vs_s, kxs_s, vxs_s = refs
    blk = ATT_BLOCK
    nb = S // blk
    win = min(3 * blk, S)
    for src, dst, is_value in ((k_ref, ks_s, False), (v_ref, vs_s, True), (kx_ref, kxs_s, False), (vx_ref, vxs_s, True)):
        heads = _split_heads_pair(src[...], is_value)
        for j in range(GQA_HKV):
            for half in range(2):
                dst[2 * j + half] = heads[j][half]

    row = lax.broadcasted_iota(jnp.int32, (2 * blk, win), 0)
    col = lax.broadcasted_iota(jnp.int32, (2 * blk, win), 1)
    rel = (row % blk) - col
    top = row[:, 0:1] < blk
    low = lax.broadcasted_iota(jnp.int32, (1, LANES), 1) < GQA_DH

    def issue(q, kstart):
        scores = []
        for j in range(GQA_HKV):
            q2 = jnp.concatenate([q[:, 2 * j * LANES:(2 * j + 1) * LANES], q[:, (2 * j + 1) * LANES:(2 * j + 2) * LANES]], axis=0)
            for half in range(2):
                i = 2 * j + half
                s_w = None if kstart is None else _dot_nt(q2, ks_s[i, pl.ds(kstart, win), :])
                scores.append((_dot_nt(q2, kxs_s[i]), s_w))
        return scores

    def finish(scores, kstart, valid):
        outs = []
        for j in range(GQA_HKV):
            acc = jnp.zeros((2 * blk, LANES), F32)
            for half in range(2):
                i = 2 * j + half
                sink = jnp.where(top, sink_ref[4 * j + half] * LOG2E, sink_ref[4 * j + 2 + half] * LOG2E)
                s_c, s_w = scores[i]
                m = jnp.maximum(jnp.max(s_c, axis=-1, keepdims=True), sink)
                if kstart is not None:
                    s_w = jnp.where(valid, s_w, MASK_VALUE)
                    m = jnp.maximum(m, jnp.max(s_w, axis=-1, keepdims=True))
                t = _dot(jnp.exp2(s_c - m).astype(BF), vxs_s[i])
                if kstart is not None:
                    t = t + _dot(jnp.exp2(s_w - m).astype(BF), vs_s[i, pl.ds(kstart, win), :])
                ones_lane = LANES - 1 if half == 0 else 0
                den = t[:, ones_lane:ones_lane + 1] + jnp.exp2(sink - m)
                acc = acc + jnp.where(low if half == 0 else jnp.logical_not(low), t * (1.0 / den), 0.0)
            outs += [acc[0:blk].astype(BF), acc[blk:2 * blk].astype(BF)]
        return jnp.concatenate(outs, axis=1)

    group = 4 if nb % 4 == 0 else (2 if nb % 2 == 0 else 1)

    def blocks(g, carry):
        def retire(q0, kstart, scores):
            dist = rel + (q0 - kstart)
            valid = (dist <= WINDOW) & (dist >= -WINDOW)
            o_ref[pl.ds(q0, blk), :] = finish(scores, kstart, valid)

        pending = []
        for n in [g * group + u for u in range(group)]:
            q0 = pl.multiple_of(n * blk, blk)
            kstart = pl.multiple_of(jnp.clip((n - 1) * blk, 0, S - win), blk)
            pending.append((q0, kstart, issue(q_ref[pl.ds(q0, blk), :], kstart)))
            if len(pending) > 2:
                retire(*pending.pop(0))
        for item in pending:
            retire(*item)
        return carry

    lax.fori_loop(0, nb // group, blocks, 0)
    if ctx_queries:
        for n in range(Lc // blk):
            ox_ref[n * blk:(n + 1) * blk, :] = finish(issue(qx_ref[n * blk:(n + 1) * blk, :], None), None, None)


def _gqa(q, k, v, sink, lay, ctx_queries):
    B, S, Lc = lay.B, lay.S, lay.Lc
    W = GQA_HQ * GQA_DH
    lat = lambda w: pl.BlockSpec((S, w), lambda b: (lay.lat_seq_block + b, 0))
    ctx = lambda w: pl.BlockSpec((Lc, w), lambda b: (b, 0))
    in_specs = [pl.BlockSpec(memory_space=pltpu.SMEM), lat(W), lat(LANES), lat(LANES), ctx(LANES), ctx(LANES)]
    out_specs = [pl.BlockSpec((S, W), lambda b: (b, 0))]
    out_shape = [jax.ShapeDtypeStruct((lay.n_lat, W), BF)]
    args = [sink, q, k, v, k, v]
    if ctx_queries:
        in_specs.append(ctx(W))
        out_specs.append(ctx(W))
        out_shape.append(jax.ShapeDtypeStruct((lay.n_ctx, W), BF))
        args.append(q)
    return pl.pallas_call(
        functools.partial(_gqa_kernel, S=S, Lc=Lc, ctx_queries=ctx_queries),
        grid=(B,),
        in_specs=in_specs,
        out_specs=out_specs,
        out_shape=out_shape,
        scratch_shapes=[pltpu.VMEM((4, S, LANES), BF), pltpu.VMEM((4, S, LANES), BF),
                        pltpu.VMEM((4, Lc, LANES), BF), pltpu.VMEM((4, Lc, LANES), BF)],
        compiler_params=_params(1),
        name="gqa_window",
    )(*args)


def _mla_heads(q_ref, kv_refs, o_ref):
    lane = lax.broadcasted_iota(jnp.int32, (1, LANES), 1)
    low = lane < MLA_DV

    def head_scores(h):
        hsl = slice(h * LANES, (h + 1) * LANES)
        qh = q_ref[:, hsl]
        return [_dot_nt(qh, k_ref[:, hsl]) for k_ref, _ in kv_refs]

    ahead = 2
    upcoming = [head_scores(h) for h in range(ahead)]
    for pair in range(MLA_H // 2):
        vsl = slice(pair * LANES, (pair + 1) * LANES)
        acc = None
        for half in range(2):
            h = 2 * pair + half
            scores = upcoming.pop(0)
            if h + ahead < MLA_H:
                upcoming.append(head_scores(h + ahead))
            m = functools.reduce(jnp.maximum, [jnp.max(s, axis=-1, keepdims=True) for s in scores])
            keep = low if half == 0 else jnp.logical_not(low)
            ones_lane = LANES - 1 if half == 0 else 0
            t = None
            for s, (_, v_ref) in zip(scores, kv_refs):
                vh = jnp.where(keep, v_ref[:, vsl], jnp.zeros((), BF))
                vh = jnp.where(lane == ones_lane, jnp.ones((), BF), vh)
                u = _dot(jnp.exp2(s - m).astype(BF), vh)
                t = u if t is None else t + u
            o = jnp.where(keep, t * (1.0 / t[:, ones_lane:ones_lane + 1]), 0.0)
            acc = o if acc is None else acc + o
        o_ref[:, vsl] = acc.astype(BF)


def _mla_lat_kernel(q_ref, kc_ref, kl_ref, vc_ref, vl_ref, o_ref):
    _mla_heads(q_ref, ((kc_ref, vc_ref), (kl_ref, vl_ref)), o_ref)


def _mla_ctx_kernel(q_ref, kc_ref, vc_ref, o_ref):
    _mla_heads(q_ref, ((kc_ref, vc_ref),), o_ref)


def _mla_latent(q, k, v, lay):
    B, S, Lc = lay.B, lay.S, lay.Lc
    tq = min(512, S)
    nq = S // tq
    lat_q0 = lay.n_ctx // tq
    kw, vw = MLA_H * LANES, MLA_H * MLA_DV
    return pl.pallas_call(
        _mla_lat_kernel,
        grid=(B, nq),
        in_specs=[pl.BlockSpec((tq, kw), lambda b, i: (lat_q0 + b * nq + i, 0)),
                  pl.BlockSpec((Lc, kw), lambda b, i: (b, 0)),
                  pl.BlockSpec((S, kw), lambda b, i: (lay.lat_seq_block + b, 0)),
                  pl.BlockSpec((Lc, vw), lambda b, i: (b, 0)),
                  pl.BlockSpec((S, vw), lambda b, i: (lay.lat_seq_block + b, 0))],
        out_specs=pl.BlockSpec((tq, vw), lambda b, i: (b * nq + i, 0)),
        out_shape=jax.ShapeDtypeStruct((lay.n_lat, vw), BF),
        compiler_params=_params(2),
        name="mla_latent",
    )(q, k, k, v, v)


def _mla_context(q, k, v, lay):
    B, Lc = lay.B, lay.Lc
    kw, vw = MLA_H * LANES, MLA_H * MLA_DV
    blk = lambda w: pl.BlockSpec((Lc, w), lambda b: (b, 0))
    return pl.pallas_call(
        _mla_ctx_kernel,
        grid=(B,),
        in_specs=[blk(kw), blk(kw), blk(vw)],
        out_specs=blk(vw),
        out_shape=jax.ShapeDtypeStruct((lay.n_ctx, vw), BF),
        compiler_params=_params(1),
        name="mla_context",
    )(q, k, v)


def _ret_chunk(Lc, S):
    return 256 if Lc % 256 == 0 and S % 256 == 0 else RET_CHUNK


def _ret_kernel(dec_ref, qc_ref, ql_ref, kc_ref, kl_ref, vc_ref, vl_ref, gc_ref, gl_ref, gn_ref,
                oc_ref, ol_ref, kv_s, r_s, *, Lc, S, ch):
    nc, nl = Lc // ch, S // ch
    n = nc + nl
    lg_f = jnp.log(_sigmoid(dec_ref[0, 0:1, :]))
    lg_b = jnp.log(_sigmoid(dec_ref[0, 1:2, :]))
    pos_i = lax.broadcasted_iota(jnp.int32, (ch, ch), 0).astype(F32)
    pos_j = lax.broadcasted_iota(jnp.int32, (ch, ch), 1).astype(F32)
    diff = pos_i - pos_j
    fwd = diff >= 0.0
    mask = jnp.where(fwd, jnp.exp(jnp.where(fwd, diff, 0.0) * lg_f), jnp.exp(jnp.where(fwd, 0.0, -diff) * lg_b))
    pos = pos_i[:, 0:1]
    lf, lb = lg_f[:, :LANES], lg_b[:, :LANES]
    zeta = (jnp.exp((ch - 1.0 - pos) * lf), jnp.exp(pos * lb))
    xi = (jnp.exp((pos + 1.0) * lf), jnp.exp((ch - pos) * lb))
    decay = (jnp.exp(ch * lf), jnp.exp(ch * lb))

    def rows(c_ref, l_ref, c):
        if c < nc:
            return c_ref[c * ch:(c + 1) * ch, :]
        return l_ref[(c - nc) * ch:(c - nc + 1) * ch, :]

    for c in range(n):
        k = rows(kc_ref, kl_ref, c).astype(F32)
        kz = jnp.concatenate([(k * zeta[0]).astype(BF), (k * zeta[1]).astype(BF)], axis=1)
        kv_s[c] = _dot_tn(kz, rows(vc_ref, vl_ref, c))

    orders = (list(range(n)), list(range(nc - 1, -1, -1)) + list(range(n - 1, nc - 1, -1)))
    for d, order in enumerate(orders):
        R = jnp.zeros((RET_DK, RET_DV), F32)
        for c in order:
            r_s[c, d * RET_DK:(d + 1) * RET_DK, :] = R.astype(BF)
            R = decay[d] * R + kv_s[c, d * RET_DK:(d + 1) * RET_DK, :]

    qk = lambda c: _dot_nt(rows(qc_ref, ql_ref, c), rows(kc_ref, kl_ref, c))
    qk_next = qk(0)
    for c in range(n):
        q = rows(qc_ref, ql_ref, c)
        s = qk_next * mask
        if c + 1 < n:
            qk_next = qk(c + 1)
        qf = q.astype(F32)
        qx = jnp.concatenate([(qf * xi[0]).astype(BF), (qf * xi[1]).astype(BF)], axis=1)
        y = _dot(s.astype(BF), rows(vc_ref, vl_ref, c)) + _dot(qx, r_s[c])
        mu = jnp.mean(y, axis=-1, keepdims=True)
        yc = y - mu
        var = jnp.mean(yc * yc, axis=-1, keepdims=True)
        yn = yc * lax.rsqrt(var + LN_EPS) * gn_ref[...]
        hg = 0.5 * rows(gc_ref, gl_ref, c).astype(F32)
        out = ((hg + hg * jnp.tanh(hg)) * yn).astype(BF)
        if c < nc:
            oc_ref[c * ch:(c + 1) * ch, :] = out
        else:
            ol_ref[(c - nc) * ch:(c - nc + 1) * ch, :] = out


def _retention(dq, dk, dv, dg, decay_lanes, gn, lay):
    B, S, Lc = lay.B, lay.S, lay.Lc
    ch = _ret_chunk(Lc, S)
    n = (Lc + S) // ch
    ctx = pl.BlockSpec((Lc, LANES), lambda b, h: (b, h))
    lat = pl.BlockSpec((S, LANES), lambda b, h: (lay.lat_seq_block + b, h))
    return pl.pallas_call(
        functools.partial(_ret_kernel, Lc=Lc, S=S, ch=ch),
        grid=(B, RET_H),
        in_specs=[pl.BlockSpec((1, 2, ch), lambda b, h: (h, 0, 0)),
                  ctx, lat, ctx, lat, ctx, lat, ctx, lat, pl.BlockSpec((1, LANES), lambda b, h: (0, h))],
        out_specs=[pl.BlockSpec((Lc, LANES), lambda b, h: (b, h)), pl.BlockSpec((S, LANES), lambda b, h: (b, h))],
        out_shape=[jax.ShapeDtypeStruct((lay.n_ctx, RET_H * RET_DV), BF), jax.ShapeDtypeStruct((lay.n_lat, RET_H * RET_DV), BF)],
        scratch_shapes=[pltpu.VMEM((n, 2 * RET_DK, RET_DV), F32), pltpu.VMEM((n, 2 * RET_DK, RET_DV), BF)],
        compiler_params=_params(2),
        name="retention",
    )(decay_lanes, dq, dq, dk, dk, dv, dv, dg, dg, gn)


def _merge_kernel(*refs, n_br, ct):
    x_ref, mod_ref = refs[:2]
    br_refs = refs[2:2 + N_BRANCH * n_br]
    wg_ref, wb_ref, wo_ref, g_ref, beta_ref, o_ref = refs[2 + N_BRANCH * n_br:]
    x = x_ref[...]
    h = (x * (1.0 + mod_ref[0, 4:5, :]) + mod_ref[0, 3:4, :]).astype(BF)
    acc = None
    pre = lambda k: _dot(h, wg_ref[:, k * D_MODEL:(k + 1) * D_MODEL])
    z_next = pre(0)
    for k in range(N_BRANCH):
        z_gate = z_next
        if k + 1 < N_BRANCH:
            z_next = pre(k + 1)
        br = _rows(br_refs[k * n_br:(k + 1) * n_br], pl.program_id(0), ct)
        t = (0.5 * jnp.tanh(z_gate) + 0.5) * _dot(br, wb_ref[k])
        acc = t if acc is None else acc + t
    merged = acc.astype(BF)
    cut = x.shape[0] // 2
    parts = (slice(0, cut), slice(cut, x.shape[0]))
    ys = [_dot(merged[rows, :], wo_ref[...]) for rows in parts]
    for rows, y in zip(parts, ys):
        z = DN_ALPHA * x[rows, :] + mod_ref[0, 5:6, :] * y
        o_ref[rows, :] = _layer_norm(z, g_ref[...], beta_ref[...])


def _merge(x, mod, branches, w_gate, wb_all, wo_all, l, g, beta, lay, g_off):
    tm = lay.tm
    n = lay.tiles - g_off
    br_specs, br_args = [], []
    for br in branches:
        specs, args = _row_specs(br, BRANCH_W, lay, g_off)
        br_specs += specs
        br_args += args
    return pl.pallas_call(
        functools.partial(_merge_kernel, n_br=len(br_args) // N_BRANCH, ct=lay.ctx_tiles),
        grid=(n,),
        in_specs=[pl.BlockSpec((tm, D_MODEL), lambda i: (i + g_off, 0)),
                  pl.BlockSpec((1, 9, D_MODEL), lambda i: (lay.mod_index(i + g_off), 0, 0))] + br_specs + [
                  pl.BlockSpec((None,) + w_gate.shape[1:], lambda i: (l, 0, 0), pipeline_mode=pl.Buffered(1)),
                  pl.BlockSpec((None, N_BRANCH, BRANCH_W, D_MODEL), lambda i: (l, 0, 0, 0), pipeline_mode=pl.Buffered(1)),
                  pl.BlockSpec((None, D_MODEL, D_MODEL), lambda i: (l, 0, 0), pipeline_mode=pl.Buffered(1)),
                  _const((1, D_MODEL)), _const((1, D_MODEL))],
        out_specs=pl.BlockSpec((tm, D_MODEL), lambda i: (i, 0)),
        out_shape=jax.ShapeDtypeStruct((n * tm, D_MODEL), F32),
        compiler_params=_params(1),
        name="gated_merge",
    )(x, mod, *br_args, w_gate, wb_all, wo_all, g, beta)


def _rope_tables(S, head_dim, lead, reps, tail, tm):
    nf = head_dim // 4
    inv = ROPE_BASE ** (-np.arange(nf, dtype=np.float64) / nf)
    n_rows = S // GRID_W
    ar = np.arange(n_rows, dtype=np.float64)[:, None] * inv[None, :]
    ac = np.arange(GRID_W, dtype=np.float64)[:, None] * inv[None, :]

    def lanes(by_row, first, second, fill):
        n = first.shape[0]
        neutral = np.full((n, nf), fill)
        head = np.concatenate([first, second, neutral, neutral] if by_row else [neutral, neutral, first, second], -1)
        pad = lambda w: np.full((n, w), fill)
        return np.tile(np.concatenate([pad(lead), head, pad(tail)], -1), (1, reps))

    row_c = lanes(True, np.cos(ar), np.cos(ar), 1.0)
    row_s = lanes(True, -np.sin(ar), np.sin(ar), 0.0)
    col_c = lanes(False, np.cos(ac), np.cos(ac), 1.0)
    col_s = lanes(False, -np.sin(ac), np.sin(ac), 0.0)
    W = row_c.shape[1]
    per_tile = tm // GRID_W

    def by_tile(tab, fill):
        t = tab.reshape(S // tm, per_tile, W)
        t = np.concatenate([t, np.full((S // tm, SUBLANES - per_tile, W), fill)], 1).reshape(-1, W)
        return np.concatenate([t, np.full((SUBLANES, W), fill)], 0).astype(np.float32)

    ident = lambda fill: np.full((GRID_W, W), fill)
    return (by_tile(row_c, 1.0), by_tile(row_s, 0.0),
            np.stack([col_c, ident(1.0)]).astype(np.float32), np.stack([col_s, ident(0.0)]).astype(np.float32))


def _mixer_weights(w_in, lru_w_a, lru_b_a, lru_w_x, lru_b_x, mla_w_uq, mla_w_ukv):
    wb = w_in.astype(BF)
    cut = sum(IN_SIZES[:8])
    gate0 = sum(IN_SIZES[:12])
    assert cut + LANES - MLA_DR == _W_HEAD
    w_proj = (wb, wb[:, :, cut:gate0])
    w_gate = wb[:, :, gate0:] * jnp.asarray(0.5, BF)
    blocks = jnp.stack([lru_w_a[:, 0], lru_w_x[:, 0], lru_w_a[:, 1], lru_w_x[:, 1]], axis=1)
    eye = jnp.eye(LRU_BLOCKS, dtype=F32)
    dense = (0.5 * eye)[None, None, :, None, :, None] * blocks[:, :, :, :, None, :]
    wg = dense.reshape(DEPTH, 4, LRU_W, LRU_W).transpose(0, 2, 1, 3).reshape(DEPTH, LRU_W, 4 * LRU_W).astype(BF)
    bg = 0.5 * jnp.stack([lru_b_a[:, 0], lru_b_x[:, 0], lru_b_a[:, 1], lru_b_x[:, 1]], axis=1).reshape(DEPTH, 1, 4 * LRU_W)
    uq = mla_w_uq.reshape(DEPTH, MLA_RQ, MLA_H, MLA_DN + MLA_DR)
    wq = jnp.pad(uq, ((0, 0), (0, 0), (0, 0), (0, LANES - MLA_DN - MLA_DR))).reshape(DEPTH, MLA_RQ, MLA_H * LANES).astype(BF)
    ukv = mla_w_ukv.reshape(DEPTH, MLA_RKV, MLA_H, MLA_DN + MLA_DV)
    wk = jnp.pad(ukv[..., :MLA_DN], ((0, 0), (0, 0), (0, 0), (0, LANES - MLA_DN))).reshape(DEPTH, MLA_RKV, MLA_H * LANES).astype(BF)
    wv = ukv[..., MLA_DN:].reshape(DEPTH, MLA_RKV, MLA_H * MLA_DV).astype(BF)
    return w_proj, w_gate, wg, bg, wq, wk, wv


def kernel(x, c, ctx, c_ctx, w_mod, b_mod, ln_g, ln_b, ffn_w_in, ffn_w_out, w_in, lru_conv_w, lru_conv_b, lru_w_a, lru_b_a, lru_w_x, lru_b_x, lru_lambda, gqa_sink, mla_q_norm, mla_kv_norm, mla_w_uq, mla_w_ukv, ret_decay, ret_gn_g, w_branch, w_out):
    B, S, D = x.shape
    Lc = ctx.shape[1]
    lay = _Layout(B, S, Lc)
    tm = lay.tm

    n_cond = -(-(B + 1) // SUBLANES) * SUBLANES
    cond = jnp.concatenate([c, c_ctx[None, :], jnp.zeros((n_cond - B - 1, D), F32)], axis=0)
    mod_all = _modulation(cond, w_mod, b_mod)[:, :B + 1].reshape(DEPTH, B + 1, 9, D)

    tabs = [_rope_tables(S, head_dim, lead, reps, tail, tm) for head_dim, lead, reps, tail in
            ((GQA_DH, 0, GQA_HQ, 0), (GQA_DH, 0, GQA_HKV, 0), (MLA_DR, 0, 1, LANES - MLA_DR), (RET_DK, 0, RET_H, 0),
             (MLA_DR, MLA_DN, MLA_H, LANES - MLA_DN - MLA_DR))]

    tok = (ctx.reshape(B * Lc, D), x.reshape(B * S, D))
    row = lambda v: v[None, :]
    ct = lay.ctx_tiles
    w1_all, w2_all = ffn_w_in.astype(BF), ffn_w_out.astype(BF)
    wb_all, wo_all = w_branch.astype(BF), w_out.astype(BF)
    w_proj, w_gate, wg, bg, wq, wk, wv = _mixer_weights(w_in, lru_w_a, lru_b_a, lru_w_x, lru_b_x, mla_w_uq, mla_w_ukv)

    for l in range(DEPTH):
        last = l == DEPTH - 1
        mod = mod_all[l]
        dec = jnp.broadcast_to(ret_decay[l].T[:, :, None], (RET_H, 2, _ret_chunk(Lc, S)))

        tok = _half_ffn(tok, mod, w1_all, w2_all, l, 0, row(ln_g[l, 0]), row(ln_b[l, 0]), lay, 0, 0)

        ax, ay, bq, bk, bv, mq, mk, mv, dq, dk, dv, dg = _mixer_proj(
            tok, mod, w_proj, l, wq[l], wk[l], wv[l], row(mla_q_norm[l]), row(mla_kv_norm[l]), tabs, lay)
        a_c, a_l = _lru(ax, ay, lru_conv_w[l], row(lru_conv_b[l]), wg[l], bg[l], lru_lambda[l], lay)
        b_out = _gqa(bq, bk, bv, gqa_sink[l], lay, ctx_queries=not last)
        c_l = _mla_latent(mq, mk, mv, lay)
        d_c, d_l = _retention(dq, dk, dv, dg, dec, row(ret_gn_g[l]), lay)

        if last:
            branches = (a_l, b_out[0], c_l, d_l)
            g_off = ct
        else:
            branches = ((a_c, a_l), (b_out[1], b_out[0]), (_mla_context(mq, mk, mv, lay), c_l), (d_c, d_l))
            g_off = 0
        tok = _merge(tok, mod, branches, w_gate, wb_all, wo_all, l, row(ln_g[l, 1]), row(ln_b[l, 1]), lay, g_off)
        tok = _half_ffn(tok, mod, w1_all, w2_all, l, 1, row(ln_g[l, 2]), row(ln_b[l, 2]), lay, 6, g_off)

    return tok.reshape(B, S, D)
```

```python
import functools

import jax
import jax.numpy as jnp
import numpy as np
from jax import lax
from jax.experimental import pallas as pl
from jax.experimental.pallas import tpu as pltpu

D_MODEL = 1024
DEPTH = 2
GRID_W = 64
N_BRANCH = 4
BRANCH_W = 512
LRU_W = 512
LRU_BLOCKS = 8
LRU_BW = LRU_W // LRU_BLOCKS
LRU_C = 8.0
GQA_HQ = 8
GQA_HKV = 2
GQA_DH = 64
WINDOW = 128
ATT_BLOCK = 128
MLA_H = 8
MLA_RQ = 384
MLA_RKV = 256
MLA_DN = 64
MLA_DR = 32
MLA_DV = 64
RET_H = 4
RET_DK = 128
RET_DV = 128
RET_CHUNK = 128
FF = 2816
ROPE_BASE = 10000.0
LN_EPS = 1e-5
MASK_VALUE = -1e30
DN_ALPHA = (2 * DEPTH) ** 0.25
IN_SIZES = (LRU_W, LRU_W, GQA_HQ * GQA_DH, GQA_HKV * GQA_DH, GQA_HKV * GQA_DH, MLA_RQ, MLA_RKV, MLA_DR,
            RET_H * RET_DK, RET_H * RET_DK, RET_H * RET_DV, RET_H * RET_DV, N_BRANCH * D_MODEL)

LANES = 128
SUBLANES = 8
VMEM_LIMIT = 56 * 1024 * 1024
MXU_DIM = 256
FFN_CHUNKS = (6 * MXU_DIM, 5 * MXU_DIM)
LOG2E = 1.4426950408889634
BF = jnp.bfloat16
F32 = jnp.float32


def _params(n_axes):
    return pltpu.CompilerParams(dimension_semantics=("arbitrary",) * n_axes, vmem_limit_bytes=VMEM_LIMIT)


def _resident(shape):
    nd = len(shape)
    return pl.BlockSpec(shape, lambda *_: (0,) * nd, pipeline_mode=pl.Buffered(1))


def _const(shape):
    nd = len(shape)
    return pl.BlockSpec(shape, lambda *_: (0,) * nd)


def _layer_norm(z, g, b):
    mu = jnp.mean(z, axis=-1, keepdims=True)
    zc = z - mu
    var = jnp.mean(zc * zc, axis=-1, keepdims=True)
    return zc * lax.rsqrt(var + LN_EPS) * g + b


def _sigmoid(x):
    return 1.0 / (1.0 + jnp.exp(-x))


def _dot(a, b):
    return jnp.dot(a, b, preferred_element_type=F32)


def _dot_nt(a, b):
    return lax.dot_general(a, b, (((1,), (1,)), ((), ())), preferred_element_type=F32)


def _dot_tn(a, b):
    return lax.dot_general(a, b, (((0,), (0,)), ((), ())), preferred_element_type=F32)


def _rope(y, c, s, nf):
    w = y.shape[-1]
    lane = lax.broadcasted_iota(jnp.int32, y.shape, 1)
    partner = jnp.where((lane % (2 * nf)) < nf, pltpu.roll(y, w - nf, axis=1), pltpu.roll(y, nf, axis=1))
    return y * c + partner * s


class _Layout:
    def __init__(self, B, S, Lc):
        self.B, self.S, self.Lc = B, S, Lc
        self.n_ctx, self.n_lat = B * Lc, B * S
        self.n_tok = self.n_ctx + self.n_lat
        tm = 512
        while S % tm or self.n_ctx % tm:
            tm //= 2
        self.tm = tm
        self.ctx_tiles = self.n_ctx // tm
        self.tiles_per_seq = S // tm
        self.tiles = self.ctx_tiles + B * self.tiles_per_seq
        assert self.n_ctx % S == 0, "latent rows must start on a whole-sequence block"
        self.lat_seq_block = self.n_ctx // S
        self.lat_att_block = self.n_ctx // ATT_BLOCK

    def mod_index(self, g):
        return jnp.where(g < self.ctx_tiles, self.B, (g - self.ctx_tiles) // self.tiles_per_seq)

    def rope_index(self, g):
        return jnp.where(g < self.ctx_tiles, self.tiles_per_seq, (g - self.ctx_tiles) % self.tiles_per_seq)


def _mod_kernel(s_ref, w_ref, b_ref, o_ref):
    s = s_ref[...]
    s = s * _sigmoid(s)
    o_ref[0] = _dot(s.astype(BF), w_ref[0].astype(BF)) + b_ref[0]


def _modulation(cond, w_mod, b_mod):
    R = cond.shape[0]
    tn = 1024
    n9 = 9 * D_MODEL
    return pl.pallas_call(
        _mod_kernel,
        grid=(DEPTH, n9 // tn),
        in_specs=[pl.BlockSpec((R, D_MODEL), lambda l, j: (0, 0)),
                  pl.BlockSpec((1, D_MODEL, tn), lambda l, j: (l, 0, j)),
                  pl.BlockSpec((1, 1, tn), lambda l, j: (l, 0, j))],
        out_specs=pl.BlockSpec((1, R, tn), lambda l, j: (l, 0, j)),
        out_shape=jax.ShapeDtypeStruct((DEPTH, R, n9), F32),
        compiler_params=_params(2),
        name="modulation",
    )(cond, w_mod, b_mod.reshape(DEPTH, 1, n9))


def _row_specs(src, width, lay, g_off, tile=lambda i: i):
    tm, ct = lay.tm, lay.ctx_tiles
    if isinstance(src, tuple):
        assert g_off == 0
        return [pl.BlockSpec((tm, width), lambda i: (jnp.minimum(tile(i), ct - 1), 0)),
                pl.BlockSpec((tm, width), lambda i: (jnp.maximum(tile(i) - ct, 0), 0))], list(src)
    assert src.shape[0] == lay.n_tok or (src.shape[0] == lay.n_lat and g_off == ct)
    off = g_off if src.shape[0] == lay.n_tok else 0
    return [pl.BlockSpec((tm, width), lambda i: (tile(i) + off, 0))], [src]


def _rows(refs, t, ct):
    if len(refs) == 2:
        return jnp.where(t < ct, refs[0][...], refs[1][...])
    return refs[0][...]


def _ffn_kernel(*refs, k0, n_x, ct):
    mod_ref, w1_ref, w2_ref, g_ref, b_ref, o_ref = refs[n_x:]
    x = _rows(refs[:n_x], pl.program_id(0), ct)
    shift = mod_ref[0, k0:k0 + 1, :]
    scale = mod_ref[0, k0 + 1:k0 + 2, :]
    gate = mod_ref[0, k0 + 2:k0 + 3, :]
    xm = (x * (1.0 + scale) + shift).astype(BF)
    assert sum(FFN_CHUNKS) == FF
    lo = 0
    up = []
    for width in FFN_CHUNKS:
        up.append((lo, width, _dot(xm, w1_ref[:, lo:lo + width]), _dot(xm, w1_ref[:, FF + lo:FF + lo + width])))
        lo += width
    hidden = [(lo, width, (a * _sigmoid(a) * b).astype(BF)) for lo, width, a, b in up]
    cut = x.shape[0] // 2
    parts = (slice(0, cut), slice(cut, x.shape[0]))
    down = [sum(_dot(h[rows, :], w2_ref[lo:lo + width, :]) for lo, width, h in hidden) for rows in parts]
    for rows, y in zip(parts, down):
        z = DN_ALPHA * x[rows, :] + (0.5 * gate) * y
        o_ref[rows, :] = _layer_norm(z, g_ref[...], b_ref[...])


def _half_ffn(x, mod, w1_all, w2_all, l, j, g, b, lay, k0, g_off):
    tm = lay.tm
    n = lay.tiles - g_off
    x_specs, x_args = _row_specs(x, D_MODEL, lay, g_off)
    pick = lambda *_: (l, j, 0, 0)
    return pl.pallas_call(
        functools.partial(_ffn_kernel, k0=k0, n_x=len(x_args), ct=lay.ctx_tiles),
        grid=(n,),
        in_specs=x_specs + [
            pl.BlockSpec((1, 9, D_MODEL), lambda i: (lay.mod_index(i + g_off), 0, 0)),
            pl.BlockSpec((None, None, D_MODEL, 2 * FF), pick, pipeline_mode=pl.Buffered(1)),
            pl.BlockSpec((None, None, FF, D_MODEL), pick, pipeline_mode=pl.Buffered(1)),
            _const((1, D_MODEL)), _const((1, D_MODEL))],
        out_specs=pl.BlockSpec((tm, D_MODEL), lambda i: (i, 0)),
        out_shape=jax.ShapeDtypeStruct((n * tm, D_MODEL), F32),
        compiler_params=_params(1),
        name="half_ffn",
    )(*x_args, mod, w1_all, w2_all, g, b)


_W_LRU, _W_GQ, _W_GKV, _W_RET = LRU_W, GQA_HQ * GQA_DH, GQA_HKV * GQA_DH, RET_H * RET_DK
_OFF_AX, _OFF_AY, _OFF_BQ, _OFF_BKV = 0, 512, 1024, 1536
_OFF_MLA, _W_MLA = 1792, MLA_RQ + MLA_RKV + LANES
_W_HEAD = _OFF_MLA + _W_MLA
_OFF_DQ, _OFF_DK, _OFF_DV, _OFF_DG = 0, 512, 1024, 1536
_PROJ_OUT_WIDTHS = (_W_LRU, _W_LRU, _W_GQ, _W_GKV, _W_GKV, MLA_H * LANES, MLA_H * LANES, MLA_H * MLA_DV,
                    _W_RET, _W_RET, _W_RET, _W_RET)
_N_ROPE_TABLES = 5


def _rope_tile(y, tab, nf):
    rc_ref, rs_ref, cc_ref, cs_ref = tab
    parts = []
    for g in range(y.shape[0] // GRID_W):
        c = rc_ref[g:g + 1, :] * cc_ref[...]
        s = rs_ref[g:g + 1, :] + cs_ref[...]
        parts.append(_rope(y[g * GRID_W:(g + 1) * GRID_W, :], c, s, nf))
    return jnp.concatenate(parts, axis=0)


def _rms(x, g):
    return x * lax.rsqrt(jnp.mean(x * x, axis=-1, keepdims=True) + LN_EPS) * g


def _proj_kernel(x_ref, mod_ref, w_ref, wt_ref, wq_ref, wk_ref, wv_ref, gq_ref, gkv_ref, *refs):
    tabs = [refs[4 * i:4 * i + 4] for i in range(_N_ROPE_TABLES)]
    t_gq, t_gk, t_kr, t_ret, t_mq = tabs
    ax_o, ay_o, bq_o, bk_o, bv_o, mq_o, mk_o, mv_o, dq_o, dk_o, dv_o, dg_o = refs[4 * _N_ROPE_TABLES:]
    x = x_ref[...]
    h = (x * (1.0 + mod_ref[0, 4:5, :]) + mod_ref[0, 3:4, :]).astype(BF)
    proj = lambda lo, width: _dot(h, w_ref[:, lo:lo + width])
    tail = lambda lo, width: _dot(h, wt_ref[:, lo:lo + width])

    c = proj(_OFF_MLA, _W_MLA)
    y_bq = proj(_OFF_BQ, _W_GQ)
    yq = _rms(c[:, :MLA_RQ], gq_ref[...]).astype(BF)
    ykv = _rms(c[:, MLA_RQ:MLA_RQ + MLA_RKV], gkv_ref[...]).astype(BF)
    y_mq = _dot(yq, wq_ref[...])
    bq_o[...] = (_rope_tile(y_bq, t_gq, GQA_DH // 4) * (GQA_DH ** -0.5 * LOG2E)).astype(BF)
    y_mk = _dot(ykv, wk_ref[...])
    mq_o[...] = (_rope_tile(y_mq, t_mq, MLA_DR // 4) * ((MLA_DN + MLA_DR) ** -0.5 * LOG2E)).astype(BF)
    y_mv = _dot(ykv, wv_ref[...])
    kr = c[:, MLA_RQ + MLA_RKV:]
    kr = jnp.where(lax.broadcasted_iota(jnp.int32, kr.shape, 1) < MLA_DR, kr, 0.0)
    kr = _rope_tile(kr, t_kr, MLA_DR // 4)
    kr_all = pltpu.roll(jnp.concatenate([kr] * MLA_H, axis=1), MLA_DN, axis=1)
    mk_o[...] = (y_mk + kr_all).astype(BF)
    y_dq = tail(_OFF_DQ, _W_RET)
    mv_o[...] = y_mv.astype(BF)
    y_dk = tail(_OFF_DK, _W_RET)
    dq_o[...] = _rope_tile(y_dq, t_ret, RET_DK // 4).astype(BF)
    kv = proj(_OFF_BKV, 2 * _W_GKV)
    dk_o[...] = (_rope_tile(y_dk, t_ret, RET_DK // 4) * RET_DK ** -0.5).astype(BF)
    y_ax = proj(_OFF_AX, _W_LRU)
    bk_o[...] = _rope_tile(kv[:, :_W_GKV], t_gk, GQA_DH // 4).astype(BF)
    bv_o[...] = kv[:, _W_GKV:].astype(BF)
    y_ay = proj(_OFF_AY, _W_LRU)
    ax_o[...] = y_ax.astype(BF)
    y_dv = tail(_OFF_DV, _W_RET)
    ay_o[...] = y_ay.astype(BF)
    y_dg = tail(_OFF_DG, _W_RET)
    dv_o[...] = y_dv.astype(BF)
    dg_o[...] = y_dg.astype(BF)


def _rope_specs(tab, lay):
    w = tab[0].shape[1]
    row = pl.BlockSpec((SUBLANES, w), lambda i: (lay.rope_index(i), 0))
    col = pl.BlockSpec((None, GRID_W, w), lambda i: (jnp.where(i < lay.ctx_tiles, 1, 0), 0, 0))
    return [row, row, col, col]


def _mixer_proj(x, mod, w_all, l, wq, wk, wv, gq, gkv, tabs, lay):
    tm = lay.tm
    row = lambda i: (i, 0)
    w_head, w_tail = w_all
    assert len(tabs) == _N_ROPE_TABLES
    tab_specs, tab_args = [], []
    for tab in tabs:
        tab_specs += _rope_specs(tab, lay)
        tab_args += list(tab)
    return pl.pallas_call(
        _proj_kernel,
        grid=(lay.tiles,),
        in_specs=[pl.BlockSpec((tm, D_MODEL), row),
                  pl.BlockSpec((1, 9, D_MODEL), lambda i: (lay.mod_index(i), 0, 0)),
                  pl.BlockSpec((None, D_MODEL, _W_HEAD), lambda i: (l, 0, 0), pipeline_mode=pl.Buffered(1)),
                  pl.BlockSpec((None,) + w_tail.shape[1:], lambda i: (l, 0, 0), pipeline_mode=pl.Buffered(1)),
                  _const(wq.shape), _const(wk.shape), _const(wv.shape), _const(gq.shape), _const(gkv.shape)] + tab_specs,
        out_specs=[pl.BlockSpec((tm, w), row) for w in _PROJ_OUT_WIDTHS],
        out_shape=[jax.ShapeDtypeStruct((lay.n_tok, w), BF) for w in _PROJ_OUT_WIDTHS],
        compiler_params=_params(1),
        name="mixer_proj",
    )(x, mod, w_head, w_tail, wq, wk, wv, gq, gkv, *tab_args)


def _conv4(xp_s, base, n, w_ref, b_ref):
    y = b_ref[...] + w_ref[2:3, :] * xp_s[base:base + n, :]
    y = y + w_ref[0:1, :] * xp_s[base - 2:base - 2 + n, :]
    y = y + w_ref[1:2, :] * xp_s[base - 1:base - 1 + n, :]
    y = y + w_ref[3:4, :] * xp_s[base + 1:base + 1 + n, :]
    return y


def _scan8(a, b, reverse):
    row = lax.broadcasted_iota(jnp.int32, a.shape, 0)
    for s in (1, 2, 4):
        if reverse:
            keep = row < SUBLANES - s
            a_sh = pltpu.roll(a, SUBLANES - s, axis=0)
            b_sh = pltpu.roll(b, SUBLANES - s, axis=0)
        else:
            keep = row >= s
            a_sh = pltpu.roll(a, s, axis=0)
            b_sh = pltpu.roll(b, s, axis=0)
        b = jnp.where(keep, b + a * b_sh, b)
        a = jnp.where(keep, a * a_sh, a)
    return a, b


def _lru_segment(T):
    for seg in (36, 44, 28, 20, 12, 52, 60, 4, 8, 16, 32):
        if T % (SUBLANES * seg) == 0:
            return seg
    raise ValueError(f"no scan segment length for {T} rows")


def _lru_kernel(axc_ref, axl_ref, ayc_ref, ayl_ref, cw_ref, cb_ref, wg_ref, bg_ref, lam_ref,
                oc_ref, ol_ref, xp_s, xa_s, a_s, b_s, h0_s, h1_s, ac_s, bc_s, *, Lc, S, rows, seg):
    W = LRU_W
    T = Lc + S
    NS = W // LANES
    G = SUBLANES * seg
    P = SUBLANES
    zpad = jnp.zeros((P, W), F32)
    xp_s[0:P, :] = zpad
    xp_s[P:P + Lc, :] = axc_ref[...].astype(F32)
    xp_s[P + Lc:2 * P + Lc, :] = zpad
    xp_s[2 * P + Lc:2 * P + T, :] = axl_ref[...].astype(F32)
    xp_s[2 * P + T:3 * P + T, :] = zpad
    xa_s[0:Lc, :] = _conv4(xp_s, P, Lc, cw_ref, cb_ref)
    xa_s[Lc:T, :] = _conv4(xp_s, 2 * P + Lc, S, cw_ref, cb_ref)
    row8 = lax.broadcasted_iota(jnp.int32, (SUBLANES, LANES), 0)
    zero8 = jnp.zeros((SUBLANES, LANES), F32)

    for d in range(2):
        reverse = d == 1
        h_s = h1_s if reverse else h0_s
        neg = -lam_ref[d:d + 1, :]
        softplus = jnp.maximum(neg, 0.0) + jnp.log(1.0 + jnp.exp(-jnp.abs(neg)))
        c = (-0.5 * LRU_C * 1.4426950408889634) * softplus

        lo = 2 * d * W

        def gate_matmuls(i):
            xa = xa_s[i * rows:(i + 1) * rows, :]
            xb = xa.astype(BF)
            return xa, _dot(xb, wg_ref[:, lo:lo + W]), _dot(xb, wg_ref[:, lo + W:lo + 2 * W])

        upcoming = gate_matmuls(0)
        for i in range(T // rows):
            xa, z_r, z_i = upcoming
            if i + 1 < T // rows:
                upcoming = gate_matmuls(i + 1)
            r0 = i * rows
            dst = (r0 + S if r0 < Lc else r0 - Lc) if reverse else r0
            t_r = jnp.tanh(z_r + bg_ref[:, lo:lo + W])
            t_i = jnp.tanh(z_i + bg_ref[:, lo + W:lo + 2 * W])
            a = jnp.exp2(c * t_r + c)
            y = 1.0 - a * a
            root = jnp.where(y > 0.0, y * lax.rsqrt(y), 0.0)
            b = (root * (0.5 * xa)) * (t_i + 1.0)
            for k in range(NS):
                a_s[k, dst:dst + rows, :] = a[:, k * LANES:(k + 1) * LANES]
                b_s[k, dst:dst + rows, :] = b[:, k * LANES:(k + 1) * LANES]

        def group(j, h_in):
            base = ((T // G - 1 - j) if reverse else j) * G
            acc_a, acc_b = [None] * NS, [None] * NS
            for i in (range(seg - 1, -1, -1) if reverse else range(seg)):
                for k in range(NS):
                    a = a_s[k, pl.ds(base + i, SUBLANES, stride=seg), :]
                    b = b_s[k, pl.ds(base + i, SUBLANES, stride=seg), :]
                    if acc_a[k] is None:
                        acc_a[k], acc_b[k] = a, b
                    else:
                        acc_b[k] = a * acc_b[k] + b
                        acc_a[k] = a * acc_a[k]
                    ac_s[k, i] = acc_a[k]
                    bc_s[k, i] = acc_b[k]
            h_out, enter = [], []
            for k in range(NS):
                tot_a, tot_b = _scan8(acc_a[k], acc_b[k], reverse)
                after = tot_a * h_in[k] + tot_b
                if reverse:
                    enter.append(jnp.where(row8 == SUBLANES - 1, h_in[k], pltpu.roll(after, SUBLANES - 1, axis=0)))
                    h_out.append(jnp.broadcast_to(after[0:1, :], after.shape))
                else:
                    enter.append(jnp.where(row8 == 0, h_in[k], pltpu.roll(after, 1, axis=0)))
                    h_out.append(jnp.broadcast_to(after[SUBLANES - 1:SUBLANES, :], after.shape))
            for i in range(seg):
                for k in range(NS):
                    h_s[k, pl.ds(base + i, SUBLANES, stride=seg), :] = ac_s[k, i] * enter[k] + bc_s[k, i]
            return tuple(h_out)

        lax.fori_loop(0, T // G, group, (zero8,) * NS)

    def gelu(v):
        c = 0.7978845608028654
        hv = 0.5 * v
        return hv + hv * jnp.tanh(v * (c + (c * 0.044715) * (v * v)))

    for k in range(NS):
        sl = slice(k * LANES, (k + 1) * LANES)
        hc = h0_s[k, 0:Lc, :] + h1_s[k, S:T, :]
        hl = h0_s[k, Lc:T, :] + h1_s[k, 0:S, :]
        oc_ref[:, sl] = (hc * gelu(ayc_ref[:, sl].astype(F32))).astype(BF)
        ol_ref[:, sl] = (hl * gelu(ayl_ref[:, sl].astype(F32))).astype(BF)


def _lru(ax, ay, conv_w, conv_b, wg, bg, lam, lay):
    B, S, Lc = lay.B, lay.S, lay.Lc
    W = LRU_W
    T = Lc + S
    rows = 256
    while Lc % rows or S % rows:
        rows //= 2
    seg = _lru_segment(T)
    ctx = pl.BlockSpec((Lc, W), lambda b: (b, 0))
    lat = pl.BlockSpec((S, W), lambda b: (b + lay.lat_seq_block, 0))
    slabs = pltpu.VMEM((W // LANES, T, LANES), F32)
    part = pltpu.VMEM((W // LANES, seg, SUBLANES, LANES), F32)
    return pl.pallas_call(
        functools.partial(_lru_kernel, Lc=Lc, S=S, rows=rows, seg=seg),
        grid=(B,),
        in_specs=[ctx, lat, ctx, lat, _const((4, W)), _const((1, W)), _const(wg.shape), _const((1, 4 * W)), _const((2, W))],
        out_specs=[pl.BlockSpec((Lc, W), lambda b: (b, 0)), pl.BlockSpec((S, W), lambda b: (b, 0))],
        out_shape=[jax.ShapeDtypeStruct((lay.n_ctx, W), BF), jax.ShapeDtypeStruct((lay.n_lat, W), BF)],
        scratch_shapes=[pltpu.VMEM((T + 3 * SUBLANES, W), F32), pltpu.VMEM((T, W), F32), slabs, slabs, slabs, slabs, part, part],
        compiler_params=_params(1),
        name="rglru",
    )(ax, ax, ay, ay, conv_w, conv_b, wg, bg, lam)


def _split_heads_pair(x, ones_lane):
    xf = x.astype(F32)
    sw = pltpu.roll(xf, GQA_DH, axis=1)
    lane = lax.broadcasted_iota(jnp.int32, xf.shape, 1)
    low = lane < GQA_DH
    lo_fill = jnp.where(lane == LANES - 1, 1.0, 0.0) if ones_lane else jnp.zeros_like(xf)
    hi_fill = jnp.where(lane == 0, 1.0, 0.0) if ones_lane else jnp.zeros_like(xf)
    head0 = (jnp.where(low, xf, lo_fill).astype(BF), jnp.where(low, hi_fill, sw).astype(BF))
    head1 = (jnp.where(low, sw, lo_fill).astype(BF), jnp.where(low, hi_fill, xf).astype(BF))
    return head0, head1


def _gqa_kernel(sink_ref, *refs, S, Lc, ctx_queries):
    if ctx_queries:
        q_ref, k_ref, v_ref, kx_ref, vx_ref, qx_ref, o_ref, ox_ref, ks_s, vs_s, kxs_s, vxs_s = refs
    else:
        q_ref, k_ref, v_ref, kx_ref, vx_ref, o_ref, ks_s, vs_s, kxs_s, vxs_s = refs
    blk = ATT_BLOCK
    nb = S // blk
    win = min(3 * blk, S)
    for src, dst, is_value in ((k_ref, ks_s, False), (v_ref, vs_s, True), (kx_ref, kxs_s, False), (vx_ref, vxs_s, True)):
        heads = _split_heads_pair(src[...], is_value)
        for j in range(GQA_HKV):
            for half in range(2):
                dst[2 * j + half] = heads[j][half]

    row = lax.broadcasted_iota(jnp.int32, (2 * blk, win), 0)
    col = lax.broadcasted_iota(jnp.int32, (2 * blk, win), 1)
    rel = (row % blk) - col
    top = row[:, 0:1] < blk
    low = lax.broadcasted_iota(jnp.int32, (1, LANES), 1) < GQA_DH

    def issue(q, kstart):
        scores = []
        for j in range(GQA_HKV):
            q2 = jnp.concatenate([q[:, 2 * j * LANES:(2 * j + 1) * LANES], q[:, (2 * j + 1) * LANES:(2 * j + 2) * LANES]], axis=0)
            for half in range(2):
                i = 2 * j + half
                s_w = None if kstart is None else _dot_nt(q2, ks_s[i, pl.ds(kstart, win), :])
                scores.append((_dot_nt(q2, kxs_s[i]), s_w))
        return scores

    def finish(scores, kstart, valid):
        outs = []
        for j in range(GQA_HKV):
            acc = jnp.zeros((2 * blk, LANES), F32)
            for half in range(2):
                i = 2 * j + half
                sink = jnp.where(top, sink_ref[4 * j + half] * LOG2E, sink_ref[4 * j + 2 + half] * LOG2E)
                s_c, s_w = scores[i]
                m = jnp.maximum(jnp.max(s_c, axis=-1, keepdims=True), sink)
                if kstart is not None:
                    s_w = jnp.where(valid, s_w, MASK_VALUE)
                    m = jnp.maximum(m, jnp.max(s_w, axis=-1, keepdims=True))
                t = _dot(jnp.exp2(s_c - m).astype(BF), vxs_s[i])
                if kstart is not None:
                    t = t + _dot(jnp.exp2(s_w - m).astype(BF), vs_s[i, pl.ds(kstart, win), :])
                ones_lane = LANES - 1 if half == 0 else 0
                den = t[:, ones_lane:ones_lane + 1] + jnp.exp2(sink - m)
                acc = acc + jnp.where(low if half == 0 else jnp.logical_not(low), t * (1.0 / den), 0.0)
            outs += [acc[0:blk].astype(BF), acc[blk:2 * blk].astype(BF)]
        return jnp.concatenate(outs, axis=1)

    group = next(g for g in (8, 4, 2, 1) if nb % g == 0)

    def blocks(g, carry):
        def retire(q0, kstart, scores):
            dist = rel + (q0 - kstart)
            valid = (dist <= WINDOW) & (dist >= -WINDOW)
            o_ref[pl.ds(q0, blk), :] = finish(scores, kstart, valid)

        pending = []
        for n in [g * group + u for u in range(group)]:
            q0 = pl.multiple_of(n * blk, blk)
            kstart = pl.multiple_of(jnp.clip((n - 1) * blk, 0, S - win), blk)
            pending.append((q0, kstart, issue(q_ref[pl.ds(q0, blk), :], kstart)))
            if len(pending) > 2:
                retire(*pending.pop(0))
        for item in pending:
            retire(*item)
        return carry

    lax.fori_loop(0, nb // group, blocks, 0)
    if ctx_queries:
        for n in range(Lc // blk):
            ox_ref[n * blk:(n + 1) * blk, :] = finish(issue(qx_ref[n * blk:(n + 1) * blk, :], None), None, None)


def _gqa(q, k, v, sink, lay, ctx_queries):
    B, S, Lc = lay.B, lay.S, lay.Lc
    W = GQA_HQ * GQA_DH
    lat = lambda w: pl.BlockSpec((S, w), lambda b: (lay.lat_seq_block + b, 0))
    ctx = lambda w: pl.BlockSpec((Lc, w), lambda b: (b, 0))
    in_specs = [pl.BlockSpec(memory_space=pltpu.SMEM), lat(W), lat(LANES), lat(LANES), ctx(LANES), ctx(LANES)]
    out_specs = [pl.BlockSpec((S, W), lambda b: (b, 0))]
    out_shape = [jax.ShapeDtypeStruct((lay.n_lat, W), BF)]
    args = [sink, q, k, v, k, v]
    if ctx_queries:
        in_specs.append(ctx(W))
        out_specs.append(ctx(W))
        out_shape.append(jax.ShapeDtypeStruct((lay.n_ctx, W), BF))
        args.append(q)
    return pl.pallas_call(
        functools.partial(_gqa_kernel, S=S, Lc=Lc, ctx_queries=ctx_queries),
        grid=(B,),
        in_specs=in_specs,
        out_specs=out_specs,
        out_shape=out_shape,
        scratch_shapes=[pltpu.VMEM((4, S, LANES), BF), pltpu.VMEM((4, S, LANES), BF),
                        pltpu.VMEM((4, Lc, LANES), BF), pltpu.VMEM((4, Lc, LANES), BF)],
        compiler_params=_params(1),
        name="gqa_window",
    )(*args)


def _mla_heads(q_ref, kv_refs, o_ref):
    lane = lax.broadcasted_iota(jnp.int32, (1, LANES), 1)
    low = lane < MLA_DV

    def head_scores(h):
        hsl = slice(h * LANES, (h + 1) * LANES)
        qh = q_ref[:, hsl]
        return [_dot_nt(qh, k_ref[:, hsl]) for k_ref, _ in kv_refs]

    ahead = 2
    upcoming = [head_scores(h) for h in range(ahead)]
    for pair in range(MLA_H // 2):
        vsl = slice(pair * LANES, (pair + 1) * LANES)
        acc = None
        for half in range(2):
            h = 2 * pair + half
            scores = upcoming.pop(0)
            if h + ahead < MLA_H:
                upcoming.append(head_scores(h + ahead))
            m = functools.reduce(jnp.maximum, [jnp.max(s, axis=-1, keepdims=True) for s in scores])
            keep = low if half == 0 else jnp.logical_not(low)
            ones_lane = LANES - 1 if half == 0 else 0
            t = None
            for s, (_, v_ref) in zip(scores, kv_refs):
                vh = jnp.where(keep, v_ref[:, vsl], jnp.zeros((), BF))
                vh = jnp.where(lane == ones_lane, jnp.ones((), BF), vh)
                u = _dot(jnp.exp2(s - m).astype(BF), vh)
                t = u if t is None else t + u
            o = jnp.where(keep, t * (1.0 / t[:, ones_lane:ones_lane + 1]), 0.0)
            acc = o if acc is None else acc + o
        o_ref[:, vsl] = acc.astype(BF)


def _mla_lat_kernel(q_ref, kc_ref, kl_ref, vc_ref, vl_ref, o_ref):
    _mla_heads(q_ref, ((kc_ref, vc_ref), (kl_ref, vl_ref)), o_ref)


def _mla_ctx_kernel(q_ref, kc_ref, vc_ref, o_ref):
    _mla_heads(q_ref, ((kc_ref, vc_ref),), o_ref)


def _mla_latent(q, k, v, lay):
    B, S, Lc = lay.B, lay.S, lay.Lc
    tq = min(512, S)
    nq = S // tq
    lat_q0 = lay.n_ctx // tq
    kw, vw = MLA_H * LANES, MLA_H * MLA_DV
    return pl.pallas_call(
        _mla_lat_kernel,
        grid=(B, nq),
        in_specs=[pl.BlockSpec((tq, kw), lambda b, i: (lat_q0 + b * nq + i, 0)),
                  pl.BlockSpec((Lc, kw), lambda b, i: (b, 0)),
                  pl.BlockSpec((S, kw), lambda b, i: (lay.lat_seq_block + b, 0)),
                  pl.BlockSpec((Lc, vw), lambda b, i: (b, 0)),
                  pl.BlockSpec((S, vw), lambda b, i: (lay.lat_seq_block + b, 0))],
        out_specs=pl.BlockSpec((tq, vw), lambda b, i: (b * nq + i, 0)),
        out_shape=jax.ShapeDtypeStruct((lay.n_lat, vw), BF),
        compiler_params=_params(2),
        name="mla_latent",
    )(q, k, k, v, v)


def _mla_context(q, k, v, lay):
    B, Lc = lay.B, lay.Lc
    kw, vw = MLA_H * LANES, MLA_H * MLA_DV
    blk = lambda w: pl.BlockSpec((Lc, w), lambda b: (b, 0))
    return pl.pallas_call(
        _mla_ctx_kernel,
        grid=(B,),
        in_specs=[blk(kw), blk(kw), blk(vw)],
        out_specs=blk(vw),
        out_shape=jax.ShapeDtypeStruct((lay.n_ctx, vw), BF),
        compiler_params=_params(1),
        name="mla_context",
    )(q, k, v)


def _ret_chunk(Lc, S):
    return 256 if Lc % 256 == 0 and S % 256 == 0 else RET_CHUNK


def _ret_kernel(dec_ref, qc_ref, ql_ref, kc_ref, kl_ref, vc_ref, vl_ref, gc_ref, gl_ref, gn_ref,
                oc_ref, ol_ref, kv_s, r_s, *, Lc, S, ch):
    nc, nl = Lc // ch, S // ch
    n = nc + nl
    lg_f = jnp.log(_sigmoid(dec_ref[0, 0:1, :]))
    lg_b = jnp.log(_sigmoid(dec_ref[0, 1:2, :]))
    pos_i = lax.broadcasted_iota(jnp.int32, (ch, ch), 0).astype(F32)
    pos_j = lax.broadcasted_iota(jnp.int32, (ch, ch), 1).astype(F32)
    diff = pos_i - pos_j
    fwd = diff >= 0.0
    mask = jnp.where(fwd, jnp.exp(jnp.where(fwd, diff, 0.0) * lg_f), jnp.exp(jnp.where(fwd, 0.0, -diff) * lg_b))
    pos = pos_i[:, 0:1]
    lf, lb = lg_f[:, :LANES], lg_b[:, :LANES]
    zeta = (jnp.exp((ch - 1.0 - pos) * lf), jnp.exp(pos * lb))
    xi = (jnp.exp((pos + 1.0) * lf), jnp.exp((ch - pos) * lb))
    decay = (jnp.exp(ch * lf), jnp.exp(ch * lb))

    def rows(c_ref, l_ref, c):
        if c < nc:
            return c_ref[c * ch:(c + 1) * ch, :]
        return l_ref[(c - nc) * ch:(c - nc + 1) * ch, :]

    for c in range(n):
        k = rows(kc_ref, kl_ref, c).astype(F32)
        kz = jnp.concatenate([(k * zeta[0]).astype(BF), (k * zeta[1]).astype(BF)], axis=1)
        kv_s[c] = _dot_tn(kz, rows(vc_ref, vl_ref, c))

    orders = (list(range(n)), list(range(nc - 1, -1, -1)) + list(range(n - 1, nc - 1, -1)))
    for d, order in enumerate(orders):
        R = jnp.zeros((RET_DK, RET_DV), F32)
        for c in order:
            r_s[c, d * RET_DK:(d + 1) * RET_DK, :] = R.astype(BF)
            R = decay[d] * R + kv_s[c, d * RET_DK:(d + 1) * RET_DK, :]

    qk = lambda c: _dot_nt(rows(qc_ref, ql_ref, c), rows(kc_ref, kl_ref, c))
    qk_next = qk(0)
    for c in range(n):
        q = rows(qc_ref, ql_ref, c)
        s = qk_next * mask
        if c + 1 < n:
            qk_next = qk(c + 1)
        qf = q.astype(F32)
        qx = jnp.concatenate([(qf * xi[0]).astype(BF), (qf * xi[1]).astype(BF)], axis=1)
        y = _dot(s.astype(BF), rows(vc_ref, vl_ref, c)) + _dot(qx, r_s[c])
        mu = jnp.mean(y, axis=-1, keepdims=True)
        yc = y - mu
        var = jnp.mean(yc * yc, axis=-1, keepdims=True)
        yn = yc * lax.rsqrt(var + LN_EPS) * gn_ref[...]
        hg = 0.5 * rows(gc_ref, gl_ref, c).astype(F32)
        out = ((hg + hg * jnp.tanh(hg)) * yn).astype(BF)
        if c < nc:
            oc_ref[c * ch:(c + 1) * ch, :] = out
        else:
            ol_ref[(c - nc) * ch:(c - nc + 1) * ch, :] = out


def _retention(dq, dk, dv, dg, decay_lanes, gn, lay):
    B, S, Lc = lay.B, lay.S, lay.Lc
    ch = _ret_chunk(Lc, S)
    n = (Lc + S) // ch
    ctx = pl.BlockSpec((Lc, LANES), lambda b, h: (b, h))
    lat = pl.BlockSpec((S, LANES), lambda b, h: (lay.lat_seq_block + b, h))
    return pl.pallas_call(
        functools.partial(_ret_kernel, Lc=Lc, S=S, ch=ch),
        grid=(B, RET_H),
        in_specs=[pl.BlockSpec((1, 2, ch), lambda b, h: (h, 0, 0)),
                  ctx, lat, ctx, lat, ctx, lat, ctx, lat, pl.BlockSpec((1, LANES), lambda b, h: (0, h))],
        out_specs=[pl.BlockSpec((Lc, LANES), lambda b, h: (b, h)), pl.BlockSpec((S, LANES), lambda b, h: (b, h))],
        out_shape=[jax.ShapeDtypeStruct((lay.n_ctx, RET_H * RET_DV), BF), jax.ShapeDtypeStruct((lay.n_lat, RET_H * RET_DV), BF)],
        scratch_shapes=[pltpu.VMEM((n, 2 * RET_DK, RET_DV), F32), pltpu.VMEM((n, 2 * RET_DK, RET_DV), BF)],
        compiler_params=_params(2),
        name="retention",
    )(decay_lanes, dq, dq, dk, dk, dv, dv, dg, dg, gn)


def _merge_kernel(*refs, n_br, ct):
    x_ref, mod_ref = refs[:2]
    br_refs = refs[2:2 + N_BRANCH * n_br]
    wg_ref, wb_ref, wo_ref, g_ref, beta_ref, o_ref = refs[2 + N_BRANCH * n_br:]
    x = x_ref[...]
    h = (x * (1.0 + mod_ref[0, 4:5, :]) + mod_ref[0, 3:4, :]).astype(BF)
    acc = None
    pre = lambda k: _dot(h, wg_ref[:, k * D_MODEL:(k + 1) * D_MODEL])
    z_next = pre(0)
    for k in range(N_BRANCH):
        z_gate = z_next
        if k + 1 < N_BRANCH:
            z_next = pre(k + 1)
        br = _rows(br_refs[k * n_br:(k + 1) * n_br], pl.program_id(0), ct)
        t = (0.5 * jnp.tanh(z_gate) + 0.5) * _dot(br, wb_ref[k])
        acc = t if acc is None else acc + t
    merged = acc.astype(BF)
    cut = x.shape[0] // 2
    parts = (slice(0, cut), slice(cut, x.shape[0]))
    ys = [_dot(merged[rows, :], wo_ref[...]) for rows in parts]
    for rows, y in zip(parts, ys):
        z = DN_ALPHA * x[rows, :] + mod_ref[0, 5:6, :] * y
        o_ref[rows, :] = _layer_norm(z, g_ref[...], beta_ref[...])


def _merge(x, mod, branches, w_gate, wb_all, wo_all, l, g, beta, lay, g_off):
    tm = lay.tm
    n = lay.tiles - g_off
    br_specs, br_args = [], []
    for br in branches:
        specs, args = _row_specs(br, BRANCH_W, lay, g_off)
        br_specs += specs
        br_args += args
    return pl.pallas_call(
        functools.partial(_merge_kernel, n_br=len(br_args) // N_BRANCH, ct=lay.ctx_tiles),
        grid=(n,),
        in_specs=[pl.BlockSpec((tm, D_MODEL), lambda i: (i + g_off, 0)),
                  pl.BlockSpec((1, 9, D_MODEL), lambda i: (lay.mod_index(i + g_off), 0, 0))] + br_specs + [
                  pl.BlockSpec((None,) + w_gate.shape[1:], lambda i: (l, 0, 0), pipeline_mode=pl.Buffered(1)),
                  pl.BlockSpec((None, N_BRANCH, BRANCH_W, D_MODEL), lambda i: (l, 0, 0, 0), pipeline_mode=pl.Buffered(1)),
                  pl.BlockSpec((None, D_MODEL, D_MODEL), lambda i: (l, 0, 0), pipeline_mode=pl.Buffered(1)),
                  _const((1, D_MODEL)), _const((1, D_MODEL))],
        out_specs=pl.BlockSpec((tm, D_MODEL), lambda i: (i, 0)),
        out_shape=jax.ShapeDtypeStruct((n * tm, D_MODEL), F32),
        compiler_params=_params(1),
        name="gated_merge",
    )(x, mod, *br_args, w_gate, wb_all, wo_all, g, beta)


def _rope_tables(S, head_dim, lead, reps, tail, tm):
    nf = head_dim // 4
    inv = ROPE_BASE ** (-np.arange(nf, dtype=np.float64) / nf)
    n_rows = S // GRID_W
    ar = np.arange(n_rows, dtype=np.float64)[:, None] * inv[None, :]
    ac = np.arange(GRID_W, dtype=np.float64)[:, None] * inv[None, :]

    def lanes(by_row, first, second, fill):
        n = first.shape[0]
        neutral = np.full((n, nf), fill)
        head = np.concatenate([first, second, neutral, neutral] if by_row else [neutral, neutral, first, second], -1)
        pad = lambda w: np.full((n, w), fill)
        return np.tile(np.concatenate([pad(lead), head, pad(tail)], -1), (1, reps))

    row_c = lanes(True, np.cos(ar), np.cos(ar), 1.0)
    row_s = lanes(True, -np.sin(ar), np.sin(ar), 0.0)
    col_c = lanes(False, np.cos(ac), np.cos(ac), 1.0)
    col_s = lanes(False, -np.sin(ac), np.sin(ac), 0.0)
    W = row_c.shape[1]
    per_tile = tm // GRID_W

    def by_tile(tab, fill):
        t = tab.reshape(S // tm, per_tile, W)
        t = np.concatenate([t, np.full((S // tm, SUBLANES - per_tile, W), fill)], 1).reshape(-1, W)
        return np.concatenate([t, np.full((SUBLANES, W), fill)], 0).astype(np.float32)

    ident = lambda fill: np.full((GRID_W, W), fill)
    return (by_tile(row_c, 1.0), by_tile(row_s, 0.0),
            np.stack([col_c, ident(1.0)]).astype(np.float32), np.stack([col_s, ident(0.0)]).astype(np.float32))


def _mixer_weights(w_in, lru_w_a, lru_b_a, lru_w_x, lru_b_x, mla_w_uq, mla_w_ukv):
    wb = w_in.astype(BF)
    cut = sum(IN_SIZES[:8])
    gate0 = sum(IN_SIZES[:12])
    assert cut + LANES - MLA_DR == _W_HEAD
    w_proj = (wb, wb[:, :, cut:gate0])
    w_gate = wb[:, :, gate0:] * jnp.asarray(0.5, BF)
    blocks = jnp.stack([lru_w_a[:, 0], lru_w_x[:, 0], lru_w_a[:, 1], lru_w_x[:, 1]], axis=1)
    eye = jnp.eye(LRU_BLOCKS, dtype=F32)
    dense = (0.5 * eye)[None, None, :, None, :, None] * blocks[:, :, :, :, None, :]
    wg = dense.reshape(DEPTH, 4, LRU_W, LRU_W).transpose(0, 2, 1, 3).reshape(DEPTH, LRU_W, 4 * LRU_W).astype(BF)
    bg = 0.5 * jnp.stack([lru_b_a[:, 0], lru_b_x[:, 0], lru_b_a[:, 1], lru_b_x[:, 1]], axis=1).reshape(DEPTH, 1, 4 * LRU_W)
    uq = mla_w_uq.reshape(DEPTH, MLA_RQ, MLA_H, MLA_DN + MLA_DR)
    wq = jnp.pad(uq, ((0, 0), (0, 0), (0, 0), (0, LANES - MLA_DN - MLA_DR))).reshape(DEPTH, MLA_RQ, MLA_H * LANES).astype(BF)
    ukv = mla_w_ukv.reshape(DEPTH, MLA_RKV, MLA_H, MLA_DN + MLA_DV)
    wk = jnp.pad(ukv[..., :MLA_DN], ((0, 0), (0, 0), (0, 0), (0, LANES - MLA_DN))).reshape(DEPTH, MLA_RKV, MLA_H * LANES).astype(BF)
    wv = ukv[..., MLA_DN:].reshape(DEPTH, MLA_RKV, MLA_H * MLA_DV).astype(BF)
    return w_proj, w_gate, wg, bg, wq, wk, wv


def kernel(x, c, ctx, c_ctx, w_mod, b_mod, ln_g, ln_b, ffn_w_in, ffn_w_out, w_in, lru_conv_w, lru_conv_b, lru_w_a, lru_b_a, lru_w_x, lru_b_x, lru_lambda, gqa_sink, mla_q_norm, mla_kv_norm, mla_w_uq, mla_w_ukv, ret_decay, ret_gn_g, w_branch, w_out):
    B, S, D = x.shape
    Lc = ctx.shape[1]
    lay = _Layout(B, S, Lc)
    tm = lay.tm

    n_cond = -(-(B + 1) // SUBLANES) * SUBLANES
    cond = jnp.concatenate([c, c_ctx[None, :], jnp.zeros((n_cond - B - 1, D), F32)], axis=0)
    mod_all = _modulation(cond, w_mod, b_mod)[:, :B + 1].reshape(DEPTH, B + 1, 9, D)

    tabs = [_rope_tables(S, head_dim, lead, reps, tail, tm) for head_dim, lead, reps, tail in
            ((GQA_DH, 0, GQA_HQ, 0), (GQA_DH, 0, GQA_HKV, 0), (MLA_DR, 0, 1, LANES - MLA_DR), (RET_DK, 0, RET_H, 0),
             (MLA_DR, MLA_DN, MLA_H, LANES - MLA_DN - MLA_DR))]

    tok = (ctx.reshape(B * Lc, D), x.reshape(B * S, D))
    row = lambda v: v[None, :]
    ct = lay.ctx_tiles
    w1_all, w2_all = ffn_w_in.astype(BF), ffn_w_out.astype(BF)
    wb_all, wo_all = w_branch.astype(BF), w_out.astype(BF)
    w_proj, w_gate, wg, bg, wq, wk, wv = _mixer_weights(w_in, lru_w_a, lru_b_a, lru_w_x, lru_b_x, mla_w_uq, mla_w_ukv)

    for l in range(DEPTH):
        last = l == DEPTH - 1
        mod = mod_all[l]
        dec = jnp.broadcast_to(ret_decay[l].T[:, :, None], (RET_H, 2, _ret_chunk(Lc, S)))

        tok = _half_ffn(tok, mod, w1_all, w2_all, l, 0, row(ln_g[l, 0]), row(ln_b[l, 0]), lay, 0, 0)

        ax, ay, bq, bk, bv, mq, mk, mv, dq, dk, dv, dg = _mixer_proj(
            tok, mod, w_proj, l, wq[l], wk[l], wv[l], row(mla_q_norm[l]), row(mla_kv_norm[l]), tabs, lay)
        a_c, a_l = _lru(ax, ay, lru_conv_w[l], row(lru_conv_b[l]), wg[l], bg[l], lru_lambda[l], lay)
        b_out = _gqa(bq, bk, bv, gqa_sink[l], lay, ctx_queries=not last)
        c_l = _mla_latent(mq, mk, mv, lay)
        d_c, d_l = _retention(dq, dk, dv, dg, dec, row(ret_gn_g[l]), lay)

        if last:
            branches = (a_l, b_out[0], c_l, d_l)
            g_off = ct
        else:
            branches = ((a_c, a_l), (b_out[1], b_out[0]), (_mla_context(mq, mk, mv, lay), c_l), (d_c, d_l))
            g_off = 0
        tok = _merge(tok, mod, branches, w_gate, wb_all, wo_all, l, row(ln_g[l, 1]), row(ln_b[l, 1]), lay, g_off)
        tok = _half_ffn(tok, mod, w1_all, w2_all, l, 1, row(ln_g[l, 2]), row(ln_b[l, 2]), lay, 6, g_off)

    return tok.reshape(B, S, D)
```
